```python
import jax, jax.numpy as jnp
from jax import lax
import numpy as np

D_MODEL = 1024
BATCH = 32
SEQ = 256
DEPTH = 2
DEC_BATCH = 4
DEC_SEQ = 4096
PAST_LEN = 512

GRID_W = 64
N_HEADS = 8
N_KV_HEADS = 2
HEAD_DIM = 64
ML_HEADS = 4
ML_DIM = 64
RET_HEADS = 4
RET_DIM = 64
ATT_Q_W = N_HEADS * HEAD_DIM
ATT_KV_W = N_KV_HEADS * HEAD_DIM
ML_W = ML_HEADS * ML_DIM
RET_W = RET_HEADS * RET_DIM
MIX_W = ATT_Q_W + ML_W + RET_W
N_BRANCH = 3
IN_SIZES = (ATT_Q_W, ATT_KV_W, ATT_KV_W, ML_W, ML_W, ML_W, ML_W, 4 * ML_HEADS, RET_W, RET_W, RET_W, RET_W, N_BRANCH * D_MODEL)
GATE_OFF = ATT_Q_W + 2 * ATT_KV_W + 4 * ML_W
N_IN = ATT_Q_W + 2 * ATT_KV_W + 4 * ML_W + 4 * ML_HEADS + 4 * RET_W + N_BRANCH * D_MODEL
D_FF = 2816
N_EXPERTS = 8
TOP_K = 2
D_FF_EXPERT = 1408
N_DENSE = (DEPTH + 1) // 2
N_MOE = DEPTH // 2
Q_BLOCK = 128
CHUNK = 128
ROPE_BASE = 10000.0
EPS = 1e-6

kernel_name = 'hybrid_flow_prefix_trunk'

F32 = jnp.float32


def rms_norm(x, g=None):
    xf = x.astype(F32)
    y = xf * lax.rsqrt(jnp.mean(xf * xf, axis=-1, keepdims=True) + EPS)
    if g is not None:
        y = y * g.astype(F32)
    return y.astype(x.dtype)


def split_in(z):
    idx = [int(i) for i in np.cumsum(IN_SIZES)[:-1]]
    return jnp.split(z, idx, axis=-1)


def flip(x):
    return jnp.flip(x, axis=1)


def axial_rope(n_tokens, dim):
    rows = n_tokens // GRID_W
    row = jnp.repeat(jnp.arange(rows), GRID_W).astype(F32)
    col = jnp.tile(jnp.arange(GRID_W), rows).astype(F32)
    quarter = dim // 4
    inv = ROPE_BASE ** (-jnp.arange(quarter, dtype=F32) / quarter)
    ang = jnp.stack([row[:, None] * inv, col[:, None] * inv], axis=1)
    return jnp.cos(ang), jnp.sin(ang)


def apply_rope(x, cos, sin):
    B, S, H, Dh = x.shape
    q4 = Dh // 4
    xf = x.astype(F32).reshape(B, S, H, 2, 2, q4)
    x1, x2 = xf[..., 0, :], xf[..., 1, :]
    c = cos[None, :, None]
    s = sin[None, :, None]
    out = jnp.stack([x1 * c - x2 * s, x2 * c + x1 * s], axis=-2)
    return out.reshape(B, S, H, Dh).astype(x.dtype)


def block_attention(q, k, v):
    B, Sq = q.shape[0], q.shape[1]
    G = N_HEADS // N_KV_HEADS
    nb = Sq // Q_BLOCK
    qb = q.reshape(B, nb, Q_BLOCK, N_KV_HEADS, G, HEAD_DIM).transpose(1, 0, 2, 3, 4, 5)
    scale = HEAD_DIM ** -0.5

    def one_block(qblk):
        s = jnp.einsum('bqhgd,bkhd->bhgqk', qblk, k, preferred_element_type=F32) * scale
        p = jax.nn.softmax(s, axis=-1).astype(v.dtype)
        return jnp.einsum('bhgqk,bkhd->bqhgd', p, v)

    o = lax.map(one_block, qb)
    return o.transpose(1, 0, 2, 3, 4, 5).reshape(B, Sq, N_HEADS * HEAD_DIM)


def to_chunks(x):
    B, S, H = x.shape[:3]
    rest = x.shape[3:]
    x = x.reshape((B, S // CHUNK, CHUNK, H) + rest)
    return x.transpose((1, 0, 3, 2) + tuple(range(4, x.ndim)))


def from_chunks(y):
    nc, B, H, L, d = y.shape
    return y.transpose(1, 0, 3, 2, 4).reshape(B, nc * L, H, d)


def mlstm_scan(q, k, v, i_pre, f_pre, C0, n0, m0):
    q = q.astype(F32)
    k = k.astype(F32) * (ML_DIM ** -0.5)
    v = v.astype(F32)
    xs = tuple(to_chunks(t) for t in (q, k, v, i_pre.astype(F32), f_pre.astype(F32)))
    causal = jnp.tril(jnp.ones((CHUNK, CHUNK), dtype=bool))

    def step(carry, xc):
        C, n, m = carry
        qc, kc, vc, ic, fc = xc
        b = jnp.cumsum(jax.nn.log_sigmoid(fc), axis=-1)
        a = b + m[..., None]
        dmat = jnp.where(causal, b[..., :, None] - b[..., None, :] + ic[..., None, :], -jnp.inf)
        mt = jnp.maximum(a, jnp.max(dmat, axis=-1))
        wd = jnp.exp(dmat - mt[..., None])
        wa = jnp.exp(a - mt)
        s = jnp.einsum('bhtd,bhsd->bhts', qc, kc) * wd
        num = jnp.einsum('bhts,bhsv->bhtv', s, vc) + wa[..., None] * jnp.einsum('bhtd,bhdv->bhtv', qc, C)
        den = jnp.sum(s, axis=-1) + wa * jnp.einsum('bhtd,bhd->bht', qc, n)
        h = num / jnp.maximum(jnp.abs(den), jnp.exp(-mt))[..., None]
        b_last = b[..., -1]
        g = b_last[..., None] - b + ic
        m_new = jnp.maximum(b_last + m, jnp.max(g, axis=-1))
        w_c = jnp.exp(b_last + m - m_new)
        w_s = jnp.exp(g - m_new[..., None])
        C_new = w_c[..., None, None] * C + jnp.einsum('bhs,bhsd,bhsv->bhdv', w_s, kc, vc)
        n_new = w_c[..., None] * n + jnp.einsum('bhs,bhsd->bhd', w_s, kc)
        return (C_new, n_new, m_new), h

    (C, n, m), h = lax.scan(step, (C0.astype(F32), n0.astype(F32), m0.astype(F32)), xs)
    return from_chunks(h), (C, n, m)


def retention_scan(q, k, v, log_gamma, S0):
    q = q.astype(F32)
    k = k.astype(F32) * (RET_DIM ** -0.5)
    v = v.astype(F32)
    xs = tuple(to_chunks(t) for t in (q, k, v))
    pos = jnp.arange(CHUNK, dtype=F32)
    lg = log_gamma.astype(F32)[:, None]
    diff = pos[:, None] - pos[None, :]
    decay = jnp.where(diff >= 0, jnp.exp(lg[..., None] * jnp.maximum(diff, 0.0)), 0.0)
    q_dec = jnp.exp(lg * (pos + 1.0))
    k_dec = jnp.exp(lg * (CHUNK - 1.0 - pos))
    c_dec = jnp.exp(lg[:, 0] * CHUNK)

    def step(S, xc):
        qc, kc, vc = xc
        s = jnp.einsum('bhtd,bhsd->bhts', qc, kc) * decay
        o = jnp.einsum('bhts,bhsv->bhtv', s, vc) + q_dec[..., None] * jnp.einsum('bhtd,bhdv->bhtv', qc, S)
        S = c_dec[:, None, None] * S + jnp.einsum('hs,bhsd,bhsv->bhdv', k_dec, kc, vc)
        return S, o

    S, o = lax.scan(step, S0.astype(F32), xs)
    return from_chunks(o), S


def swiglu(x, wg, wu, wd):
    return (jax.nn.silu(x @ wg) * (x @ wu)) @ wd


def moe_swiglu(h, router, wg, wu, wd):
    B, S, D = h.shape
    t = h.reshape(B * S, D)
    logits = jnp.einsum('nd,de->ne', t, router, preferred_element_type=F32)
    top_v, top_i = lax.top_k(logits, TOP_K)
    w = jax.nn.softmax(top_v, axis=-1)
    gates = jnp.sum(jax.nn.one_hot(top_i, N_EXPERTS, dtype=F32) * w[..., None], axis=1)
    out = jnp.zeros(t.shape, F32)
    for e in range(N_EXPERTS):
        out = out + gates[:, e:e + 1] * swiglu(t, wg[e], wu[e], wd[e])
    return out.reshape(B, S, D).astype(h.dtype)


def modulation(cvec, w, b):
    c2 = cvec.reshape(-1, D_MODEL)
    mod = jax.nn.silu(c2) @ w + b
    return mod[:, None, :]


def mixer(h, l, P, ctx):
    B, S, _ = h.shape
    z = jnp.einsum('bsd,dn->bsn', h, P['w_in'][l]) + P['b_in'][l]
    aq, ak, av, mq, mk, mv, mo, mg, rq, rk, rv, rg, bg = split_in(z)
    aq = rms_norm(aq.reshape(B, S, N_HEADS, HEAD_DIM), P['q_norm'][l])
    ak = rms_norm(ak.reshape(B, S, N_KV_HEADS, HEAD_DIM), P['k_norm'][l])
    av = av.reshape(B, S, N_KV_HEADS, HEAD_DIM)
    mq, mk, mv = (t.reshape(B, S, ML_HEADS, ML_DIM) for t in (mq, mk, mv))
    i_f, f_f, i_b, f_b = jnp.split(mg, 4, axis=-1)
    rq, rk, rv = (t.reshape(B, S, RET_HEADS, RET_DIM) for t in (rq, rk, rv))
    log_g = jax.nn.log_sigmoid(P['ret_decay'][l].astype(F32))
    if ctx is None:
        keys, vals = ak, av
        ml0 = [(jnp.zeros((B, ML_HEADS, ML_DIM, ML_DIM), F32), jnp.zeros((B, ML_HEADS, ML_DIM), F32), jnp.zeros((B, ML_HEADS), F32))] * 2
        rt0 = [jnp.zeros((B, RET_HEADS, RET_DIM, RET_DIM), F32)] * 2
    else:
        cos_a, sin_a = axial_rope(S, HEAD_DIM)
        aq = apply_rope(aq, cos_a, sin_a)
        ak = apply_rope(ak, cos_a, sin_a)
        cos_r, sin_r = axial_rope(S, RET_DIM)
        rq = apply_rope(rq, cos_r, sin_r)
        rk = apply_rope(rk, cos_r, sin_r)
        keys = jnp.concatenate([ctx['k'].astype(ak.dtype), ak], axis=1)
        vals = jnp.concatenate([ctx['v'].astype(av.dtype), av], axis=1)
        ml0 = [(ctx['mc'][:, d], ctx['mn'][:, d], ctx['mm'][:, d]) for d in range(2)]
        rt0 = [ctx['rs'][:, d] for d in range(2)]
    att = block_attention(aq, keys, vals)
    hf, stf = mlstm_scan(mq, mk, mv, i_f, f_f, *ml0[0])
    hb, stb = mlstm_scan(flip(mq), flip(mk), flip(mv), flip(i_b), flip(f_b), *ml0[1])
    og = jax.nn.sigmoid(mo.astype(F32)).reshape(B, S, ML_HEADS, ML_DIM)
    ml = (rms_norm(hf + flip(hb)) * og).reshape(B, S, ML_W).astype(h.dtype)
    rf, sf = retention_scan(rq, rk, rv, log_g[0], rt0[0])
    rb, sb = retention_scan(flip(rq), flip(rk), flip(rv), log_g[1], rt0[1])
    ret = (rms_norm(rf + flip(rb)).reshape(B, S, RET_W) * jax.nn.silu(rg.astype(F32))).astype(h.dtype)
    wb = P['w_branch'][l]
    ga, gm, gr = jnp.split(jax.nn.sigmoid(bg), N_BRANCH, axis=-1)
    merged = ga * (att @ wb[:ATT_Q_W]) + gm * (ml @ wb[ATT_Q_W:ATT_Q_W + ML_W]) + gr * (ret @ wb[ATT_Q_W + ML_W:])
    y = merged @ P['w_out'][l]
    if ctx is None:
        state = (ak, av, jnp.stack([stf[0], stb[0]], axis=1), jnp.stack([stf[1], stb[1]], axis=1), jnp.stack([stf[2], stb[2]], axis=1), jnp.stack([sf, sb], axis=1))
    else:
        state = None
    return y, state


def channel_mixer(h, l, P):
    j = l // 2
    if l % 2 == 0:
        return swiglu(h, P['ffn_w_gate'][j], P['ffn_w_up'][j], P['ffn_w_down'][j])
    return moe_swiglu(h, P['moe_router'][j], P['moe_w_gate'][j], P['moe_w_up'][j], P['moe_w_down'][j])


def trunk_layer(x, cvec, l, P, ctx):
    mod = modulation(cvec, P['w_ada'][l], P['b_ada'][l])
    sh1, sc1, g1, sh2, sc2, g2 = jnp.split(mod, 6, axis=-1)
    h = rms_norm(x, P['norm_mix'][l]) * (1 + sc1) + sh1
    y, state = mixer(h, l, P, ctx)
    x = x + g1 * y
    h = rms_norm(x, P['norm_ffn'][l]) * (1 + sc2) + sh2
    x = x + g2 * channel_mixer(h, l, P)
    return x, state


def setup_inputs(seed: int = 0) -> dict:
    key = jax.random.key(seed)
    ks = iter(jax.random.split(key, 40))
    nrm = lambda shape: jax.random.normal(next(ks), shape, F32)
    D = D_MODEL
    b_in = 0.01 * nrm((DEPTH, N_IN))
    f_bias = jnp.linspace(3.0, 6.0, ML_HEADS)
    b_in = b_in.at[:, GATE_OFF + ML_HEADS:GATE_OFF + 2 * ML_HEADS].add(f_bias)
    b_in = b_in.at[:, GATE_OFF + 3 * ML_HEADS:GATE_OFF + 4 * ML_HEADS].add(f_bias)
    ret_base = jnp.log(2.0 ** (5.0 + jnp.arange(RET_HEADS, dtype=F32)) - 1.0)
    w_branch = jnp.concatenate([nrm((DEPTH, ATT_Q_W, D)) * ATT_Q_W ** -0.5, nrm((DEPTH, ML_W, D)) * ML_W ** -0.5, nrm((DEPTH, RET_W, D)) * RET_W ** -0.5], axis=1)
    return {
        'x_prompt': nrm((BATCH, SEQ, D)),
        'x_sample': nrm((DEC_BATCH, DEC_SEQ, D)),
        'c': nrm((DEC_BATCH, D)),
        'cache_attn_k': nrm((DEC_BATCH, DEPTH, PAST_LEN, N_KV_HEADS, HEAD_DIM)),
        'cache_attn_v': nrm((DEC_BATCH, DEPTH, PAST_LEN, N_KV_HEADS, HEAD_DIM)),
        'state_mlstm_c': 0.5 * nrm((DEC_BATCH, DEPTH, 2, ML_HEADS, ML_DIM, ML_DIM)),
        'state_mlstm_n': 0.5 * nrm((DEC_BATCH, DEPTH, 2, ML_HEADS, ML_DIM)),
        'state_mlstm_m': 1.0 + 0.5 * nrm((DEC_BATCH, DEPTH, 2, ML_HEADS)),
        'state_ret_s': nrm((DEC_BATCH, DEPTH, 2, RET_HEADS, RET_DIM, RET_DIM)),
        'c_ctx': nrm((D,)),
        'w_ada': nrm((DEPTH, D, 6 * D)) * (0.5 * D ** -0.5),
        'b_ada': 0.01 * nrm((DEPTH, 6 * D)),
        'norm_mix': 1.0 + 0.02 * nrm((DEPTH, D)),
        'norm_ffn': 1.0 + 0.02 * nrm((DEPTH, D)),
        'w_in': nrm((DEPTH, D, N_IN)) * D ** -0.5,
        'b_in': b_in,
        'q_norm': 1.0 + 0.02 * nrm((DEPTH, HEAD_DIM)),
        'k_norm': 1.0 + 0.02 * nrm((DEPTH, HEAD_DIM)),
        'ret_decay': ret_base + 0.01 * nrm((DEPTH, 2, RET_HEADS)),
        'w_branch': w_branch,
        'w_out': nrm((DEPTH, D, D)) * D ** -0.5,
        'ffn_w_gate': nrm((N_DENSE, D, D_FF)) * D ** -0.5,
        'ffn_w_up': nrm((N_DENSE, D, D_FF)) * D ** -0.5,
        'ffn_w_down': nrm((N_DENSE, D_FF, D)) * D_FF ** -0.5,
        'moe_router': nrm((N_MOE, D, N_EXPERTS)) * D ** -0.5,
        'moe_w_gate': nrm((N_MOE, N_EXPERTS, D, D_FF_EXPERT)) * D ** -0.5,
        'moe_w_up': nrm((N_MOE, N_EXPERTS, D, D_FF_EXPERT)) * D ** -0.5,
        'moe_w_down': nrm((N_MOE, N_EXPERTS, D_FF_EXPERT, D)) * D_FF_EXPERT ** -0.5,
        'final_norm': 1.0 + 0.02 * nrm((D,)),
    }


def reference(x_prompt, x_sample, c, cache_attn_k, cache_attn_v, state_mlstm_c, state_mlstm_n, state_mlstm_m, state_ret_s, c_ctx, w_ada, b_ada, norm_mix, norm_ffn, w_in, b_in, q_norm, k_norm, ret_decay, w_branch, w_out, ffn_w_gate, ffn_w_up, ffn_w_down, moe_router, moe_w_gate, moe_w_up, moe_w_down, final_norm):
    P = dict(w_ada=w_ada, b_ada=b_ada, norm_mix=norm_mix, norm_ffn=norm_ffn, w_in=w_in, b_in=b_in, q_norm=q_norm, k_norm=k_norm, ret_decay=ret_decay, w_branch=w_branch, w_out=w_out, ffn_w_gate=ffn_w_gate, ffn_w_up=ffn_w_up, ffn_w_down=ffn_w_down, moe_router=moe_router, moe_w_gate=moe_w_gate, moe_w_up=moe_w_up, moe_w_down=moe_w_down)
    xp = x_prompt
    states = []
    for l in range(DEPTH):
        xp, st = trunk_layer(xp, c_ctx, l, P, None)
        states.append(st)
    y_prompt = rms_norm(xp, final_norm)
    new_attn_k = jnp.stack([s[0] for s in states], axis=1)
    new_attn_v = jnp.stack([s[1] for s in states], axis=1)
    new_mlstm_c = jnp.stack([s[2] for s in states], axis=1)
    new_mlstm_n = jnp.stack([s[3] for s in states], axis=1)
    new_mlstm_m = jnp.stack([s[4] for s in states], axis=1)
    new_ret_s = jnp.stack([s[5] for s in states], axis=1)
    xs = x_sample
    for l in range(DEPTH):
        ctx = dict(k=cache_attn_k[:, l], v=cache_attn_v[:, l], mc=state_mlstm_c[:, l], mn=state_mlstm_n[:, l], mm=state_mlstm_m[:, l], rs=state_ret_s[:, l])
        xs, _ = trunk_layer(xs, c, l, P, ctx)
    y_sample = rms_norm(xs, final_norm)
    return (y_prompt, y_sample, new_attn_k, new_attn_v, new_mlstm_c, new_mlstm_n, new_mlstm_m, new_ret_s)
```

```python
import functools

import jax
import jax.numpy as jnp
from jax import lax
from jax.experimental import pallas as pl
from jax.experimental.pallas import tpu as pltpu

F32 = jnp.float32
BF16 = jnp.bfloat16

N_HEADS = 8
N_KV_HEADS = 2
HEAD_DIM = 64
ML_HEADS = 4
RET_HEADS = 4
GRID_W = 64
CHUNK = 128
ROPE_BASE = 10000.0
EPS = 1e-6
N_EXPERTS = 8
LANES = 128
HALF = 64

ATT_Q_W = N_HEADS * HEAD_DIM
ATT_KV_W = N_KV_HEADS * HEAD_DIM
ML_W = ML_HEADS * HEAD_DIM
RET_W = RET_HEADS * HEAD_DIM
N_GATES = 4 * ML_HEADS

TOK_TILE = 256
FFN_TILE = 512
ATT_Q_TILE = 128
ATT_K_TILE = 512
VMEM_LIMIT = 56 * 1024 * 1024

O_AQ, O_AK, O_AV = 0, 512, 640
O_MQ, O_MK, O_MV, O_MO = 768, 1024, 1280, 1536
O_RQ, O_RK, O_RV, O_RG = 1792, 2048, 2304, 2560
O_BG = 2816
O_MG = 5888
N_IN_PAD = 6016


def _cparams(sem, vmem=VMEM_LIMIT):
    return pltpu.CompilerParams(dimension_semantics=sem, vmem_limit_bytes=vmem)


def _lane_iota(shape=(1, LANES)):
    return lax.broadcasted_iota(jnp.int32, shape, len(shape) - 1)


def _head_mean_matrix():
    r = lax.broadcasted_iota(jnp.int32, (LANES, LANES), 0) >> 6
    c = lax.broadcasted_iota(jnp.int32, (LANES, LANES), 1) >> 6
    return jnp.where(r == c, 1.0 / HALF, 0.0).astype(BF16)


def _head_rms(x, mean_mat):
    sq = x * x
    hi = sq.astype(BF16)
    lo = (sq - hi.astype(F32)).astype(BF16)
    ms = jnp.dot(hi, mean_mat, preferred_element_type=F32) + jnp.dot(lo, mean_mat, preferred_element_type=F32)
    return x * lax.rsqrt(ms + EPS)


def _rope(y, cos, sin, lane):
    up = pltpu.roll(y, LANES - 16, 1)
    dn = pltpu.roll(y, 16, 1)
    partner = jnp.where((lane & 31) < 16, up, dn)
    return y * cos + partner * sin


def _log_sigmoid(x):
    return jnp.minimum(x, 0.0) - jnp.log1p(jnp.exp(-jnp.abs(x)))


def _sigmoid(x):
    return 1.0 / (1.0 + jnp.exp(-x))


def _split3(x):
    h = x.astype(BF16)
    r = x - h.astype(F32)
    m = r.astype(BF16)
    l = (r - m.astype(F32)).astype(BF16)
    return h, m, l


def _ada_kernel(c_ref, w_ref, b_ref, o_ref):
    c = c_ref[...]
    a = (c * _sigmoid(c)).astype(BF16)
    o_ref[...] = jnp.dot(a, w_ref[...].astype(BF16), preferred_element_type=F32) + b_ref[...]


def _ada_call(c8, w_ada, b_ada):
    depth, d, n6 = w_ada.shape
    tn = 1536
    return pl.pallas_call(
        _ada_kernel,
        out_shape=jax.ShapeDtypeStruct((depth, 8, n6), F32),
        grid=(depth, n6 // tn),
        in_specs=[
            pl.BlockSpec((8, d), lambda l, j: (0, 0)),
            pl.BlockSpec((None, d, tn), lambda l, j: (l, 0, j)),
            pl.BlockSpec((None, 1, tn), lambda l, j: (l, 0, j)),
        ],
        out_specs=pl.BlockSpec((None, 8, tn), lambda l, j: (l, 0, j)),
        compiler_params=_cparams(("arbitrary", "arbitrary")),
        name="ada_mod",
    )(c8, w_ada, b_ada.reshape(depth, 1, n6))


def _inproj_kernel(x_ref, mod_ref, gn_ref, w_ref, b_ref, qg_ref, kg_ref, cos_ref, sin_ref,
                   q_ref, k_ref, v_ref, mqkv_ref, rqkv_ref, mo_ref, rg_ref, bg_ref, gcol_ref, grow_ref):
    x = x_ref[...]
    ms = jnp.mean(x * x, axis=-1, keepdims=True)
    h = x * lax.rsqrt(ms + EPS) * gn_ref[...]
    h = h * (1.0 + mod_ref[1:2, :]) + mod_ref[0:1, :]
    hb = h.astype(BF16)

    def seg(a, b):
        return jnp.dot(hb, w_ref[:, a:b], preferred_element_type=F32) + b_ref[:, a:b]

    lane = _lane_iota()
    mean_mat = _head_mean_matrix()
    cos = cos_ref[...]
    sin = sin_ref[...]
    for s in range(ATT_Q_W // LANES):
        a = O_AQ + s * LANES
        y = _head_rms(seg(a, a + LANES), mean_mat) * qg_ref[...]
        q_ref[:, s * LANES:(s + 1) * LANES] = (_rope(y, cos, sin, lane) * 0.125).astype(BF16)
    y = _head_rms(seg(O_AK, O_AK + LANES), mean_mat) * kg_ref[...]
    k_ref[...] = _rope(y, cos, sin, lane)
    v_ref[...] = seg(O_AV, O_AV + LANES)

    mqkv_ref[:, 0:ML_W] = seg(O_MQ, O_MQ + ML_W).astype(BF16)
    mqkv_ref[:, ML_W:2 * ML_W] = (seg(O_MK, O_MK + ML_W) * 0.125).astype(BF16)
    mqkv_ref[:, 2 * ML_W:3 * ML_W] = seg(O_MV, O_MV + ML_W).astype(BF16)
    mo_ref[...] = seg(O_MO, O_MO + ML_W)

    for s in range(RET_W // LANES):
        a = O_RQ + s * LANES
        rqkv_ref[:, s * LANES:(s + 1) * LANES] = _rope(seg(a, a + LANES), cos, sin, lane).astype(BF16)
        a = O_RK + s * LANES
        rqkv_ref[:, RET_W + s * LANES:RET_W + (s + 1) * LANES] = (
            _rope(seg(a, a + LANES), cos, sin, lane) * 0.125).astype(BF16)
    rqkv_ref[:, 2 * RET_W:3 * RET_W] = seg(O_RV, O_RV + RET_W).astype(BF16)
    rg_ref[...] = seg(O_RG, O_RG + RET_W)
    bg_ref[...] = seg(O_BG, O_MG)

    g = seg(O_MG, O_MG + LANES)
    is_f = ((lane >> 2) & 1) == 1
    g = jnp.where(is_f, _log_sigmoid(g), g)
    gcol_ref[...] = g
    grow_ref[...] = g.T[0:N_GATES, :]


def _inproj_call(x, mod, gn, w, b, qg, kg, cos_t, sin_t, n1, s2):
    n, d = x.shape
    tm = TOK_TILE
    n1t = n1 // tm
    t2 = s2 // tm

    def mod_row(t):
        return jnp.where(t < n1t, 0, 1 + (t - n1t) // t2)

    def tab_row(t):
        return jnp.where(t < n1t, 0, 1 + (t - n1t) % t2)

    row = lambda w_: pl.BlockSpec((tm, w_), lambda t: (t, 0))
    const = lambda shp: pl.BlockSpec(shp, lambda t: (0,) * len(shp))
    out_shape = [
        jax.ShapeDtypeStruct((n, ATT_Q_W), BF16),
        jax.ShapeDtypeStruct((n, ATT_KV_W), F32),
        jax.ShapeDtypeStruct((n, ATT_KV_W), F32),
        jax.ShapeDtypeStruct((n, 3 * ML_W), BF16),
        jax.ShapeDtypeStruct((n, 3 * RET_W), BF16),
        jax.ShapeDtypeStruct((n, ML_W), F32),
        jax.ShapeDtypeStruct((n, RET_W), F32),
        jax.ShapeDtypeStruct((n, 3 * d), F32),
        jax.ShapeDtypeStruct((n, LANES), F32),
        jax.ShapeDtypeStruct((N_GATES, n), F32),
    ]
    out_specs = [row(ATT_Q_W), row(ATT_KV_W), row(ATT_KV_W), row(3 * ML_W), row(3 * RET_W), row(ML_W),
                 row(RET_W), row(3 * d), row(LANES), pl.BlockSpec((N_GATES, tm), lambda t: (0, t))]
    return pl.pallas_call(
        _inproj_kernel,
        out_shape=out_shape,
        grid=(n // tm,),
        in_specs=[
            row(d),
            pl.BlockSpec((None, 6, d), lambda t: (mod_row(t), 0, 0)),
            const((1, d)),
            const((d, N_IN_PAD)),
            const((1, N_IN_PAD)),
            const((1, LANES)),
            const((1, LANES)),
            pl.BlockSpec((tm, LANES), lambda t: (tab_row(t), 0)),
            pl.BlockSpec((tm, LANES), lambda t: (tab_row(t), 0)),
        ],
        out_specs=out_specs,
        compiler_params=_cparams(("arbitrary",)),
        name="in_proj",
    )(x, mod, gn, w, b, qg, kg, cos_t, sin_t)


def _attn_kernel(q_ref, k_ref, v_ref, o_ref, *, tk):
    tq = q_ref.shape[0]
    nk = k_ref.shape[0] // tk
    lo = _lane_iota() < HALF
    for j in range(ATT_Q_W // LANES):
        qs = q_ref[:, j * LANES:(j + 1) * LANES]
        outs = []
        for half in range(2):
            qm = jnp.where(lo if half == 0 else jnp.logical_not(lo), qs, jnp.zeros_like(qs))

            def body(c, carry, qm=qm):
                m, l, acc = carry
                off = pl.multiple_of(c * tk, tk)
                kc = k_ref[pl.ds(off, tk), :]
                vc = v_ref[pl.ds(off, tk), :]
                s = lax.dot_general(qm, kc, (((1,), (1,)), ((), ())), preferred_element_type=F32)
                m_new = jnp.maximum(m, jnp.max(s, axis=-1, keepdims=True))
                alpha = jnp.exp(m - m_new)
                p = jnp.exp(s - m_new)
                l = alpha * l + jnp.sum(p, axis=-1, keepdims=True)
                acc = alpha * acc + jnp.dot(p.astype(BF16), vc, preferred_element_type=F32)
                return m_new, l, acc

            init = (jnp.full((tq, 1), -jnp.inf, F32), jnp.zeros((tq, 1), F32), jnp.zeros((tq, LANES), F32))
            m, l, acc = lax.fori_loop(0, nk, body, init)
            outs.append(acc / l)
        o_ref[:, j * LANES:(j + 1) * LANES] = jnp.where(lo, outs[0], outs[1]).astype(o_ref.dtype)


def _attn_call(q, k, v):
    b, sq, _ = q.shape
    sk = k.shape[1]
    tq = min(ATT_Q_TILE, sq)
    tk = min(ATT_K_TILE, sk)
    return pl.pallas_call(
        functools.partial(_attn_kernel, tk=tk),
        out_shape=jax.ShapeDtypeStruct((b, sq, ATT_Q_W), BF16),
        grid=(b, sq // tq),
        in_specs=[
            pl.BlockSpec((None, tq, ATT_Q_W), lambda i, j: (i, j, 0)),
            pl.BlockSpec((None, sk, ATT_KV_W), lambda i, j: (i, 0, 0)),
            pl.BlockSpec((None, sk, ATT_KV_W), lambda i, j: (i, 0, 0)),
        ],
        out_specs=pl.BlockSpec((None, tq, ATT_Q_W), lambda i, j: (i, j, 0)),
        compiler_params=_cparams(("arbitrary", "arbitrary")),
        name="attention",
    )(q, k, v)


def _scan_kernel(mf_ref, mb_ref, rf_ref, rb_ref, gcf_ref, gcb_ref, grf_ref, grb_ref, rd_ref,
                 c0_ref, n0_ref, m0_ref, s0_ref,
                 hf_ref, hb_ref, of_ref, ob_ref, cout_ref, nout_ref, mout_ref, sout_ref,
                 c_st, n_st, m_st, s_st, dec_st, qdec_st, kdec_st, cdec_st):
    L = CHUNK
    b_idx = pl.program_id(0)
    j = pl.program_id(1)
    nc = pl.num_programs(1)
    lane = _lane_iota()
    lo = lane < HALF
    row_i = lax.broadcasted_iota(jnp.int32, (L, L), 0)
    col_i = lax.broadcasted_iota(jnp.int32, (L, L), 1)
    blockmask = (row_i >> 6) == (col_i >> 6)
    causal = (row_i >= col_i, row_i <= col_i)

    @pl.when(jnp.logical_and(b_idx == 0, j == 0))
    def _init_tables():
        lg = _log_sigmoid(rd_ref[...])
        pos = lax.broadcasted_iota(jnp.int32, (L, 1), 0).astype(F32)
        diff = (row_i - col_i).astype(F32)
        for d in range(2):
            sd = diff if d == 0 else -diff
            for p in range(2):
                qd, kd, cd = [], [], []
                for e in range(2):
                    r = d * RET_HEADS + 2 * p + e
                    g = lg[r:r + 1, 0:1]
                    dec_st[d, 2 * p + e] = jnp.where(sd >= 0, jnp.exp(g * jnp.maximum(sd, 0.0)), 0.0)
                    if d == 0:
                        qd.append(jnp.exp(g * (pos + 1.0)))
                        kd.append(jnp.exp(g * (L - 1.0 - pos)))
                    else:
                        qd.append(jnp.exp(g * (L - pos)))
                        kd.append(jnp.exp(g * pos))
                    cd.append(jnp.exp(g * float(L)))
                qdec_st[d, p] = jnp.where(lo, qd[0], qd[1])
                kdec_st[d, p] = jnp.where(lo, kd[0], kd[1])
                cdec_st[d, p] = jnp.where(lo, cd[0], cd[1])

    @pl.when(j == 0)
    def _load_state():
        c_st[...] = c0_ref[...]
        n_st[...] = n0_ref[...]
        m_st[...] = m0_ref[...]
        s_st[...] = s0_ref[...]

    tri = (row_i >= col_i).astype(BF16)
    triu = (row_i <= col_i).astype(BF16)

    def cumsums(gc_ref, gr_ref, d):
        a_col, a_row = (tri, triu) if d == 0 else (triu, tri)
        col = sum(jnp.dot(a_col, part, preferred_element_type=F32) for part in _split3(gc_ref[...]))
        rowv = sum(jnp.dot(part, a_row, preferred_element_type=F32) for part in _split3(gr_ref[...]))
        return col, rowv

    for d, (m_ref, r_ref, gc_ref, gr_ref, h_out, o_out) in enumerate(
            ((mf_ref, rf_ref, gcf_ref, grf_ref, hf_ref, of_ref),
             (mb_ref, rb_ref, gcb_ref, grb_ref, hb_ref, ob_ref))):
        gcol = gc_ref[...]
        grow = gr_ref[...]
        cum_col, cum_row = cumsums(gc_ref, gr_ref, d)
        gi = 2 * ML_HEADS * d
        gf = gi + ML_HEADS
        last = L - 1 if d == 0 else 0
        for p in range(2):
            sl = slice(p * LANES, (p + 1) * LANES)
            q2 = m_ref[:, p * LANES:(p + 1) * LANES]
            k2 = m_ref[:, ML_W + p * LANES:ML_W + (p + 1) * LANES]
            v2 = m_ref[:, 2 * ML_W + p * LANES:2 * ML_W + (p + 1) * LANES]
            c2 = c_st[d, p]
            n2 = n_st[d, p]
            q_c = jnp.dot(q2, c2.astype(BF16), preferred_element_type=F32)
            q_n = q2.astype(F32) * n2
            h_e, ws_e, wc_e = [], [], []
            for e in range(2):
                hd = 2 * p + e
                hm = lo if e == 0 else jnp.logical_not(lo)
                m_prev = m_st[d, hd][:, 0:1]
                i_col = gcol[:, gi + hd:gi + hd + 1]
                b_col = cum_col[:, gf + hd:gf + hd + 1]
                i_row = grow[gi + hd:gi + hd + 1, :]
                b_row = cum_row[gf + hd:gf + hd + 1, :]
                b_last = b_col[last:last + 1, :]
                a = b_col + m_prev
                dm = jnp.where(causal[d], b_col - b_row + i_row, -jnp.inf)
                mt = jnp.maximum(a, jnp.max(dm, axis=-1, keepdims=True))
                wd = jnp.exp(dm - mt)
                wa = jnp.exp(a - mt)
                qe = jnp.where(hm, q2, jnp.zeros_like(q2))
                s = lax.dot_general(qe, k2, (((1,), (1,)), ((), ())), preferred_element_type=F32) * wd
                sv = jnp.dot(s.astype(BF16), v2, preferred_element_type=F32)
                den = jnp.sum(s, axis=-1, keepdims=True) + wa * jnp.sum(
                    jnp.where(hm, q_n, 0.0), axis=-1, keepdims=True)
                dd = jnp.maximum(jnp.abs(den), jnp.exp(-mt))
                h_e.append((sv + wa * q_c) / dd)
                g_col = b_last - b_col + i_col
                m_new = jnp.maximum(b_last + m_prev, jnp.max(g_col, axis=0, keepdims=True))
                wc_e.append(jnp.exp(b_last + m_prev - m_new))
                ws_e.append(jnp.exp(g_col - m_new))
                m_st[d, hd] = jnp.broadcast_to(m_new, (1, LANES))
            h_out[:, sl] = jnp.where(lo, h_e[0], h_e[1])
            kw = k2.astype(F32) * jnp.where(lo, ws_e[0], ws_e[1])
            wc2 = jnp.where(lo, wc_e[0], wc_e[1])
            upd = jnp.dot(kw.T.astype(BF16), v2, preferred_element_type=F32)
            c_st[d, p] = jnp.where(blockmask, wc2 * c2 + upd, 0.0)
            n_st[d, p] = wc2 * n2 + jnp.sum(kw, axis=0, keepdims=True)
            q2 = r_ref[:, p * LANES:(p + 1) * LANES]
            k2 = r_ref[:, RET_W + p * LANES:RET_W + (p + 1) * LANES]
            v2 = r_ref[:, 2 * RET_W + p * LANES:2 * RET_W + (p + 1) * LANES]
            s2 = s_st[d, p]
            q_s = jnp.dot(q2, s2.astype(BF16), preferred_element_type=F32) * qdec_st[d, p]
            sv_e = []
            for e in range(2):
                hm = lo if e == 0 else jnp.logical_not(lo)
                qe = jnp.where(hm, q2, jnp.zeros_like(q2))
                s = lax.dot_general(qe, k2, (((1,), (1,)), ((), ())), preferred_element_type=F32)
                s = s * dec_st[d, 2 * p + e]
                sv_e.append(jnp.dot(s.astype(BF16), v2, preferred_element_type=F32))
            o_out[:, sl] = jnp.where(lo, sv_e[0], sv_e[1]) + q_s
            kw = k2.astype(F32) * kdec_st[d, p]
            upd = jnp.dot(kw.T.astype(BF16), v2, preferred_element_type=F32)
            s_st[d, p] = jnp.where(blockmask, cdec_st[d, p] * s2 + upd, 0.0)

    @pl.when(j == nc - 1)
    def _store_state():
        cout_ref[...] = c_st[...]
        nout_ref[...] = n_st[...]
        mout_ref[...] = m_st[...]
        sout_ref[...] = s_st[...]


def _scan_call(mqkv, rqkv, gcol, grow, rd, c0, n0, m0, s0):
    b, s, _ = mqkv.shape
    L = CHUNK
    nc = s // L
    fwd = lambda w_: pl.BlockSpec((None, L, w_), lambda i, j: (i, j, 0))
    bwd = lambda w_: pl.BlockSpec((None, L, w_), lambda i, j: (i, nc - 1 - j, 0))
    st = lambda shp: pl.BlockSpec((None,) + shp, lambda i, j: (i,) + (0,) * len(shp))
    pair_mat = (2, 2, LANES, LANES)
    pair_vec = (2, 2, 1, LANES)
    head_vec = (2, ML_HEADS, 1, LANES)
    out_shape = [
        jax.ShapeDtypeStruct((b, s, ML_W), F32), jax.ShapeDtypeStruct((b, s, ML_W), F32),
        jax.ShapeDtypeStruct((b, s, RET_W), F32), jax.ShapeDtypeStruct((b, s, RET_W), F32),
        jax.ShapeDtypeStruct((b,) + pair_mat, F32), jax.ShapeDtypeStruct((b,) + pair_vec, F32),
        jax.ShapeDtypeStruct((b,) + head_vec, F32), jax.ShapeDtypeStruct((b,) + pair_mat, F32),
    ]
    return pl.pallas_call(
        _scan_kernel,
        out_shape=out_shape,
        grid=(b, nc),
        in_specs=[
            fwd(3 * ML_W), bwd(3 * ML_W), fwd(3 * RET_W), bwd(3 * RET_W),
            fwd(LANES), bwd(LANES),
            pl.BlockSpec((N_GATES, L), lambda i, j: (0, i * nc + j)),
            pl.BlockSpec((N_GATES, L), lambda i, j: (0, i * nc + nc - 1 - j)),
            pl.BlockSpec((8, LANES), lambda i, j: (0, 0)),
            st(pair_mat), st(pair_vec), st(head_vec), st(pair_mat),
        ],
        out_specs=[fwd(ML_W), bwd(ML_W), fwd(RET_W), bwd(RET_W),
                   st(pair_mat), st(pair_vec), st(head_vec), st(pair_mat)],
        scratch_shapes=[
            pltpu.VMEM(pair_mat, F32), pltpu.VMEM(pair_vec, F32), pltpu.VMEM(head_vec, F32),
            pltpu.VMEM(pair_mat, F32),
            pltpu.VMEM((2, RET_HEADS, L, L), F32), pltpu.VMEM((2, 2, L, LANES), F32),
            pltpu.VMEM((2, 2, L, LANES), F32), pltpu.VMEM((2, 2, 1, LANES), F32),
        ],
        compiler_params=_cparams(("arbitrary", "arbitrary")),
        name="scan_mixers",
    )(mqkv, mqkv, rqkv, rqkv, gcol, gcol, grow, grow, rd, c0, n0, m0, s0)


def _merge_kernel(x_ref, mod_ref, att_ref, hf_ref, hb_ref, of_ref, ob_ref, mo_ref, rg_ref, bg_ref,
                  wb_ref, wo_ref, gn_ref, *rest, moe):
    if moe:
        router_ref, x1_ref, h2_ref, gates_ref = rest
    else:
        x1_ref, h2_ref = rest
    d = x_ref.shape[1]
    mean_mat = _head_mean_matrix()
    y = jnp.dot(att_ref[...], wb_ref[0:ATT_Q_W, :], preferred_element_type=F32) * _sigmoid(bg_ref[:, 0:d])
    ml, ret = [], []
    for s in range(ML_W // LANES):
        sl = slice(s * LANES, (s + 1) * LANES)
        ml.append((_head_rms(hf_ref[:, sl] + hb_ref[:, sl], mean_mat) * _sigmoid(mo_ref[:, sl])).astype(BF16))
        rg = rg_ref[:, sl]
        ret.append((_head_rms(of_ref[:, sl] + ob_ref[:, sl], mean_mat) * (rg * _sigmoid(rg))).astype(BF16))
    ml = jnp.concatenate(ml, axis=1)
    ret = jnp.concatenate(ret, axis=1)
    y += jnp.dot(ml, wb_ref[ATT_Q_W:ATT_Q_W + ML_W, :], preferred_element_type=F32) * _sigmoid(bg_ref[:, d:2 * d])
    y += jnp.dot(ret, wb_ref[ATT_Q_W + ML_W:, :], preferred_element_type=F32) * _sigmoid(bg_ref[:, 2 * d:3 * d])
    y = jnp.dot(y.astype(BF16), wo_ref[...], preferred_element_type=F32)
    x1 = x_ref[...] + mod_ref[2:3, :] * y
    x1_ref[...] = x1
    ms = jnp.mean(x1 * x1, axis=-1, keepdims=True)
    h2 = x1 * lax.rsqrt(ms + EPS) * gn_ref[...]
    h2 = h2 * (1.0 + mod_ref[4:5, :]) + mod_ref[3:4, :]
    h2_ref[...] = h2.astype(BF16)
    if moe:
        hh = h2.astype(BF16)
        hl = (h2 - hh.astype(F32)).astype(BF16)
        r = router_ref[...]
        rh = r.astype(BF16)
        rl = (r - rh.astype(F32)).astype(BF16)
        logits = (jnp.dot(hh, rh, preferred_element_type=F32) + jnp.dot(hl, rh, preferred_element_type=F32)
                  + jnp.dot(hh, rl, preferred_element_type=F32))
        lane = _lane_iota()
        lg = jnp.where(lane < N_EXPERTS, logits, -jnp.inf)
        m1 = jnp.max(lg, axis=-1, keepdims=True)
        i1 = jnp.min(jnp.where(lg == m1, lane, LANES), axis=-1, keepdims=True)
        sel1 = lane == i1
        lg2 = jnp.where(sel1, -jnp.inf, lg)
        m2 = jnp.max(lg2, axis=-1, keepdims=True)
        i2 = jnp.min(jnp.where(lg2 == m2, lane, LANES), axis=-1, keepdims=True)
        sel2 = lane == i2
        e2 = jnp.exp(m2 - m1)
        den = 1.0 + e2
        gates_ref[...] = jnp.where(sel1, 1.0 / den, 0.0) + jnp.where(sel2, e2 / den, 0.0)


def _merge_call(x, mod, att, hf, hb, of, ob, mo, rg, bg, wb, wo, gn, router, n1, s2):
    n, d = x.shape
    tm = TOK_TILE
    n1t = n1 // tm
    t2 = s2 // tm
    moe = router is not None

    def mod_row(t):
        return jnp.where(t < n1t, 0, 1 + (t - n1t) // t2)

    row = lambda w_: pl.BlockSpec((tm, w_), lambda t: (t, 0))
    const = lambda shp: pl.BlockSpec(shp, lambda t: (0,) * len(shp))
    in_specs = [row(d), pl.BlockSpec((None, 6, d), lambda t: (mod_row(t), 0, 0)), row(ATT_Q_W),
                row(ML_W), row(ML_W), row(RET_W), row(RET_W), row(ML_W), row(RET_W), row(3 * d),
                const((d, d)), const((d, d)), const((1, d))]
    args = [x, mod, att, hf, hb, of, ob, mo, rg, bg, wb, wo, gn]
    out_shape = [jax.ShapeDtypeStruct((n, d), F32), jax.ShapeDtypeStruct((n, d), BF16)]
    out_specs = [row(d), row(d)]
    if moe:
        in_specs.append(const((d, LANES)))
        args.append(router)
        out_shape.append(jax.ShapeDtypeStruct((n, LANES), F32))
        out_specs.append(row(LANES))
    return pl.pallas_call(
        functools.partial(_merge_kernel, moe=moe),
        out_shape=out_shape,
        grid=(n // tm,),
        in_specs=in_specs,
        out_specs=out_specs,
        compiler_params=_cparams(("arbitrary",)),
        name="merge_out",
    )(*args)


def _final_norm(x, g):
    ms = jnp.mean(x * x, axis=-1, keepdims=True)
    return x * lax.rsqrt(ms + EPS) * g


def _ffn_kernel(h_ref, x1_ref, mod_ref, wg_ref, wu_ref, wd_ref, fn_ref, o_ref, *, n_chunks, final):
    h = h_ref[...]
    f = wg_ref.shape[1]
    fc = f // n_chunks
    acc = jnp.zeros(o_ref.shape, F32)
    for c in range(n_chunks):
        sl = slice(c * fc, (c + 1) * fc)
        g = jnp.dot(h, wg_ref[:, sl], preferred_element_type=F32)
        u = jnp.dot(h, wu_ref[:, sl], preferred_element_type=F32)
        a = (g * _sigmoid(g) * u).astype(BF16)
        acc += jnp.dot(a, wd_ref[sl, :], preferred_element_type=F32)
    x2 = x1_ref[...] + mod_ref[5:6, :] * acc
    o_ref[...] = _final_norm(x2, fn_ref[...]) if final else x2


def _ffn_call(h2, x1, mod, wg, wu, wd, fn, n1, s2, final):
    n, d = x1.shape
    f = wg.shape[1]
    tm = FFN_TILE

    def mod_row(t):
        return jnp.where(t * tm < n1, 0, 1 + (t * tm - n1) // s2)

    row = lambda dt: pl.BlockSpec((tm, d), lambda t: (t, 0))
    const = lambda shp: pl.BlockSpec(shp, lambda t: (0,) * len(shp), pipeline_mode=pl.Buffered(1))
    return pl.pallas_call(
        functools.partial(_ffn_kernel, n_chunks=2, final=final),
        out_shape=jax.ShapeDtypeStruct((n, d), F32),
        grid=(n // tm,),
        in_specs=[row(BF16), row(F32), pl.BlockSpec((None, 6, d), lambda t: (mod_row(t), 0, 0)),
                  const((d, f)), const((d, f)), const((f, d)), pl.BlockSpec((1, d), lambda t: (0, 0))],
        out_specs=row(F32),
        compiler_params=_cparams(("arbitrary",)),
        name="ffn_dense",
    )(h2, x1, mod, wg, wu, wd, fn)


def _moe_kernel(h_ref, gates_ref, x1_ref, mod_ref, wg_ref, wu_ref, wd_ref, fn_ref, o_ref, acc_ref, *, final):
    e = pl.program_id(1)

    @pl.when(e == 0)
    def _zero():
        acc_ref[...] = jnp.zeros_like(acc_ref)

    h = h_ref[...]
    g = jnp.dot(h, wg_ref[...], preferred_element_type=F32)
    u = jnp.dot(h, wu_ref[...], preferred_element_type=F32)
    a = (g * _sigmoid(g) * u).astype(BF16)
    y = jnp.dot(a, wd_ref[...], preferred_element_type=F32)
    gate = jnp.sum(jnp.where(_lane_iota() == e, gates_ref[...], 0.0), axis=-1, keepdims=True)
    acc_ref[...] += gate * y

    @pl.when(e == pl.num_programs(1) - 1)
    def _finish():
        x2 = x1_ref[...] + mod_ref[5:6, :] * acc_ref[...]
        o_ref[...] = _final_norm(x2, fn_ref[...]) if final else x2


def _moe_call(h2, gates, x1, mod, wg, wu, wd, fn, n1, s2, final):
    n, d = x1.shape
    ne, _, f = wg.shape
    tm = FFN_TILE

    def mod_row(t):
        return jnp.where(t * tm < n1, 0, 1 + (t * tm - n1) // s2)

    row = lambda w_: pl.BlockSpec((tm, w_), lambda t, e: (t, 0))
    return pl.pallas_call(
        functools.partial(_moe_kernel, final=final),
        out_shape=jax.ShapeDtypeStruct((n, d), F32),
        grid=(n // tm, ne),
        in_specs=[row(d), row(LANES), row(d), pl.BlockSpec((None, 6, d), lambda t, e: (mod_row(t), 0, 0)),
                  pl.BlockSpec((None, d, f), lambda t, e: (e, 0, 0)),
                  pl.BlockSpec((None, d, f), lambda t, e: (e, 0, 0)),
                  pl.BlockSpec((None, f, d), lambda t, e: (e, 0, 0)),
                  pl.BlockSpec((1, d), lambda t, e: (0, 0))],
        out_specs=row(d),
        scratch_shapes=[pltpu.VMEM((tm, d), F32)],
        compiler_params=_cparams(("arbitrary", "arbitrary")),
        name="moe_dense",
    )(h2, gates, x1, mod, wg, wu, wd, fn)


def _q_head_perm():
    order = []
    g = N_HEADS // N_KV_HEADS
    for j in range(g):
        order += [j, g + j]
    cols = []
    for h in order:
        cols += list(range(h * HEAD_DIM, (h + 1) * HEAD_DIM))
    return jnp.asarray(cols, jnp.int32)


def _prep_in_weights(w_in, b_in):
    g0 = ATT_Q_W + 2 * ATT_KV_W + 4 * ML_W
    qperm = _q_head_perm()
    n_in = w_in.shape[-1]
    cols = jnp.concatenate([qperm, jnp.arange(ATT_Q_W, g0), jnp.arange(g0 + N_GATES, n_in),
                            jnp.arange(g0, g0 + N_GATES)])
    pad = N_IN_PAD - n_in
    w = jnp.pad(jnp.take(w_in, cols, axis=-1), ((0, 0), (0, 0), (0, pad))).astype(BF16)
    b = jnp.pad(jnp.take(b_in, cols, axis=-1), ((0, 0), (0, pad)))
    return w, b[:, None, :]


def _rope_tables(s2, tm):
    pos = jnp.arange(s2)
    rowp = (pos // GRID_W).astype(F32)
    colp = (pos % GRID_W).astype(F32)
    quarter = HEAD_DIM // 4
    inv = ROPE_BASE ** (-jnp.arange(quarter, dtype=F32) / quarter)
    ang_r = rowp[:, None] * inv
    ang_c = colp[:, None] * inv
    cos_h = jnp.concatenate([jnp.cos(ang_r), jnp.cos(ang_r), jnp.cos(ang_c), jnp.cos(ang_c)], axis=1)
    sin_h = jnp.concatenate([-jnp.sin(ang_r), jnp.sin(ang_r), -jnp.sin(ang_c), jnp.sin(ang_c)], axis=1)
    cos_t = jnp.concatenate([cos_h, cos_h], axis=1)
    sin_t = jnp.concatenate([sin_h, sin_h], axis=1)
    cos_t = jnp.concatenate([jnp.ones((tm, LANES), F32), cos_t], axis=0)
    sin_t = jnp.concatenate([jnp.zeros((tm, LANES), F32), sin_t], axis=0)
    return cos_t, sin_t


def _pair_blockdiag(m):
    b, nd, h, d, _ = m.shape
    m = m.reshape(b, nd, h // 2, 2, d, d)
    z = jnp.zeros_like(m[:, :, :, 0])
    top = jnp.concatenate([m[:, :, :, 0], z], axis=-1)
    bot = jnp.concatenate([z, m[:, :, :, 1]], axis=-1)
    return jnp.concatenate([top, bot], axis=-2)


def _pair_unblock(m):
    b, nd, p, _, _ = m.shape
    a = m[:, :, :, :HALF, :HALF]
    c = m[:, :, :, HALF:, HALF:]
    return jnp.stack([a, c], axis=3).reshape(b, nd, 2 * p, HALF, HALF)


def kernel(x_prompt, x_sample, c, cache_attn_k, cache_attn_v, state_mlstm_c, state_mlstm_n, state_mlstm_m,
           state_ret_s, c_ctx, w_ada, b_ada, norm_mix, norm_ffn, w_in, b_in, q_norm, k_norm, ret_decay,
           w_branch, w_out, ffn_w_gate, ffn_w_up, ffn_w_down, moe_router, moe_w_gate, moe_w_up, moe_w_down,
           final_norm):
    b1, s1, d = x_prompt.shape
    b2, s2, _ = x_sample.shape
    depth = w_in.shape[0]
    n1, n2 = b1 * s1, b2 * s2
    assert s1 % TOK_TILE == 0 and s2 % FFN_TILE == 0 and n1 % FFN_TILE == 0 and b2 + 1 <= 8

    x = jnp.concatenate([x_prompt.reshape(n1, d), x_sample.reshape(n2, d)], axis=0)
    c8 = jnp.concatenate([c_ctx[None, :], c, jnp.zeros((8 - 1 - b2, d), F32)], axis=0)
    mod_all = _ada_call(c8, w_ada, b_ada).reshape(depth, 8, 6, d)

    w_in_p, b_in_p = _prep_in_weights(w_in, b_in)
    cos_t, sin_t = _rope_tables(s2, TOK_TILE)
    qperm = _q_head_perm()
    wb = jnp.concatenate([jnp.take(w_branch[:, :ATT_Q_W], qperm, axis=1), w_branch[:, ATT_Q_W:]], axis=1).astype(BF16)
    wo = w_out.astype(BF16)
    fn = final_norm[None, :]

    zeros_like_state = lambda shp: jnp.zeros((b1,) + shp, F32)
    states = []
    for l in range(depth):
        mod = mod_all[l]
        qg = jnp.tile(q_norm[l], 2)[None, :]
        kg = jnp.tile(k_norm[l], 2)[None, :]
        q, k, v, mqkv, rqkv, mo, rg, bg, gcol, grow = _inproj_call(
            x, mod, norm_mix[l][None, :], w_in_p[l], b_in_p[l], qg, kg, cos_t, sin_t, n1, s2)

        k1 = k[:n1].reshape(b1, s1, ATT_KV_W)
        v1 = v[:n1].reshape(b1, s1, ATT_KV_W)
        att1 = _attn_call(q[:n1].reshape(b1, s1, ATT_Q_W), k1.astype(BF16), v1.astype(BF16))
        k2 = jnp.concatenate([cache_attn_k[:, l].reshape(b2, -1, ATT_KV_W), k[n1:].reshape(b2, s2, ATT_KV_W)], axis=1)
        v2 = jnp.concatenate([cache_attn_v[:, l].reshape(b2, -1, ATT_KV_W), v[n1:].reshape(b2, s2, ATT_KV_W)], axis=1)
        att2 = _attn_call(q[n1:].reshape(b2, s2, ATT_Q_W), k2.astype(BF16), v2.astype(BF16))
        att = jnp.concatenate([att1.reshape(n1, ATT_Q_W), att2.reshape(n2, ATT_Q_W)], axis=0)

        rd = jnp.broadcast_to(ret_decay[l].reshape(2 * RET_HEADS, 1), (2 * RET_HEADS, LANES))
        r1 = _scan_call(mqkv[:n1].reshape(b1, s1, -1), rqkv[:n1].reshape(b1, s1, -1),
                        gcol[:n1].reshape(b1, s1, LANES), grow[:, :n1], rd,
                        zeros_like_state((2, 2, LANES, LANES)), zeros_like_state((2, 2, 1, LANES)),
                        zeros_like_state((2, ML_HEADS, 1, LANES)), zeros_like_state((2, 2, LANES, LANES)))
        c0 = _pair_blockdiag(state_mlstm_c[:, l])
        n0 = state_mlstm_n[:, l].reshape(b2, 2, 2, 1, LANES)
        m0 = jnp.broadcast_to(state_mlstm_m[:, l][..., None, None], (b2, 2, ML_HEADS, 1, LANES))
        s0 = _pair_blockdiag(state_ret_s[:, l])
        r2 = _scan_call(mqkv[n1:].reshape(b2, s2, -1), rqkv[n1:].reshape(b2, s2, -1),
                        gcol[n1:].reshape(b2, s2, LANES), grow[:, n1:], rd, c0, n0, m0, s0)
        hf, hb, of, ob = (jnp.concatenate([a.reshape(n1, -1), bb.reshape(n2, -1)], axis=0)
                          for a, bb in zip(r1[:4], r2[:4]))
        states.append((k1.reshape(b1, s1, N_KV_HEADS, HEAD_DIM), v1.reshape(b1, s1, N_KV_HEADS, HEAD_DIM),
                       _pair_unblock(r1[4]), r1[5].reshape(b1, 2, ML_HEADS, HEAD_DIM), r1[6][..., 0, 0],
                       _pair_unblock(r1[7])))

        moe = l % 2 == 1
        jj = l // 2
        router = jnp.pad(moe_router[jj], ((0, 0), (0, LANES - N_EXPERTS))) if moe else None
        outs = _merge_call(x, mod, att, hf, hb, of, ob, mo, rg, bg, wb[l], wo[l], norm_ffn[l][None, :], router, n1, s2)
        final = l == depth - 1
        if moe:
            x1, h2, gates = outs
            x = _moe_call(h2, gates, x1, mod, moe_w_gate[jj].astype(BF16), moe_w_up[jj].astype(BF16),
                          moe_w_down[jj].astype(BF16), fn, n1, s2, final)
        else:
            x1, h2 = outs
            x = _ffn_call(h2, x1, mod, ffn_w_gate[jj].astype(BF16), ffn_w_up[jj].astype(BF16),
                          ffn_w_down[jj].astype(BF16), fn, n1, s2, final)

    y_prompt = x[:n1].reshape(b1, s1, d)
    y_sample = x[n1:].reshape(b2, s2, d)
    stack = lambda i: jnp.stack([s[i] for s in states], axis=1)
    return (y_prompt, y_sample, stack(0), stack(1), stack(2), stack(3), stack(4), stack(5))
```

```python
import functools

import jax
import jax.numpy as jnp
from jax import lax
from jax.experimental import pallas as pl
from jax.experimental.pallas import tpu as pltpu

F32 = jnp.float32
BF16 = jnp.bfloat16

N_HEADS = 8
N_KV_HEADS = 2
HEAD_DIM = 64
ML_HEADS = 4
RET_HEADS = 4
GRID_W = 64
CHUNK = 128
ROPE_BASE = 10000.0
EPS = 1e-6
N_EXPERTS = 8
LANES = 128
HALF = 64

ATT_Q_W = N_HEADS * HEAD_DIM
ATT_KV_W = N_KV_HEADS * HEAD_DIM
ML_W = ML_HEADS * HEAD_DIM
RET_W = RET_HEADS * HEAD_DIM
N_GATES = 4 * ML_HEADS

TOK_TILE = 256
FFN_TILE = 512
ATT_Q_TILE = 256
ATT_K_TILE = 512
LOG2E = 1.4426950408889634
Q_SCALE = 0.125 * LOG2E
VMEM_LIMIT = 56 * 1024 * 1024

O_AQ, O_AK, O_AV = 0, 512, 640
O_MQ, O_MK, O_MV, O_MO = 768, 1024, 1280, 1536
O_RQ, O_RK, O_RV, O_RG = 1792, 2048, 2304, 2560
O_BG = 2816
O_MG = 5888
N_IN_PAD = 6016


def _cparams(sem, vmem=VMEM_LIMIT):
    return pltpu.CompilerParams(dimension_semantics=sem, vmem_limit_bytes=vmem)


def _lane_iota(shape=(1, LANES)):
    return lax.broadcasted_iota(jnp.int32, shape, len(shape) - 1)


def _head_mean_matrix():
    r = lax.broadcasted_iota(jnp.int32, (LANES, LANES), 0) >> 6
    c = lax.broadcasted_iota(jnp.int32, (LANES, LANES), 1) >> 6
    return jnp.where(r == c, 1.0 / HALF, 0.0).astype(BF16)


def _head_rms(x, mean_mat):
    sq = x * x
    hi = sq.astype(BF16)
    lo = (sq - hi.astype(F32)).astype(BF16)
    ms = jnp.dot(hi, mean_mat, preferred_element_type=F32) + jnp.dot(lo, mean_mat, preferred_element_type=F32)
    return x * lax.rsqrt(ms + EPS)


def _rope(y, cos, sin, lane):
    up = pltpu.roll(y, LANES - 16, 1)
    dn = pltpu.roll(y, 16, 1)
    partner = jnp.where((lane & 31) < 16, up, dn)
    return y * cos + partner * sin


def _log_sigmoid(x):
    return jnp.minimum(x, 0.0) - jnp.log1p(jnp.exp(-jnp.abs(x)))


def _sigmoid(x):
    return 1.0 / (1.0 + jnp.exp(-x))


def _split3(x):
    h = x.astype(BF16)
    r = x - h.astype(F32)
    m = r.astype(BF16)
    l = (r - m.astype(F32)).astype(BF16)
    return h, m, l


def _ada_kernel(c_ref, w_ref, b_ref, o_ref):
    c = c_ref[...]
    a = (c * _sigmoid(c)).astype(BF16)
    o_ref[...] = jnp.dot(a, w_ref[...].astype(BF16), preferred_element_type=F32) + b_ref[...]


def _ada_call(c8, w_ada, b_ada):
    depth, d, n6 = w_ada.shape
    tn = 1536
    return pl.pallas_call(
        _ada_kernel,
        out_shape=jax.ShapeDtypeStruct((depth, 8, n6), F32),
        grid=(depth, n6 // tn),
        in_specs=[
            pl.BlockSpec((8, d), lambda l, j: (0, 0)),
            pl.BlockSpec((None, d, tn), lambda l, j: (l, 0, j)),
            pl.BlockSpec((None, 1, tn), lambda l, j: (l, 0, j)),
        ],
        out_specs=pl.BlockSpec((None, 8, tn), lambda l, j: (l, 0, j)),
        compiler_params=_cparams(("arbitrary", "arbitrary")),
        name="ada_mod",
    )(c8, w_ada, b_ada.reshape(depth, 1, n6))


def _inproj_kernel(x_ref, mod_ref, gn_ref, w_ref, b_ref, qg_ref, kg_ref, cos_ref, sin_ref,
                   q_ref, k_ref, v_ref, kb_ref, vb_ref, mqkv_ref, rqkv_ref, mo_ref, rg_ref, bg_ref, gcol_ref,
                   grow_ref):
    x = x_ref[...]
    ms = jnp.mean(x * x, axis=-1, keepdims=True)
    h = x * lax.rsqrt(ms + EPS) * gn_ref[...]
    h = h * (1.0 + mod_ref[1:2, :]) + mod_ref[0:1, :]
    hb = h.astype(BF16)

    def seg(a, b):
        return jnp.dot(hb, w_ref[:, a:b], preferred_element_type=F32) + b_ref[:, a:b]

    lane = _lane_iota()
    mean_mat = _head_mean_matrix()
    cos = cos_ref[...]
    sin = sin_ref[...]
    for s in range(ATT_Q_W // LANES):
        a = O_AQ + s * LANES
        y = _head_rms(seg(a, a + LANES), mean_mat) * qg_ref[...]
        q_ref[s] = (_rope(y, cos, sin, lane) * Q_SCALE).astype(BF16)
    y = _head_rms(seg(O_AK, O_AK + LANES), mean_mat) * kg_ref[...]
    y = _rope(y, cos, sin, lane)
    k_ref[...] = y
    kb_ref[...] = y.astype(BF16)
    y = seg(O_AV, O_AV + LANES)
    v_ref[...] = y
    vb_ref[...] = y.astype(BF16)

    mqkv_ref[:, 0:ML_W] = seg(O_MQ, O_MQ + ML_W).astype(BF16)
    mqkv_ref[:, ML_W:2 * ML_W] = (seg(O_MK, O_MK + ML_W) * 0.125).astype(BF16)
    mqkv_ref[:, 2 * ML_W:3 * ML_W] = seg(O_MV, O_MV + ML_W).astype(BF16)
    mo_ref[...] = seg(O_MO, O_MO + ML_W)

    for s in range(RET_W // LANES):
        a = O_RQ + s * LANES
        rqkv_ref[:, s * LANES:(s + 1) * LANES] = _rope(seg(a, a + LANES), cos, sin, lane).astype(BF16)
        a = O_RK + s * LANES
        rqkv_ref[:, RET_W + s * LANES:RET_W + (s + 1) * LANES] = (
            _rope(seg(a, a + LANES), cos, sin, lane) * 0.125).astype(BF16)
    rqkv_ref[:, 2 * RET_W:3 * RET_W] = seg(O_RV, O_RV + RET_W).astype(BF16)
    rg_ref[...] = seg(O_RG, O_RG + RET_W)
    bg_ref[...] = seg(O_BG, O_MG)

    g = seg(O_MG, O_MG + LANES)
    is_f = ((lane >> 2) & 1) == 1
    g = jnp.where(is_f, _log_sigmoid(g), g)
    gcol_ref[...] = g
    grow_ref[...] = g.T[0:N_GATES, :]


def _inproj_call(x, mod, gn, w, b, qg, kg, cos_t, sin_t, n1, s2):
    n, d = x.shape
    tm = TOK_TILE
    n1t = n1 // tm
    t2 = s2 // tm

    def mod_row(t):
        return jnp.where(t < n1t, 0, 1 + (t - n1t) // t2)

    def tab_row(t):
        return jnp.where(t < n1t, 0, 1 + (t - n1t) % t2)

    row = lambda w_: pl.BlockSpec((tm, w_), lambda t: (t, 0))
    const = lambda shp: pl.BlockSpec(shp, lambda t: (0,) * len(shp))
    n_slabs = ATT_Q_W // LANES
    out_shape = [
        jax.ShapeDtypeStruct((n_slabs, n, LANES), BF16),
        jax.ShapeDtypeStruct((n, ATT_KV_W), F32),
        jax.ShapeDtypeStruct((n, ATT_KV_W), F32),
        jax.ShapeDtypeStruct((n, ATT_KV_W), BF16),
        jax.ShapeDtypeStruct((n, ATT_KV_W), BF16),
        jax.ShapeDtypeStruct((n, 3 * ML_W), BF16),
        jax.ShapeDtypeStruct((n, 3 * RET_W), BF16),
        jax.ShapeDtypeStruct((n, ML_W), F32),
        jax.ShapeDtypeStruct((n, RET_W), F32),
        jax.ShapeDtypeStruct((n, 3 * d), F32),
        jax.ShapeDtypeStruct((n, LANES), F32),
        jax.ShapeDtypeStruct((N_GATES, n), F32),
    ]
    out_specs = [pl.BlockSpec((n_slabs, tm, LANES), lambda t: (0, t, 0)), row(ATT_KV_W), row(ATT_KV_W),
                 row(ATT_KV_W), row(ATT_KV_W), row(3 * ML_W), row(3 * RET_W), row(ML_W),
                 row(RET_W), row(3 * d), row(LANES), pl.BlockSpec((N_GATES, tm), lambda t: (0, t))]
    return pl.pallas_call(
        _inproj_kernel,
        out_shape=out_shape,
        grid=(n // tm,),
        in_specs=[
            row(d),
            pl.BlockSpec((None, 6, d), lambda t: (mod_row(t), 0, 0)),
            const((1, d)),
            const((d, N_IN_PAD)),
            const((1, N_IN_PAD)),
            const((1, LANES)),
            const((1, LANES)),
            pl.BlockSpec((tm, LANES), lambda t: (tab_row(t), 0)),
            pl.BlockSpec((tm, LANES), lambda t: (tab_row(t), 0)),
        ],
        out_specs=out_specs,
        compiler_params=_cparams(("arbitrary",)),
        name="in_proj",
    )(x, mod, gn, w, b, qg, kg, cos_t, sin_t)


def _attn_kernel(q_ref, kt_ref, v_ref, o_ref, s_sc, *, tk):
    tq = q_ref.shape[1]
    nk = kt_ref.shape[1] // tk
    lo = _lane_iota() < HALF

    def slab(j, carry):
        qs = q_ref[j]
        outs = []
        for half in range(2):
            qm = jnp.where(lo if half == 0 else jnp.logical_not(lo), qs, jnp.zeros_like(qs))
            mx = None
            for c in range(nk):
                s = jnp.dot(qm, kt_ref[:, c * tk:(c + 1) * tk], preferred_element_type=F32)
                s_sc[half, :, c * tk:(c + 1) * tk] = s
                cm = jnp.max(s, axis=-1, keepdims=True)
                mx = cm if mx is None else jnp.maximum(mx, cm)
            l = jnp.zeros((tq, 1), F32)
            acc = jnp.zeros((tq, LANES), F32)
            for c in range(nk):
                p = jnp.exp2(s_sc[half, :, c * tk:(c + 1) * tk] - mx)
                l += jnp.sum(p, axis=-1, keepdims=True)
                acc += jnp.dot(p.astype(BF16), v_ref[c * tk:(c + 1) * tk, :], preferred_element_type=F32)
            outs.append(acc / l)
        o_ref[j] = jnp.where(lo, outs[0], outs[1]).astype(o_ref.dtype)
        return carry

    lax.fori_loop(0, q_ref.shape[0], slab, 0)


def _attn_call(q, kt, v, tok_off, sq):
    n_slabs = q.shape[0]
    b, sk, _ = v.shape
    tq = min(ATT_Q_TILE, sq)
    tk = min(ATT_K_TILE, sk)
    nq = sq // tq
    off = tok_off // tq
    return pl.pallas_call(
        functools.partial(_attn_kernel, tk=tk),
        out_shape=jax.ShapeDtypeStruct((n_slabs, b * sq, LANES), BF16),
        grid=(b, nq),
        in_specs=[
            pl.BlockSpec((n_slabs, tq, LANES), lambda i, j: (0, off + i * nq + j, 0)),
            pl.BlockSpec((None, ATT_KV_W, sk), lambda i, j: (i, 0, 0)),
            pl.BlockSpec((None, sk, ATT_KV_W), lambda i, j: (i, 0, 0)),
        ],
        out_specs=pl.BlockSpec((n_slabs, tq, LANES), lambda i, j: (0, i * nq + j, 0)),
        scratch_shapes=[pltpu.VMEM((2, tq, sk), F32)],
        compiler_params=_cparams(("arbitrary", "arbitrary")),
        name="attention",
    )(q, kt, v)


def _scan_kernel(mf_ref, mb_ref, rf_ref, rb_ref, gcf_ref, gcb_ref, grf_ref, grb_ref, rd_ref,
                 c0_ref, n0_ref, m0_ref, s0_ref,
                 hf_ref, hb_ref, of_ref, ob_ref, cout_ref, nout_ref, mout_ref, sout_ref,
                 c_st, n_st, m_st, s_st, dec_st, qdec_st, kdec_st, cdec_st):
    L = CHUNK
    b_idx = pl.program_id(0)
    j = pl.program_id(1)
    nc = pl.num_programs(1)
    lane = _lane_iota()
    lo = lane < HALF
    row_i = lax.broadcasted_iota(jnp.int32, (L, L), 0)
    col_i = lax.broadcasted_iota(jnp.int32, (L, L), 1)
    blockmask = (row_i >> 6) == (col_i >> 6)
    causal = (row_i >= col_i, row_i <= col_i)

    @pl.when(jnp.logical_and(b_idx == 0, j == 0))
    def _init_tables():
        lg = _log_sigmoid(rd_ref[...])
        pos = lax.broadcasted_iota(jnp.int32, (L, 1), 0).astype(F32)
        diff = (row_i - col_i).astype(F32)
        for d in range(2):
            sd = diff if d == 0 else -diff
            for p in range(2):
                qd, kd, cd = [], [], []
                for e in range(2):
                    r = d * RET_HEADS + 2 * p + e
                    g = lg[r:r + 1, 0:1]
                    dec_st[d, 2 * p + e] = jnp.where(sd >= 0, jnp.exp(g * jnp.maximum(sd, 0.0)), 0.0)
                    if d == 0:
                        qd.append(jnp.exp(g * (pos + 1.0)))
                        kd.append(jnp.exp(g * (L - 1.0 - pos)))
                    else:
                        qd.append(jnp.exp(g * (L - pos)))
                        kd.append(jnp.exp(g * pos))
                    cd.append(jnp.exp(g * float(L)))
                qdec_st[d, p] = jnp.where(lo, qd[0], qd[1])
                kdec_st[d, p] = jnp.where(lo, kd[0], kd[1])
                cdec_st[d, p] = jnp.where(lo, cd[0], cd[1])

    @pl.when(j == 0)
    def _load_state():
        c_st[...] = c0_ref[...]
        n_st[...] = n0_ref[...]
        m_st[...] = m0_ref[...]
        s_st[...] = s0_ref[...]

    tri = (row_i >= col_i).astype(BF16)
    triu = (row_i <= col_i).astype(BF16)

    def cumsums(gc_ref, gr_ref, d):
        a_col, a_row = (tri, triu) if d == 0 else (triu, tri)
        col = sum(jnp.dot(a_col, part, preferred_element_type=F32) for part in _split3(gc_ref[...]))
        rowv = sum(jnp.dot(part, a_row, preferred_element_type=F32) for part in _split3(gr_ref[...]))
        return col, rowv

    for d, (m_ref, r_ref, gc_ref, gr_ref, h_out, o_out) in enumerate(
            ((mf_ref, rf_ref, gcf_ref, grf_ref, hf_ref, of_ref),
             (mb_ref, rb_ref, gcb_ref, grb_ref, hb_ref, ob_ref))):
        gcol = gc_ref[...]
        grow = gr_ref[...]
        cum_col, cum_row = cumsums(gc_ref, gr_ref, d)
        gi = 2 * ML_HEADS * d
        gf = gi + ML_HEADS
        last = L - 1 if d == 0 else 0
        for p in range(2):
            sl = slice(p * LANES, (p + 1) * LANES)
            q2 = m_ref[:, p * LANES:(p + 1) * LANES]
            k2 = m_ref[:, ML_W + p * LANES:ML_W + (p + 1) * LANES]
            v2 = m_ref[:, 2 * ML_W + p * LANES:2 * ML_W + (p + 1) * LANES]
            c2 = c_st[d, p]
            n2 = n_st[d, p]
            q_c = jnp.dot(q2, c2.astype(BF16), preferred_element_type=F32)
            q_n = q2.astype(F32) * n2
            h_e, ws_e, wc_e = [], [], []
            for e in range(2):
                hd = 2 * p + e
                hm = lo if e == 0 else jnp.logical_not(lo)
                m_prev = m_st[d, hd][:, 0:1]
                i_col = gcol[:, gi + hd:gi + hd + 1]
                b_col = cum_col[:, gf + hd:gf + hd + 1]
                i_row = grow[gi + hd:gi + hd + 1, :]
                b_row = cum_row[gf + hd:gf + hd + 1, :]
                b_last = b_col[last:last + 1, :]
                a = b_col + m_prev
                dm = jnp.where(causal[d], b_col - b_row + i_row, -jnp.inf)
                mt = jnp.maximum(a, jnp.max(dm, axis=-1, keepdims=True))
                wd = jnp.exp(dm - mt)
                wa = jnp.exp(a - mt)
                qe = jnp.where(hm, q2, jnp.zeros_like(q2))
                s = lax.dot_general(qe, k2, (((1,), (1,)), ((), ())), preferred_element_type=F32) * wd
                sv = jnp.dot(s.astype(BF16), v2, preferred_element_type=F32)
                den = jnp.sum(s, axis=-1, keepdims=True) + wa * jnp.sum(
                    jnp.where(hm, q_n, 0.0), axis=-1, keepdims=True)
                dd = jnp.maximum(jnp.abs(den), jnp.exp(-mt))
                h_e.append((sv + wa * q_c) / dd)
                g_col = b_last - b_col + i_col
                m_new = jnp.maximum(b_last + m_prev, jnp.max(g_col, axis=0, keepdims=True))
                wc_e.append(jnp.exp(b_last + m_prev - m_new))
                ws_e.append(jnp.exp(g_col - m_new))
                m_st[d, hd] = jnp.broadcast_to(m_new, (1, LANES))
            h_out[:, sl] = jnp.where(lo, h_e[0], h_e[1])
            kw = k2.astype(F32) * jnp.where(lo, ws_e[0], ws_e[1])
            wc2 = jnp.where(lo, wc_e[0], wc_e[1])
            upd = jnp.dot(kw.T.astype(BF16), v2, preferred_element_type=F32)
            c_st[d, p] = jnp.where(blockmask, wc2 * c2 + upd, 0.0)
            n_st[d, p] = wc2 * n2 + jnp.sum(kw, axis=0, keepdims=True)
            q2 = r_ref[:, p * LANES:(p + 1) * LANES]
            k2 = r_ref[:, RET_W + p * LANES:RET_W + (p + 1) * LANES]
            v2 = r_ref[:, 2 * RET_W + p * LANES:2 * RET_W + (p + 1) * LANES]
            s2 = s_st[d, p]
            q_s = jnp.dot(q2, s2.astype(BF16), preferred_element_type=F32) * qdec_st[d, p]
            sv_e = []
            for e in range(2):
                hm = lo if e == 0 else jnp.logical_not(lo)
                qe = jnp.where(hm, q2, jnp.zeros_like(q2))
                s = lax.dot_general(qe, k2, (((1,), (1,)), ((), ())), preferred_element_type=F32)
                s = s * dec_st[d, 2 * p + e]
                sv_e.append(jnp.dot(s.astype(BF16), v2, preferred_element_type=F32))
            o_out[:, sl] = jnp.where(lo, sv_e[0], sv_e[1]) + q_s
            kw = k2.astype(F32) * kdec_st[d, p]
            upd = jnp.dot(kw.T.astype(BF16), v2, preferred_element_type=F32)
            s_st[d, p] = jnp.where(blockmask, cdec_st[d, p] * s2 + upd, 0.0)

    @pl.when(j == nc - 1)
    def _store_state():
        cout_ref[...] = c_st[...]
        nout_ref[...] = n_st[...]
        mout_ref[...] = m_st[...]
        sout_ref[...] = s_st[...]


def _scan_call(mqkv, rqkv, gcol, grow, rd, c0, n0, m0, s0, tok_off, s):
    b = c0.shape[0]
    L = CHUNK
    nc = s // L
    off = tok_off // L
    fwd = lambda w_, o=off: pl.BlockSpec((L, w_), lambda i, j: (o + i * nc + j, 0))
    bwd = lambda w_, o=off: pl.BlockSpec((L, w_), lambda i, j: (o + i * nc + nc - 1 - j, 0))
    st = lambda shp: pl.BlockSpec((None,) + shp, lambda i, j: (i,) + (0,) * len(shp))
    pair_mat = (2, 2, LANES, LANES)
    pair_vec = (2, 2, 1, LANES)
    head_vec = (2, ML_HEADS, 1, LANES)
    out_shape = [
        jax.ShapeDtypeStruct((b * s, ML_W), F32), jax.ShapeDtypeStruct((b * s, ML_W), F32),
        jax.ShapeDtypeStruct((b * s, RET_W), F32), jax.ShapeDtypeStruct((b * s, RET_W), F32),
        jax.ShapeDtypeStruct((b,) + pair_mat, F32), jax.ShapeDtypeStruct((b,) + pair_vec, F32),
        jax.ShapeDtypeStruct((b,) + head_vec, F32), jax.ShapeDtypeStruct((b,) + pair_mat, F32),
    ]
    return pl.pallas_call(
        _scan_kernel,
        out_shape=out_shape,
        grid=(b, nc),
        in_specs=[
            fwd(3 * ML_W), bwd(3 * ML_W), fwd(3 * RET_W), bwd(3 * RET_W),
            fwd(LANES), bwd(LANES),
            pl.BlockSpec((N_GATES, L), lambda i, j: (0, off + i * nc + j)),
            pl.BlockSpec((N_GATES, L), lambda i, j: (0, off + i * nc + nc - 1 - j)),
            pl.BlockSpec((8, LANES), lambda i, j: (0, 0)),
            st(pair_mat), st(pair_vec), st(head_vec), st(pair_mat),
        ],
        out_specs=[fwd(ML_W, 0), bwd(ML_W, 0), fwd(RET_W, 0), bwd(RET_W, 0),
                   st(pair_mat), st(pair_vec), st(head_vec), st(pair_mat)],
        scratch_shapes=[
            pltpu.VMEM(pair_mat, F32), pltpu.VMEM(pair_vec, F32), pltpu.VMEM(head_vec, F32),
            pltpu.VMEM(pair_mat, F32),
            pltpu.VMEM((2, RET_HEADS, L, L), F32), pltpu.VMEM((2, 2, L, LANES), F32),
            pltpu.VMEM((2, 2, L, LANES), F32), pltpu.VMEM((2, 2, 1, LANES), F32),
        ],
        compiler_params=_cparams(("arbitrary", "arbitrary")),
        name="scan_mixers",
    )(mqkv, mqkv, rqkv, rqkv, gcol, gcol, grow, grow, rd, c0, n0, m0, s0)


def _merge_kernel(x_ref, mod_ref, att1_ref, att2_ref, hf1_ref, hb1_ref, of1_ref, ob1_ref, hf2_ref, hb2_ref,
                  of2_ref, ob2_ref, mo_ref, rg_ref, bg_ref, wb_ref, wo_ref, gn_ref, *rest, moe, n1_tiles):
    sc1_refs = (hf1_ref, hb1_ref, of1_ref, ob1_ref)
    sc2_refs = (hf2_ref, hb2_ref, of2_ref, ob2_ref)
    if moe:
        router_ref, x1_ref, h2_ref, gates_ref = rest
    else:
        x1_ref, h2_ref = rest
    d = x_ref.shape[1]
    first = pl.program_id(0) < n1_tiles
    pick = lambda r1, r2, idx: jnp.where(first, r1[idx], r2[idx])
    mean_mat = _head_mean_matrix()
    y = None
    for s in range(ATT_Q_W // LANES):
        part = jnp.dot(pick(att1_ref, att2_ref, s), wb_ref[s * LANES:(s + 1) * LANES, :],
                       preferred_element_type=F32)
        y = part if y is None else y + part
    y = y * _sigmoid(bg_ref[:, 0:d])
    ml, ret = [], []
    for s in range(ML_W // LANES):
        sl = (slice(None), slice(s * LANES, (s + 1) * LANES))
        hf, hb, of, ob = (pick(r1, r2, sl) for r1, r2 in zip(sc1_refs, sc2_refs))
        ml.append((_head_rms(hf + hb, mean_mat) * _sigmoid(mo_ref[sl])).astype(BF16))
        rg = rg_ref[sl]
        ret.append((_head_rms(of + ob, mean_mat) * (rg * _sigmoid(rg))).astype(BF16))
    ml = jnp.concatenate(ml, axis=1)
    ret = jnp.concatenate(ret, axis=1)
    y += jnp.dot(ml, wb_ref[ATT_Q_W:ATT_Q_W + ML_W, :], preferred_element_type=F32) * _sigmoid(bg_ref[:, d:2 * d])
    y += jnp.dot(ret, wb_ref[ATT_Q_W + ML_W:, :], preferred_element_type=F32) * _sigmoid(bg_ref[:, 2 * d:3 * d])
    y = jnp.dot(y.astype(BF16), wo_ref[...], preferred_element_type=F32)
    x1 = x_ref[...] + mod_ref[2:3, :] * y
    x1_ref[...] = x1
    ms = jnp.mean(x1 * x1, axis=-1, keepdims=True)
    h2 = x1 * lax.rsqrt(ms + EPS) * gn_ref[...]
    h2 = h2 * (1.0 + mod_ref[4:5, :]) + mod_ref[3:4, :]
    h2_ref[...] = h2.astype(BF16)
    if moe:
        hh = h2.astype(BF16)
        hl = (h2 - hh.astype(F32)).astype(BF16)
        r = router_ref[...]
        rh = r.astype(BF16)
        rl = (r - rh.astype(F32)).astype(BF16)
        logits = (jnp.dot(hh, rh, preferred_element_type=F32) + jnp.dot(hl, rh, preferred_element_type=F32)
                  + jnp.dot(hh, rl, preferred_element_type=F32))
        lane = _lane_iota()
        lg = jnp.where(lane < N_EXPERTS, logits, -jnp.inf)
        m1 = jnp.max(lg, axis=-1, keepdims=True)
        i1 = jnp.min(jnp.where(lg == m1, lane, LANES), axis=-1, keepdims=True)
        sel1 = lane == i1
        lg2 = jnp.where(sel1, -jnp.inf, lg)
        m2 = jnp.max(lg2, axis=-1, keepdims=True)
        i2 = jnp.min(jnp.where(lg2 == m2, lane, LANES), axis=-1, keepdims=True)
        sel2 = lane == i2
        e2 = jnp.exp(m2 - m1)
        den = 1.0 + e2
        gates_ref[...] = jnp.where(sel1, 1.0 / den, 0.0) + jnp.where(sel2, e2 / den, 0.0)


def _merge_call(x, mod, att1, att2, scan1, scan2, mo, rg, bg, wb, wo, gn, router, n1, s2):
    n, d = x.shape
    tm = TOK_TILE
    n1t = n1 // tm
    t2 = s2 // tm
    moe = router is not None
    n_slabs = att1.shape[0]

    def mod_row(t):
        return jnp.where(t < n1t, 0, 1 + (t - n1t) // t2)

    g1 = lambda t: jnp.minimum(t, n1t - 1)
    g2 = lambda t: jnp.maximum(t - n1t, 0)
    row = lambda w_: pl.BlockSpec((tm, w_), lambda t: (t, 0))
    row1 = lambda w_: pl.BlockSpec((tm, w_), lambda t: (g1(t), 0))
    row2 = lambda w_: pl.BlockSpec((tm, w_), lambda t: (g2(t), 0))
    const = lambda shp: pl.BlockSpec(shp, lambda t: (0,) * len(shp))
    in_specs = [row(d), pl.BlockSpec((None, 6, d), lambda t: (mod_row(t), 0, 0)),
                pl.BlockSpec((n_slabs, tm, LANES), lambda t: (0, g1(t), 0)),
                pl.BlockSpec((n_slabs, tm, LANES), lambda t: (0, g2(t), 0)),
                row1(ML_W), row1(ML_W), row1(RET_W), row1(RET_W),
                row2(ML_W), row2(ML_W), row2(RET_W), row2(RET_W),
                row(ML_W), row(RET_W), row(3 * d),
                const((d, d)), const((d, d)), const((1, d))]
    args = [x, mod, att1, att2, *scan1, *scan2, mo, rg, bg, wb, wo, gn]
    out_shape = [jax.ShapeDtypeStruct((n, d), F32), jax.ShapeDtypeStruct((n, d), BF16)]
    out_specs = [row(d), row(d)]
    if moe:
        in_specs.append(const((d, LANES)))
        args.append(router)
        out_shape.append(jax.ShapeDtypeStruct((n, LANES), F32))
        out_specs.append(row(LANES))
    return pl.pallas_call(
        functools.partial(_merge_kernel, moe=moe, n1_tiles=n1t),
        out_shape=out_shape,
        grid=(n // tm,),
        in_specs=in_specs,
        out_specs=out_specs,
        compiler_params=_cparams(("arbitrary",)),
        name="merge_out",
    )(*args)


def _final_norm(x, g):
    ms = jnp.mean(x * x, axis=-1, keepdims=True)
    return x * lax.rsqrt(ms + EPS) * g


def _ffn_kernel(h_ref, x1_ref, mod_ref, wg_ref, wu_ref, wd_ref, fn_ref, o_ref, *, n_chunks, final):
    h = h_ref[...]
    f = wg_ref.shape[1]
    fc = f // n_chunks
    acc = jnp.zeros(o_ref.shape, F32)
    for c in range(n_chunks):
        sl = slice(c * fc, (c + 1) * fc)
        g = jnp.dot(h, wg_ref[:, sl], preferred_element_type=F32)
        u = jnp.dot(h, wu_ref[:, sl], preferred_element_type=F32)
        a = (g * _sigmoid(g) * u).astype(BF16)
        acc += jnp.dot(a, wd_ref[sl, :], preferred_element_type=F32)
    x2 = x1_ref[...] + mod_ref[5:6, :] * acc
    o_ref[...] = _final_norm(x2, fn_ref[...]) if final else x2


def _ffn_call(h2, x1, mod, wg, wu, wd, fn, n1, s2, final):
    n, d = x1.shape
    f = wg.shape[1]
    tm = FFN_TILE

    def mod_row(t):
        return jnp.where(t * tm < n1, 0, 1 + (t * tm - n1) // s2)

    row = lambda dt: pl.BlockSpec((tm, d), lambda t: (t, 0))
    const = lambda shp: pl.BlockSpec(shp, lambda t: (0,) * len(shp), pipeline_mode=pl.Buffered(1))
    return pl.pallas_call(
        functools.partial(_ffn_kernel, n_chunks=2, final=final),
        out_shape=jax.ShapeDtypeStruct((n, d), F32),
        grid=(n // tm,),
        in_specs=[row(BF16), row(F32), pl.BlockSpec((None, 6, d), lambda t: (mod_row(t), 0, 0)),
                  const((d, f)), const((d, f)), const((f, d)), pl.BlockSpec((1, d), lambda t: (0, 0))],
        out_specs=row(F32),
        compiler_params=_cparams(("arbitrary",)),
        name="ffn_dense",
    )(h2, x1, mod, wg, wu, wd, fn)


def _moe_kernel(h_ref, gates_ref, x1_ref, mod_ref, wg_ref, wu_ref, wd_ref, fn_ref, o_ref, acc_ref, *, final):
    e = pl.program_id(1)

    @pl.when(e == 0)
    def _zero():
        acc_ref[...] = jnp.zeros_like(acc_ref)

    h = h_ref[...]
    g = jnp.dot(h, wg_ref[...], preferred_element_type=F32)
    u = jnp.dot(h, wu_ref[...], preferred_element_type=F32)
    a = (g * _sigmoid(g) * u).astype(BF16)
    y = jnp.dot(a, wd_ref[...], preferred_element_type=F32)
    gate = jnp.sum(jnp.where(_lane_iota() == e, gates_ref[...], 0.0), axis=-1, keepdims=True)
    acc_ref[...] += gate * y

    @pl.when(e == pl.num_programs(1) - 1)
    def _finish():
        x2 = x1_ref[...] + mod_ref[5:6, :] * acc_ref[...]
        o_ref[...] = _final_norm(x2, fn_ref[...]) if final else x2


def _moe_call(h2, gates, x1, mod, wg, wu, wd, fn, n1, s2, final):
    n, d = x1.shape
    ne, _, f = wg.shape
    tm = FFN_TILE

    def mod_row(t):
        return jnp.where(t * tm < n1, 0, 1 + (t * tm - n1) // s2)

    row = lambda w_: pl.BlockSpec((tm, w_), lambda t, e: (t, 0))
    return pl.pallas_call(
        functools.partial(_moe_kernel, final=final),
        out_shape=jax.ShapeDtypeStruct((n, d), F32),
        grid=(n // tm, ne),
        in_specs=[row(d), row(LANES), row(d), pl.BlockSpec((None, 6, d), lambda t, e: (mod_row(t), 0, 0)),
                  pl.BlockSpec((None, d, f), lambda t, e: (e, 0, 0)),
                  pl.BlockSpec((None, d, f), lambda t, e: (e, 0, 0)),
                  pl.BlockSpec((None, f, d), lambda t, e: (e, 0, 0)),
                  pl.BlockSpec((1, d), lambda t, e: (0, 0))],
        out_specs=row(d),
        scratch_shapes=[pltpu.VMEM((tm, d), F32)],
        compiler_params=_cparams(("arbitrary", "arbitrary")),
        name="moe_dense",
    )(h2, gates, x1, mod, wg, wu, wd, fn)


def _pair_q_heads(a, axis):
    g = N_HEADS // N_KV_HEADS
    shp = a.shape
    a = a.reshape(shp[:axis] + (N_KV_HEADS, g, HEAD_DIM) + shp[axis + 1:])
    a = jnp.swapaxes(a, axis, axis + 1)
    return a.reshape(shp)


def _prep_in_weights(w_in, b_in):
    g0 = ATT_Q_W + 2 * ATT_KV_W + 4 * ML_W
    pad = N_IN_PAD - w_in.shape[-1]

    def reorder(a):
        parts = [_pair_q_heads(a[..., :ATT_Q_W], a.ndim - 1), a[..., ATT_Q_W:g0], a[..., g0 + N_GATES:],
                 a[..., g0:g0 + N_GATES], jnp.zeros(a.shape[:-1] + (pad,), a.dtype)]
        return jnp.concatenate(parts, axis=-1)

    return reorder(w_in).astype(BF16), reorder(b_in)[:, None, :]


def _rope_tables(s2, tm):
    pos = jnp.arange(s2)
    rowp = (pos // GRID_W).astype(F32)
    colp = (pos % GRID_W).astype(F32)
    quarter = HEAD_DIM // 4
    inv = ROPE_BASE ** (-jnp.arange(quarter, dtype=F32) / quarter)
    ang_r = rowp[:, None] * inv
    ang_c = colp[:, None] * inv
    cos_h = jnp.concatenate([jnp.cos(ang_r), jnp.cos(ang_r), jnp.cos(ang_c), jnp.cos(ang_c)], axis=1)
    sin_h = jnp.concatenate([-jnp.sin(ang_r), jnp.sin(ang_r), -jnp.sin(ang_c), jnp.sin(ang_c)], axis=1)
    cos_t = jnp.concatenate([cos_h, cos_h], axis=1)
    sin_t = jnp.concatenate([sin_h, sin_h], axis=1)
    cos_t = jnp.concatenate([jnp.ones((tm, LANES), F32), cos_t], axis=0)
    sin_t = jnp.concatenate([jnp.zeros((tm, LANES), F32), sin_t], axis=0)
    return cos_t, sin_t


def _pair_blockdiag(m):
    b, nd, h, d, _ = m.shape
    m = m.reshape(b, nd, h // 2, 2, d, d)
    z = jnp.zeros_like(m[:, :, :, 0])
    top = jnp.concatenate([m[:, :, :, 0], z], axis=-1)
    bot = jnp.concatenate([z, m[:, :, :, 1]], axis=-1)
    return jnp.concatenate([top, bot], axis=-2)


def _pair_unblock(m):
    b, nd, p, _, _ = m.shape
    a = m[:, :, :, :HALF, :HALF]
    c = m[:, :, :, HALF:, HALF:]
    return jnp.stack([a, c], axis=3).reshape(b, nd, 2 * p, HALF, HALF)


def kernel(x_prompt, x_sample, c, cache_attn_k, cache_attn_v, state_mlstm_c, state_mlstm_n, state_mlstm_m,
           state_ret_s, c_ctx, w_ada, b_ada, norm_mix, norm_ffn, w_in, b_in, q_norm, k_norm, ret_decay,
           w_branch, w_out, ffn_w_gate, ffn_w_up, ffn_w_down, moe_router, moe_w_gate, moe_w_up, moe_w_down,
           final_norm):
    b1, s1, d = x_prompt.shape
    b2, s2, _ = x_sample.shape
    depth = w_in.shape[0]
    n1, n2 = b1 * s1, b2 * s2
    assert s1 % TOK_TILE == 0 and s2 % FFN_TILE == 0 and n1 % FFN_TILE == 0 and b2 + 1 <= 8

    x = jnp.concatenate([x_prompt.reshape(n1, d), x_sample.reshape(n2, d)], axis=0)
    c8 = jnp.concatenate([c_ctx[None, :], c, jnp.zeros((8 - 1 - b2, d), F32)], axis=0)
    mod_all = _ada_call(c8, w_ada, b_ada).reshape(depth, 8, 6, d)

    w_in_p, b_in_p = _prep_in_weights(w_in, b_in)
    cos_t, sin_t = _rope_tables(s2, TOK_TILE)
    wb = jnp.concatenate([_pair_q_heads(w_branch[:, :ATT_Q_W], 1), w_branch[:, ATT_Q_W:]], axis=1).astype(BF16)
    wo = w_out.astype(BF16)
    fn = final_norm[None, :]

    zeros_like_state = lambda shp: jnp.zeros((b1,) + shp, F32)
    states = []
    for l in range(depth):
        mod = mod_all[l]
        qg = jnp.tile(q_norm[l], 2)[None, :]
        kg = jnp.tile(k_norm[l], 2)[None, :]
        q, k, v, kb, vb, mqkv, rqkv, mo, rg, bg, gcol, grow = _inproj_call(
            x, mod, norm_mix[l][None, :], w_in_p[l], b_in_p[l], qg, kg, cos_t, sin_t, n1, s2)

        kt1 = jnp.swapaxes(kb[:n1].reshape(b1, s1, ATT_KV_W), 1, 2)
        att1 = _attn_call(q, kt1, vb[:n1].reshape(b1, s1, ATT_KV_W), 0, s1)
        k2 = jnp.concatenate([cache_attn_k[:, l].reshape(b2, -1, ATT_KV_W).astype(BF16),
                              kb[n1:].reshape(b2, s2, ATT_KV_W)], axis=1)
        v2 = jnp.concatenate([cache_attn_v[:, l].reshape(b2, -1, ATT_KV_W).astype(BF16),
                              vb[n1:].reshape(b2, s2, ATT_KV_W)], axis=1)
        att2 = _attn_call(q, jnp.swapaxes(k2, 1, 2), v2, n1, s2)

        rd = jnp.broadcast_to(ret_decay[l].reshape(2 * RET_HEADS, 1), (2 * RET_HEADS, LANES))
        r1 = _scan_call(mqkv, rqkv, gcol, grow, rd,
                        zeros_like_state((2, 2, LANES, LANES)), zeros_like_state((2, 2, 1, LANES)),
                        zeros_like_state((2, ML_HEADS, 1, LANES)), zeros_like_state((2, 2, LANES, LANES)), 0, s1)
        c0 = _pair_blockdiag(state_mlstm_c[:, l])
        n0 = state_mlstm_n[:, l].reshape(b2, 2, 2, 1, LANES)
        m0 = jnp.broadcast_to(state_mlstm_m[:, l][..., None, None], (b2, 2, ML_HEADS, 1, LANES))
        s0 = _pair_blockdiag(state_ret_s[:, l])
        r2 = _scan_call(mqkv, rqkv, gcol, grow, rd, c0, n0, m0, s0, n1, s2)
        states.append((k[:n1].reshape(b1, s1, N_KV_HEADS, HEAD_DIM), v[:n1].reshape(b1, s1, N_KV_HEADS, HEAD_DIM),
                       _pair_unblock(r1[4]), r1[5].reshape(b1, 2, ML_HEADS, HEAD_DIM), r1[6][..., 0, 0],
                       _pair_unblock(r1[7])))

        moe = l % 2 == 1
        jj = l // 2
        router = jnp.pad(moe_router[jj], ((0, 0), (0, LANES - N_EXPERTS))) if moe else None
        outs = _merge_call(x, mod, att1, att2, r1[:4], r2[:4], mo, rg, bg, wb[l], wo[l], norm_ffn[l][None, :],
                           router, n1, s2)
        final = l == depth - 1
        if moe:
            x1, h2, gates = outs
            x = _moe_call(h2, gates, x1, mod, moe_w_gate[jj].astype(BF16), moe_w_up[jj].astype(BF16),
                          moe_w_down[jj].astype(BF16), fn, n1, s2, final)
        else:
            x1, h2 = outs
            x = _ffn_call(h2, x1, mod, ffn_w_gate[jj].astype(BF16), ffn_w_up[jj].astype(BF16),
                          ffn_w_down[jj].astype(BF16), fn, n1, s2, final)

    y_prompt = x[:n1].reshape(b1, s1, d)
    y_sample = x[n1:].reshape(b2, s2, d)
    stack = lambda i: jnp.stack([s[i] for s in states], axis=1)
    return (y_prompt, y_sample, stack(0), stack(1), stack(2), stack(3), stack(4), stack(5))
```

```python
import functools

import jax
import jax.numpy as jnp
from jax import lax
from jax.experimental import pallas as pl
from jax.experimental.pallas import tpu as pltpu

F32 = jnp.float32
BF16 = jnp.bfloat16

N_HEADS = 8
N_KV_HEADS = 2
HEAD_DIM = 64
ML_HEADS = 4
RET_HEADS = 4
GRID_W = 64
CHUNK = 128
ROPE_BASE = 10000.0
EPS = 1e-6
N_EXPERTS = 8
LANES = 128
HALF = 64

ATT_Q_W = N_HEADS * HEAD_DIM
ATT_KV_W = N_KV_HEADS * HEAD_DIM
ML_W = ML_HEADS * HEAD_DIM
RET_W = RET_HEADS * HEAD_DIM
N_GATES = 4 * ML_HEADS

TOK_TILE = 256
FFN_TILE = 512
MOE_TILE = 512
ATT_Q_TILE = 256
ATT_K_TILE = 512
LOG2E = 1.4426950408889634
Q_SCALE = 0.125 * LOG2E
VMEM_LIMIT = 56 * 1024 * 1024

O_AQ, O_AK, O_AV = 0, 512, 640
O_MQ, O_MK, O_MV, O_MO = 768, 1024, 1280, 1536
O_RQ, O_RK, O_RV, O_RG = 1792, 2048, 2304, 2560
O_BG = 2816
O_MG = 5888
N_IN_PAD = 6016


def _cparams(sem, vmem=VMEM_LIMIT):
    return pltpu.CompilerParams(dimension_semantics=sem, vmem_limit_bytes=vmem)


def _lane_iota(shape=(1, LANES)):
    return lax.broadcasted_iota(jnp.int32, shape, len(shape) - 1)


def _head_mean_matrix():
    r = lax.broadcasted_iota(jnp.int32, (LANES, LANES), 0) >> 6
    c = lax.broadcasted_iota(jnp.int32, (LANES, LANES), 1) >> 6
    return jnp.where(r == c, 1.0 / HALF, 0.0).astype(BF16)


def _head_rms(x, mean_mat):
    sq = x * x
    hi = sq.astype(BF16)
    lo = (sq - hi.astype(F32)).astype(BF16)
    ms = jnp.dot(hi, mean_mat, preferred_element_type=F32) + jnp.dot(lo, mean_mat, preferred_element_type=F32)
    return x * lax.rsqrt(ms + EPS)


def _rope(y, cos, sin, lane):
    up = pltpu.roll(y, LANES - 16, 1)
    dn = pltpu.roll(y, 16, 1)
    partner = jnp.where((lane & 31) < 16, up, dn)
    return y * cos + partner * sin


def _log_sigmoid(x):
    return jnp.minimum(x, 0.0) - jnp.log1p(jnp.exp(-jnp.abs(x)))


def _sigmoid(x):
    return 1.0 / (1.0 + jnp.exp(-x))


def _split3(x):
    h = x.astype(BF16)
    r = x - h.astype(F32)
    m = r.astype(BF16)
    l = (r - m.astype(F32)).astype(BF16)
    return h, m, l


def _ada_kernel(c_ref, w_ref, b_ref, o_ref):
    c = c_ref[...]
    a = (c * _sigmoid(c)).astype(BF16)
    o_ref[...] = jnp.dot(a, w_ref[...].astype(BF16), preferred_element_type=F32) + b_ref[...]


def _ada_call(c8, w_ada, b_ada):
    depth, d, n6 = w_ada.shape
    tn = 1536
    return pl.pallas_call(
        _ada_kernel,
        out_shape=jax.ShapeDtypeStruct((depth, 8, n6), F32),
        grid=(depth, n6 // tn),
        in_specs=[
            pl.BlockSpec((8, d), lambda l, j: (0, 0)),
            pl.BlockSpec((None, d, tn), lambda l, j: (l, 0, j)),
            pl.BlockSpec((None, 1, tn), lambda l, j: (l, 0, j)),
        ],
        out_specs=pl.BlockSpec((None, 8, tn), lambda l, j: (l, 0, j)),
        compiler_params=_cparams(("arbitrary", "arbitrary")),
        name="ada_mod",
    )(c8, w_ada, b_ada.reshape(depth, 1, n6))


def _inproj_kernel(x_ref, mod_ref, gn_ref, w_ref, b_ref, qg_ref, kg_ref, cos_ref, sin_ref,
                   q_ref, k_ref, v_ref, kb_ref, vb_ref, mqkv_ref, rqkv_ref, mo_ref, rg_ref, bg_ref, gcol_ref,
                   grow_ref):
    x = x_ref[...]
    ms = jnp.mean(x * x, axis=-1, keepdims=True)
    h = x * lax.rsqrt(ms + EPS) * gn_ref[...]
    h = h * (1.0 + mod_ref[1:2, :]) + mod_ref[0:1, :]
    hb = h.astype(BF16)

    def seg(a, b):
        return jnp.dot(hb, w_ref[:, a:b], preferred_element_type=F32) + b_ref[:, a:b]

    lane = _lane_iota()
    mean_mat = _head_mean_matrix()
    cos = cos_ref[...]
    sin = sin_ref[...]
    for s in range(ATT_Q_W // LANES):
        a = O_AQ + s * LANES
        y = _head_rms(seg(a, a + LANES), mean_mat) * qg_ref[...]
        q_ref[s] = (_rope(y, cos, sin, lane) * Q_SCALE).astype(BF16)
    y = _head_rms(seg(O_AK, O_AK + LANES), mean_mat) * kg_ref[...]
    y = _rope(y, cos, sin, lane)
    k_ref[...] = y
    kb_ref[...] = y.astype(BF16)
    y = seg(O_AV, O_AV + LANES)
    v_ref[...] = y
    vb_ref[...] = y.astype(BF16)

    mqkv_ref[:, 0:ML_W] = seg(O_MQ, O_MQ + ML_W).astype(BF16)
    mqkv_ref[:, ML_W:2 * ML_W] = (seg(O_MK, O_MK + ML_W) * 0.125).astype(BF16)
    mqkv_ref[:, 2 * ML_W:3 * ML_W] = seg(O_MV, O_MV + ML_W).astype(BF16)
    mo_ref[...] = seg(O_MO, O_MO + ML_W)

    for s in range(RET_W // LANES):
        a = O_RQ + s * LANES
        rqkv_ref[:, s * LANES:(s + 1) * LANES] = _rope(seg(a, a + LANES), cos, sin, lane).astype(BF16)
        a = O_RK + s * LANES
        rqkv_ref[:, RET_W + s * LANES:RET_W + (s + 1) * LANES] = (
            _rope(seg(a, a + LANES), cos, sin, lane) * 0.125).astype(BF16)
    rqkv_ref[:, 2 * RET_W:3 * RET_W] = seg(O_RV, O_RV + RET_W).astype(BF16)
    rg_ref[...] = seg(O_RG, O_RG + RET_W)
    bg_ref[...] = seg(O_BG, O_MG)

    g = seg(O_MG, O_MG + LANES)
    is_f = ((lane >> 2) & 1) == 1
    g = jnp.where(is_f, _log_sigmoid(g), g)
    gcol_ref[...] = g
    grow_ref[...] = g.T[0:N_GATES, :]


def _inproj_call(x, mod, gn, w, b, qg, kg, cos_t, sin_t, n1, s2):
    n, d = x.shape
    tm = TOK_TILE
    n1t = n1 // tm
    t2 = s2 // tm

    def mod_row(t):
        return jnp.where(t < n1t, 0, 1 + (t - n1t) // t2)

    def tab_row(t):
        return jnp.where(t < n1t, 0, 1 + (t - n1t) % t2)

    row = lambda w_: pl.BlockSpec((tm, w_), lambda t: (t, 0))
    const = lambda shp: pl.BlockSpec(shp, lambda t: (0,) * len(shp))
    n_slabs = ATT_Q_W // LANES
    out_shape = [
        jax.ShapeDtypeStruct((n_slabs, n, LANES), BF16),
        jax.ShapeDtypeStruct((n, ATT_KV_W), F32),
        jax.ShapeDtypeStruct((n, ATT_KV_W), F32),
        jax.ShapeDtypeStruct((n, ATT_KV_W), BF16),
        jax.ShapeDtypeStruct((n, ATT_KV_W), BF16),
        jax.ShapeDtypeStruct((n, 3 * ML_W), BF16),
        jax.ShapeDtypeStruct((n, 3 * RET_W), BF16),
        jax.ShapeDtypeStruct((n, ML_W), F32),
        jax.ShapeDtypeStruct((n, RET_W), F32),
        jax.ShapeDtypeStruct((n, 3 * d), F32),
        jax.ShapeDtypeStruct((n, LANES), F32),
        jax.ShapeDtypeStruct((N_GATES, n), F32),
    ]
    out_specs = [pl.BlockSpec((n_slabs, tm, LANES), lambda t: (0, t, 0)), row(ATT_KV_W), row(ATT_KV_W),
                 row(ATT_KV_W), row(ATT_KV_W), row(3 * ML_W), row(3 * RET_W), row(ML_W),
                 row(RET_W), row(3 * d), row(LANES), pl.BlockSpec((N_GATES, tm), lambda t: (0, t))]
    return pl.pallas_call(
        _inproj_kernel,
        out_shape=out_shape,
        grid=(n // tm,),
        in_specs=[
            row(d),
            pl.BlockSpec((None, 6, d), lambda t: (mod_row(t), 0, 0)),
            const((1, d)),
            const((d, N_IN_PAD)),
            const((1, N_IN_PAD)),
            const((1, LANES)),
            const((1, LANES)),
            pl.BlockSpec((tm, LANES), lambda t: (tab_row(t), 0)),
            pl.BlockSpec((tm, LANES), lambda t: (tab_row(t), 0)),
        ],
        out_specs=out_specs,
        compiler_params=_cparams(("arbitrary",)),
        name="in_proj",
    )(x, mod, gn, w, b, qg, kg, cos_t, sin_t)


def _attn_kernel(q_ref, kt_ref, v_ref, o_ref, s_sc, *, tk):
    tq = q_ref.shape[1]
    nk = kt_ref.shape[1] // tk
    lo = _lane_iota() < HALF

    def slab(j, carry):
        qs = q_ref[j]
        outs = []
        for half in range(2):
            qm = jnp.where(lo if half == 0 else jnp.logical_not(lo), qs, jnp.zeros_like(qs))
            mx = None
            for c in range(nk):
                s = jnp.dot(qm, kt_ref[:, c * tk:(c + 1) * tk], preferred_element_type=F32)
                s_sc[half, :, c * tk:(c + 1) * tk] = s
                cm = jnp.max(s, axis=-1, keepdims=True)
                mx = cm if mx is None else jnp.maximum(mx, cm)
            l = jnp.zeros((tq, 1), F32)
            acc = jnp.zeros((tq, LANES), F32)
            for c in range(nk):
                p = jnp.exp2(s_sc[half, :, c * tk:(c + 1) * tk] - mx)
                l += jnp.sum(p, axis=-1, keepdims=True)
                acc += jnp.dot(p.astype(BF16), v_ref[c * tk:(c + 1) * tk, :], preferred_element_type=F32)
            outs.append(acc / l)
        o_ref[j] = jnp.where(lo, outs[0], outs[1]).astype(o_ref.dtype)
        return carry

    lax.fori_loop(0, q_ref.shape[0], slab, 0)


def _attn_call(q, kt, v, tok_off, sq):
    n_slabs = q.shape[0]
    b, sk, _ = v.shape
    tq = min(ATT_Q_TILE, sq)
    tk = min(ATT_K_TILE, sk)
    nq = sq // tq
    off = tok_off // tq
    return pl.pallas_call(
        functools.partial(_attn_kernel, tk=tk),
        out_shape=jax.ShapeDtypeStruct((n_slabs, b * sq, LANES), BF16),
        grid=(b, nq),
        in_specs=[
            pl.BlockSpec((n_slabs, tq, LANES), lambda i, j: (0, off + i * nq + j, 0)),
            pl.BlockSpec((None, ATT_KV_W, sk), lambda i, j: (i, 0, 0)),
            pl.BlockSpec((None, sk, ATT_KV_W), lambda i, j: (i, 0, 0)),
        ],
        out_specs=pl.BlockSpec((n_slabs, tq, LANES), lambda i, j: (0, i * nq + j, 0)),
        scratch_shapes=[pltpu.VMEM((2, tq, sk), F32)],
        compiler_params=_cparams(("arbitrary", "arbitrary")),
        name="attention",
    )(q, kt, v)


def _scan_kernel(mf_ref, mb_ref, rf_ref, rb_ref, gcf_ref, gcb_ref, grf_ref, grb_ref, rd_ref,
                 c0_ref, n0_ref, m0_ref, s0_ref,
                 hf_ref, hb_ref, of_ref, ob_ref, cout_ref, nout_ref, mout_ref, sout_ref,
                 c_st, n_st, m_st, s_st, dec_st, qdec_st, kdec_st, cdec_st):
    L = CHUNK
    b_idx = pl.program_id(0)
    j = pl.program_id(1)
    nc = pl.num_programs(1)
    lane = _lane_iota()
    lo = lane < HALF
    row_i = lax.broadcasted_iota(jnp.int32, (L, L), 0)
    col_i = lax.broadcasted_iota(jnp.int32, (L, L), 1)
    blockmask = (row_i >> 6) == (col_i >> 6)
    causal = (row_i >= col_i, row_i <= col_i)

    @pl.when(jnp.logical_and(b_idx == 0, j == 0))
    def _init_tables():
        lg = _log_sigmoid(rd_ref[...])
        pos = lax.broadcasted_iota(jnp.int32, (L, 1), 0).astype(F32)
        diff = (row_i - col_i).astype(F32)
        for d in range(2):
            sd = diff if d == 0 else -diff
            for p in range(2):
                qd, kd, cd = [], [], []
                for e in range(2):
                    r = d * RET_HEADS + 2 * p + e
                    g = lg[r:r + 1, 0:1]
                    dec_st[d, 2 * p + e] = jnp.where(sd >= 0, jnp.exp(g * jnp.maximum(sd, 0.0)), 0.0)
                    if d == 0:
                        qd.append(jnp.exp(g * (pos + 1.0)))
                        kd.append(jnp.exp(g * (L - 1.0 - pos)))
                    else:
                        qd.append(jnp.exp(g * (L - pos)))
                        kd.append(jnp.exp(g * pos))
                    cd.append(jnp.exp(g * float(L)))
                qdec_st[d, p] = jnp.where(lo, qd[0], qd[1])
                kdec_st[d, p] = jnp.where(lo, kd[0], kd[1])
                cdec_st[d, p] = jnp.where(lo, cd[0], cd[1])

    @pl.when(j == 0)
    def _load_state():
        c_st[...] = c0_ref[...]
        n_st[...] = n0_ref[...]
        m_st[...] = m0_ref[...]
        s_st[...] = s0_ref[...]

    tri = (row_i >= col_i).astype(BF16)
    triu = (row_i <= col_i).astype(BF16)

    def cumsums(gc_ref, gr_ref, d):
        a_col, a_row = (tri, triu) if d == 0 else (triu, tri)
        col = sum(jnp.dot(a_col, part, preferred_element_type=F32) for part in _split3(gc_ref[...]))
        rowv = sum(jnp.dot(part, a_row, preferred_element_type=F32) for part in _split3(gr_ref[...]))
        return col, rowv

    for d, (m_ref, r_ref, gc_ref, gr_ref, h_out, o_out) in enumerate(
            ((mf_ref, rf_ref, gcf_ref, grf_ref, hf_ref, of_ref),
             (mb_ref, rb_ref, gcb_ref, grb_ref, hb_ref, ob_ref))):
        gcol = gc_ref[...]
        grow = gr_ref[...]
        cum_col, cum_row = cumsums(gc_ref, gr_ref, d)
        gi = 2 * ML_HEADS * d
        gf = gi + ML_HEADS
        last = L - 1 if d == 0 else 0
        for p in range(2):
            sl = slice(p * LANES, (p + 1) * LANES)
            q2 = m_ref[:, p * LANES:(p + 1) * LANES]
            k2 = m_ref[:, ML_W + p * LANES:ML_W + (p + 1) * LANES]
            v2 = m_ref[:, 2 * ML_W + p * LANES:2 * ML_W + (p + 1) * LANES]
            c2 = c_st[d, p]
            n2 = n_st[d, p]
            q_c = jnp.dot(q2, c2.astype(BF16), preferred_element_type=F32)
            q_n = q2.astype(F32) * n2
            h_e, ws_e, wc_e = [], [], []
            for e in range(2):
                hd = 2 * p + e
                hm = lo if e == 0 else jnp.logical_not(lo)
                m_prev = m_st[d, hd][:, 0:1]
                i_col = gcol[:, gi + hd:gi + hd + 1]
                b_col = cum_col[:, gf + hd:gf + hd + 1]
                i_row = grow[gi + hd:gi + hd + 1, :]
                b_row = cum_row[gf + hd:gf + hd + 1, :]
                b_last = b_col[last:last + 1, :]
                a = b_col + m_prev
                dm = jnp.where(causal[d], b_col - b_row + i_row, -jnp.inf)
                mt = jnp.maximum(a, jnp.max(dm, axis=-1, keepdims=True))
                wd = jnp.exp(dm - mt)
                wa = jnp.exp(a - mt)
                qe = jnp.where(hm, q2, jnp.zeros_like(q2))
                s = lax.dot_general(qe, k2, (((1,), (1,)), ((), ())), preferred_element_type=F32) * wd
                sv = jnp.dot(s.astype(BF16), v2, preferred_element_type=F32)
                den = jnp.sum(s, axis=-1, keepdims=True) + wa * jnp.sum(
                    jnp.where(hm, q_n, 0.0), axis=-1, keepdims=True)
                dd = jnp.maximum(jnp.abs(den), jnp.exp(-mt))
                h_e.append((sv + wa * q_c) / dd)
                g_col = b_last - b_col + i_col
                m_new = jnp.maximum(b_last + m_prev, jnp.max(g_col, axis=0, keepdims=True))
                wc_e.append(jnp.exp(b_last + m_prev - m_new))
                ws_e.append(jnp.exp(g_col - m_new))
                m_st[d, hd] = jnp.broadcast_to(m_new, (1, LANES))
            h_out[:, sl] = jnp.where(lo, h_e[0], h_e[1])
            kw = k2.astype(F32) * jnp.where(lo, ws_e[0], ws_e[1])
            wc2 = jnp.where(lo, wc_e[0], wc_e[1])
            upd = jnp.dot(kw.T.astype(BF16), v2, preferred_element_type=F32)
            c_st[d, p] = jnp.where(blockmask, wc2 * c2 + upd, 0.0)
            n_st[d, p] = wc2 * n2 + jnp.sum(kw, axis=0, keepdims=True)
            q2 = r_ref[:, p * LANES:(p + 1) * LANES]
            k2 = r_ref[:, RET_W + p * LANES:RET_W + (p + 1) * LANES]
            v2 = r_ref[:, 2 * RET_W + p * LANES:2 * RET_W + (p + 1) * LANES]
            s2 = s_st[d, p]
            q_s = jnp.dot(q2, s2.astype(BF16), preferred_element_type=F32) * qdec_st[d, p]
            sv_e = []
            for e in range(2):
                hm = lo if e == 0 else jnp.logical_not(lo)
                qe = jnp.where(hm, q2, jnp.zeros_like(q2))
                s = lax.dot_general(qe, k2, (((1,), (1,)), ((), ())), preferred_element_type=F32)
                s = s * dec_st[d, 2 * p + e]
                sv_e.append(jnp.dot(s.astype(BF16), v2, preferred_element_type=F32))
            o_out[:, sl] = jnp.where(lo, sv_e[0], sv_e[1]) + q_s
            kw = k2.astype(F32) * kdec_st[d, p]
            upd = jnp.dot(kw.T.astype(BF16), v2, preferred_element_type=F32)
            s_st[d, p] = jnp.where(blockmask, cdec_st[d, p] * s2 + upd, 0.0)

    @pl.when(j == nc - 1)
    def _store_state():
        cout_ref[...] = c_st[...]
        nout_ref[...] = n_st[...]
        mout_ref[...] = m_st[...]
        sout_ref[...] = s_st[...]


def _scan_call(mqkv, rqkv, gcol, grow, rd, c0, n0, m0, s0, tok_off, s):
    b = c0.shape[0]
    L = CHUNK
    nc = s // L
    off = tok_off // L
    fwd = lambda w_, o=off: pl.BlockSpec((L, w_), lambda i, j: (o + i * nc + j, 0))
    bwd = lambda w_, o=off: pl.BlockSpec((L, w_), lambda i, j: (o + i * nc + nc - 1 - j, 0))
    st = lambda shp: pl.BlockSpec((None,) + shp, lambda i, j: (i,) + (0,) * len(shp))
    pair_mat = (2, 2, LANES, LANES)
    pair_vec = (2, 2, 1, LANES)
    head_vec = (2, ML_HEADS, 1, LANES)
    out_shape = [
        jax.ShapeDtypeStruct((b * s, ML_W), F32), jax.ShapeDtypeStruct((b * s, ML_W), F32),
        jax.ShapeDtypeStruct((b * s, RET_W), F32), jax.ShapeDtypeStruct((b * s, RET_W), F32),
        jax.ShapeDtypeStruct((b,) + pair_mat, F32), jax.ShapeDtypeStruct((b,) + pair_vec, F32),
        jax.ShapeDtypeStruct((b,) + head_vec, F32), jax.ShapeDtypeStruct((b,) + pair_mat, F32),
    ]
    return pl.pallas_call(
        _scan_kernel,
        out_shape=out_shape,
        grid=(b, nc),
        in_specs=[
            fwd(3 * ML_W), bwd(3 * ML_W), fwd(3 * RET_W), bwd(3 * RET_W),
            fwd(LANES), bwd(LANES),
            pl.BlockSpec((N_GATES, L), lambda i, j: (0, off + i * nc + j)),
            pl.BlockSpec((N_GATES, L), lambda i, j: (0, off + i * nc + nc - 1 - j)),
            pl.BlockSpec((8, LANES), lambda i, j: (0, 0)),
            st(pair_mat), st(pair_vec), st(head_vec), st(pair_mat),
        ],
        out_specs=[fwd(ML_W, 0), bwd(ML_W, 0), fwd(RET_W, 0), bwd(RET_W, 0),
                   st(pair_mat), st(pair_vec), st(head_vec), st(pair_mat)],
        scratch_shapes=[
            pltpu.VMEM(pair_mat, F32), pltpu.VMEM(pair_vec, F32), pltpu.VMEM(head_vec, F32),
            pltpu.VMEM(pair_mat, F32),
            pltpu.VMEM((2, RET_HEADS, L, L), F32), pltpu.VMEM((2, 2, L, LANES), F32),
            pltpu.VMEM((2, 2, L, LANES), F32), pltpu.VMEM((2, 2, 1, LANES), F32),
        ],
        compiler_params=_cparams(("arbitrary", "arbitrary")),
        name="scan_mixers",
    )(mqkv, mqkv, rqkv, rqkv, gcol, gcol, grow, grow, rd, c0, n0, m0, s0)


def _merge_kernel(x_ref, mod_ref, att1_ref, att2_ref, hf1_ref, hb1_ref, of1_ref, ob1_ref, hf2_ref, hb2_ref,
                  of2_ref, ob2_ref, mo_ref, rg_ref, bg_ref, wb_ref, wo_ref, gn_ref, *rest, moe, n1_tiles):
    sc1_refs = (hf1_ref, hb1_ref, of1_ref, ob1_ref)
    sc2_refs = (hf2_ref, hb2_ref, of2_ref, ob2_ref)
    if moe:
        router_ref, x1_ref, h2_ref, gates_ref = rest
    else:
        x1_ref, h2_ref = rest
    d = x_ref.shape[1]
    first = pl.program_id(0) < n1_tiles
    pick = lambda r1, r2, idx: jnp.where(first, r1[idx], r2[idx])
    mean_mat = _head_mean_matrix()
    y = None
    for s in range(ATT_Q_W // LANES):
        part = jnp.dot(pick(att1_ref, att2_ref, s), wb_ref[s * LANES:(s + 1) * LANES, :],
                       preferred_element_type=F32)
        y = part if y is None else y + part
    y = y * _sigmoid(bg_ref[:, 0:d])
    ml, ret = [], []
    for s in range(ML_W // LANES):
        sl = (slice(None), slice(s * LANES, (s + 1) * LANES))
        hf, hb, of, ob = (pick(r1, r2, sl) for r1, r2 in zip(sc1_refs, sc2_refs))
        ml.append((_head_rms(hf + hb, mean_mat) * _sigmoid(mo_ref[sl])).astype(BF16))
        rg = rg_ref[sl]
        ret.append((_head_rms(of + ob, mean_mat) * (rg * _sigmoid(rg))).astype(BF16))
    ml = jnp.concatenate(ml, axis=1)
    ret = jnp.concatenate(ret, axis=1)
    y += jnp.dot(ml, wb_ref[ATT_Q_W:ATT_Q_W + ML_W, :], preferred_element_type=F32) * _sigmoid(bg_ref[:, d:2 * d])
    y += jnp.dot(ret, wb_ref[ATT_Q_W + ML_W:, :], preferred_element_type=F32) * _sigmoid(bg_ref[:, 2 * d:3 * d])
    y = jnp.dot(y.astype(BF16), wo_ref[...], preferred_element_type=F32)
    x1 = x_ref[...] + mod_ref[2:3, :] * y
    x1_ref[...] = x1
    ms = jnp.mean(x1 * x1, axis=-1, keepdims=True)
    h2 = x1 * lax.rsqrt(ms + EPS) * gn_ref[...]
    h2 = h2 * (1.0 + mod_ref[4:5, :]) + mod_ref[3:4, :]
    h2_ref[...] = h2.astype(h2_ref.dtype)
    if moe:
        hh = h2.astype(BF16)
        hl = (h2 - hh.astype(F32)).astype(BF16)
        r = router_ref[...]
        rh = r.astype(BF16)
        rl = (r - rh.astype(F32)).astype(BF16)
        logits = (jnp.dot(hh, rh, preferred_element_type=F32) + jnp.dot(hl, rh, preferred_element_type=F32)
                  + jnp.dot(hh, rl, preferred_element_type=F32))
        lane = _lane_iota()
        lg = jnp.where(lane < N_EXPERTS, logits, -jnp.inf)
        m1 = jnp.max(lg, axis=-1, keepdims=True)
        i1 = jnp.min(jnp.where(lg == m1, lane, LANES), axis=-1, keepdims=True)
        sel1 = lane == i1
        lg2 = jnp.where(sel1, -jnp.inf, lg)
        m2 = jnp.max(lg2, axis=-1, keepdims=True)
        i2 = jnp.min(jnp.where(lg2 == m2, lane, LANES), axis=-1, keepdims=True)
        e2 = jnp.exp(m2 - m1)
        den = 1.0 + e2
        gates_ref[...] = jnp.where(lane == 0, i1.astype(F32), jnp.where(
            lane == 1, i2.astype(F32), jnp.where(lane == 2, 1.0 / den, jnp.where(lane == 3, e2 / den, 0.0))))


def _merge_call(x, mod, att1, att2, scan1, scan2, mo, rg, bg, wb, wo, gn, router, n1, s2):
    n, d = x.shape
    tm = TOK_TILE
    n1t = n1 // tm
    t2 = s2 // tm
    moe = router is not None
    n_slabs = att1.shape[0]

    def mod_row(t):
        return jnp.where(t < n1t, 0, 1 + (t - n1t) // t2)

    g1 = lambda t: jnp.minimum(t, n1t - 1)
    g2 = lambda t: jnp.maximum(t - n1t, 0)
    row = lambda w_: pl.BlockSpec((tm, w_), lambda t: (t, 0))
    row1 = lambda w_: pl.BlockSpec((tm, w_), lambda t: (g1(t), 0))
    row2 = lambda w_: pl.BlockSpec((tm, w_), lambda t: (g2(t), 0))
    const = lambda shp: pl.BlockSpec(shp, lambda t: (0,) * len(shp))
    in_specs = [row(d), pl.BlockSpec((None, 6, d), lambda t: (mod_row(t), 0, 0)),
                pl.BlockSpec((n_slabs, tm, LANES), lambda t: (0, g1(t), 0)),
                pl.BlockSpec((n_slabs, tm, LANES), lambda t: (0, g2(t), 0)),
                row1(ML_W), row1(ML_W), row1(RET_W), row1(RET_W),
                row2(ML_W), row2(ML_W), row2(RET_W), row2(RET_W),
                row(ML_W), row(RET_W), row(3 * d),
                const((d, d)), const((d, d)), const((1, d))]
    args = [x, mod, att1, att2, *scan1, *scan2, mo, rg, bg, wb, wo, gn]
    out_shape = [jax.ShapeDtypeStruct((n, d), F32), jax.ShapeDtypeStruct((n, d), F32 if moe else BF16)]
    out_specs = [row(d), row(d)]
    if moe:
        in_specs.append(const((d, LANES)))
        args.append(router)
        out_shape.append(jax.ShapeDtypeStruct((n, LANES), F32))
        out_specs.append(row(LANES))
    return pl.pallas_call(
        functools.partial(_merge_kernel, moe=moe, n1_tiles=n1t),
        out_shape=out_shape,
        grid=(n // tm,),
        in_specs=in_specs,
        out_specs=out_specs,
        compiler_params=_cparams(("arbitrary",)),
        name="merge_out",
    )(*args)


def _final_norm(x, g):
    ms = jnp.mean(x * x, axis=-1, keepdims=True)
    return x * lax.rsqrt(ms + EPS) * g


def _ffn_kernel(h_ref, x1_ref, mod_ref, wg_ref, wu_ref, wd_ref, fn_ref, o_ref, *, n_chunks, final):
    h = h_ref[...]
    f = wg_ref.shape[1]
    fc = f // n_chunks
    acc = jnp.zeros(o_ref.shape, F32)
    for c in range(n_chunks):
        sl = slice(c * fc, (c + 1) * fc)
        g = jnp.dot(h, wg_ref[:, sl], preferred_element_type=F32)
        u = jnp.dot(h, wu_ref[:, sl], preferred_element_type=F32)
        a = (g * _sigmoid(g) * u).astype(BF16)
        acc += jnp.dot(a, wd_ref[sl, :], preferred_element_type=F32)
    x2 = x1_ref[...] + mod_ref[5:6, :] * acc
    o_ref[...] = _final_norm(x2, fn_ref[...]) if final else x2


def _ffn_call(h2, x1, mod, wg, wu, wd, fn, n1, s2, final):
    n, d = x1.shape
    f = wg.shape[1]
    tm = FFN_TILE

    def mod_row(t):
        return jnp.where(t * tm < n1, 0, 1 + (t * tm - n1) // s2)

    row = lambda dt: pl.BlockSpec((tm, d), lambda t: (t, 0))
    const = lambda shp: pl.BlockSpec(shp, lambda t: (0,) * len(shp), pipeline_mode=pl.Buffered(1))
    return pl.pallas_call(
        functools.partial(_ffn_kernel, n_chunks=2, final=final),
        out_shape=jax.ShapeDtypeStruct((n, d), F32),
        grid=(n // tm,),
        in_specs=[row(BF16), row(F32), pl.BlockSpec((None, 6, d), lambda t: (mod_row(t), 0, 0)),
                  const((d, f)), const((d, f)), const((f, d)), pl.BlockSpec((1, d), lambda t: (0, 0))],
        out_specs=row(F32),
        compiler_params=_cparams(("arbitrary",)),
        name="ffn_dense",
    )(h2, x1, mod, wg, wu, wd, fn)


def _route_tables(route, n):
    tmx = MOE_TILE
    e = jnp.concatenate([route[:, 0], route[:, 1]]).astype(jnp.int32)
    oh = (e[:, None] == jnp.arange(N_EXPERTS, dtype=jnp.int32)[None, :]).astype(jnp.int32)
    cs = jnp.cumsum(oh, axis=0)
    counts = cs[-1]
    rank = jnp.sum(oh * cs, axis=1) - 1
    padded = ((counts + tmx - 1) // tmx) * tmx
    pend = jnp.cumsum(padded)
    pos = jnp.sum(oh * (pend - padded)[None, :], axis=1) + rank
    n_rows = 2 * n + N_EXPERTS * tmx
    n_tiles = n_rows // tmx
    a = jnp.arange(2 * n, dtype=jnp.int32)
    src = jnp.zeros((n_rows + 2 * tmx,), jnp.int32).at[pos].set(a % n)
    dump = 2 * n + jnp.arange(n_rows, dtype=jnp.int32) % (2 * tmx)
    dst = jnp.concatenate([2 * n + tmx + jnp.arange(tmx, dtype=jnp.int32), dump.at[pos].set(a)])
    tile_e = jnp.sum((jnp.arange(n_tiles + 1, dtype=jnp.int32) * tmx)[:, None] >= pend[None, :], axis=1)
    tile_e = jnp.minimum(tile_e, N_EXPERTS - 1).astype(jnp.int32)
    return src.reshape(n_tiles + 2, 1, tmx), dst.reshape(n_tiles + 1, 1, tmx), tile_e


def _moe_routed_kernel(te_ref, src_ref, srcn_ref, dst_ref, h_hbm, wg_ref, wu_ref, wd_ref, yy_hbm,
                       xbuf, ybuf, sem_in, sem_out):
    del te_ref
    tmx = MOE_TILE
    t = pl.program_id(0)
    last = pl.num_programs(0) - 1
    slot = t % 2
    other = 1 - slot

    def gather(idx_ref, s):
        for r in range(tmx):
            pltpu.make_async_copy(h_hbm.at[pl.ds(idx_ref[0, r], 1), :], xbuf.at[s, pl.ds(r, 1), :],
                                  sem_in.at[s]).start()

    def wait_gather(s):
        pltpu.make_async_copy(h_hbm.at[pl.ds(0, tmx), :], xbuf.at[s], sem_in.at[s]).wait()

    def wait_scatter(s):
        pltpu.make_async_copy(ybuf.at[s], yy_hbm.at[pl.ds(0, tmx), :], sem_out.at[s]).wait()

    @pl.when(t == 0)
    def _prologue():
        ybuf[...] = jnp.zeros_like(ybuf)
        gather(src_ref, 0)

    @pl.when(t >= 1)
    def _free_ybuf():
        wait_scatter(slot)

    wait_gather(slot)
    gather(srcn_ref, other)
    for r in range(tmx):
        pltpu.make_async_copy(ybuf.at[other, pl.ds(r, 1), :], yy_hbm.at[pl.ds(dst_ref[0, r], 1), :],
                              sem_out.at[other]).start()
    x = xbuf[slot].astype(BF16)
    g = jnp.dot(x, wg_ref[...], preferred_element_type=F32)
    u = jnp.dot(x, wu_ref[...], preferred_element_type=F32)
    a = (g * _sigmoid(g) * u).astype(BF16)
    ybuf[slot] = jnp.dot(a, wd_ref[...], preferred_element_type=F32)

    @pl.when(t == last)
    def _drain():
        wait_gather(other)
        wait_scatter(other)


def _moe_routed_call(h2, src, dst, tile_e, wg, wu, wd):
    n, d = h2.shape
    ne, _, f = wg.shape
    tmx = MOE_TILE
    n_tiles = tile_e.shape[0] - 1
    smem_blk = lambda fn_: pl.BlockSpec((None, 1, tmx), fn_, memory_space=pltpu.SMEM)
    grid_spec = pltpu.PrefetchScalarGridSpec(
        num_scalar_prefetch=1,
        grid=(n_tiles + 1,),
        in_specs=[
            smem_blk(lambda t, te: (t, 0, 0)),
            smem_blk(lambda t, te: (t + 1, 0, 0)),
            smem_blk(lambda t, te: (t, 0, 0)),
            pl.BlockSpec(memory_space=pl.ANY),
            pl.BlockSpec((None, d, f), lambda t, te: (te[t], 0, 0)),
            pl.BlockSpec((None, d, f), lambda t, te: (te[t], 0, 0)),
            pl.BlockSpec((None, f, d), lambda t, te: (te[t], 0, 0)),
        ],
        out_specs=pl.BlockSpec(memory_space=pl.ANY),
        scratch_shapes=[pltpu.VMEM((2, tmx, d), F32), pltpu.VMEM((2, tmx, d), F32),
                        pltpu.SemaphoreType.DMA((2,)), pltpu.SemaphoreType.DMA((2,))],
    )
    return pl.pallas_call(
        _moe_routed_kernel,
        out_shape=jax.ShapeDtypeStruct((2 * n + 2 * tmx, d), F32),
        grid_spec=grid_spec,
        compiler_params=_cparams(("arbitrary",)),
        name="moe_experts",
    )(tile_e, src, src, dst, h2, wg, wu, wd)


def _moe_combine_kernel(x1_ref, mod_ref, route_ref, y1_ref, y2_ref, fn_ref, o_ref, *, final):
    w1 = route_ref[:, 2:3]
    w2 = route_ref[:, 3:4]
    x2 = x1_ref[...] + mod_ref[5:6, :] * (w1 * y1_ref[...] + w2 * y2_ref[...])
    o_ref[...] = _final_norm(x2, fn_ref[...]) if final else x2


def _moe_combine_call(x1, mod, route, yy, fn, n1, s2, final):
    n, d = x1.shape
    tm = FFN_TILE
    nt = n // tm

    def mod_row(t):
        return jnp.where(t * tm < n1, 0, 1 + (t * tm - n1) // s2)

    row = lambda w_: pl.BlockSpec((tm, w_), lambda t: (t, 0))
    return pl.pallas_call(
        functools.partial(_moe_combine_kernel, final=final),
        out_shape=jax.ShapeDtypeStruct((n, d), F32),
        grid=(nt,),
        in_specs=[row(d), pl.BlockSpec((None, 6, d), lambda t: (mod_row(t), 0, 0)), row(LANES),
                  row(d), pl.BlockSpec((tm, d), lambda t: (nt + t, 0)), pl.BlockSpec((1, d), lambda t: (0, 0))],
        out_specs=row(d),
        compiler_params=_cparams(("arbitrary",)),
        name="moe_combine",
    )(x1, mod, route, yy, yy, fn)


def _pair_q_heads(a, axis):
    g = N_HEADS // N_KV_HEADS
    shp = a.shape
    a = a.reshape(shp[:axis] + (N_KV_HEADS, g, HEAD_DIM) + shp[axis + 1:])
    a = jnp.swapaxes(a, axis, axis + 1)
    return a.reshape(shp)


def _prep_in_weights(w_in, b_in):
    g0 = ATT_Q_W + 2 * ATT_KV_W + 4 * ML_W
    pad = N_IN_PAD - w_in.shape[-1]

    def reorder(a):
        parts = [_pair_q_heads(a[..., :ATT_Q_W], a.ndim - 1), a[..., ATT_Q_W:g0], a[..., g0 + N_GATES:],
                 a[..., g0:g0 + N_GATES], jnp.zeros(a.shape[:-1] + (pad,), a.dtype)]
        return jnp.concatenate(parts, axis=-1)

    return reorder(w_in).astype(BF16), reorder(b_in)[:, None, :]


def _rope_tables(s2, tm):
    pos = jnp.arange(s2)
    rowp = (pos // GRID_W).astype(F32)
    colp = (pos % GRID_W).astype(F32)
    quarter = HEAD_DIM // 4
    inv = ROPE_BASE ** (-jnp.arange(quarter, dtype=F32) / quarter)
    ang_r = rowp[:, None] * inv
    ang_c = colp[:, None] * inv
    cos_h = jnp.concatenate([jnp.cos(ang_r), jnp.cos(ang_r), jnp.cos(ang_c), jnp.cos(ang_c)], axis=1)
    sin_h = jnp.concatenate([-jnp.sin(ang_r), jnp.sin(ang_r), -jnp.sin(ang_c), jnp.sin(ang_c)], axis=1)
    cos_t = jnp.concatenate([cos_h, cos_h], axis=1)
    sin_t = jnp.concatenate([sin_h, sin_h], axis=1)
    cos_t = jnp.concatenate([jnp.ones((tm, LANES), F32), cos_t], axis=0)
    sin_t = jnp.concatenate([jnp.zeros((tm, LANES), F32), sin_t], axis=0)
    return cos_t, sin_t


def _pair_blockdiag(m):
    b, nd, h, d, _ = m.shape
    m = m.reshape(b, nd, h // 2, 2, d, d)
    z = jnp.zeros_like(m[:, :, :, 0])
    top = jnp.concatenate([m[:, :, :, 0], z], axis=-1)
    bot = jnp.concatenate([z, m[:, :, :, 1]], axis=-1)
    return jnp.concatenate([top, bot], axis=-2)


def _pair_unblock(m):
    b, nd, p, _, _ = m.shape
    a = m[:, :, :, :HALF, :HALF]
    c = m[:, :, :, HALF:, HALF:]
    return jnp.stack([a, c], axis=3).reshape(b, nd, 2 * p, HALF, HALF)


def kernel(x_prompt, x_sample, c, cache_attn_k, cache_attn_v, state_mlstm_c, state_mlstm_n, state_mlstm_m,
           state_ret_s, c_ctx, w_ada, b_ada, norm_mix, norm_ffn, w_in, b_in, q_norm, k_norm, ret_decay,
           w_branch, w_out, ffn_w_gate, ffn_w_up, ffn_w_down, moe_router, moe_w_gate, moe_w_up, moe_w_down,
           final_norm):
    b1, s1, d = x_prompt.shape
    b2, s2, _ = x_sample.shape
    depth = w_in.shape[0]
    n1, n2 = b1 * s1, b2 * s2
    assert s1 % TOK_TILE == 0 and s2 % FFN_TILE == 0 and n1 % FFN_TILE == 0 and b2 + 1 <= 8

    x = jnp.concatenate([x_prompt.reshape(n1, d), x_sample.reshape(n2, d)], axis=0)
    c8 = jnp.concatenate([c_ctx[None, :], c, jnp.zeros((8 - 1 - b2, d), F32)], axis=0)
    mod_all = _ada_call(c8, w_ada, b_ada).reshape(depth, 8, 6, d)

    w_in_p, b_in_p = _prep_in_weights(w_in, b_in)
    cos_t, sin_t = _rope_tables(s2, TOK_TILE)
    wb = jnp.concatenate([_pair_q_heads(w_branch[:, :ATT_Q_W], 1), w_branch[:, ATT_Q_W:]], axis=1).astype(BF16)
    wo = w_out.astype(BF16)
    fn = final_norm[None, :]

    zeros_like_state = lambda shp: jnp.zeros((b1,) + shp, F32)
    states = []
    for l in range(depth):
        mod = mod_all[l]
        qg = jnp.tile(q_norm[l], 2)[None, :]
        kg = jnp.tile(k_norm[l], 2)[None, :]
        q, k, v, kb, vb, mqkv, rqkv, mo, rg, bg, gcol, grow = _inproj_call(
            x, mod, norm_mix[l][None, :], w_in_p[l], b_in_p[l], qg, kg, cos_t, sin_t, n1, s2)

        kt1 = jnp.swapaxes(kb[:n1].reshape(b1, s1, ATT_KV_W), 1, 2)
        att1 = _attn_call(q, kt1, vb[:n1].reshape(b1, s1, ATT_KV_W), 0, s1)
        k2 = jnp.concatenate([cache_attn_k[:, l].reshape(b2, -1, ATT_KV_W).astype(BF16),
                              kb[n1:].reshape(b2, s2, ATT_KV_W)], axis=1)
        v2 = jnp.concatenate([cache_attn_v[:, l].reshape(b2, -1, ATT_KV_W).astype(BF16),
                              vb[n1:].reshape(b2, s2, ATT_KV_W)], axis=1)
        att2 = _attn_call(q, jnp.swapaxes(k2, 1, 2), v2, n1, s2)

        rd = jnp.broadcast_to(ret_decay[l].reshape(2 * RET_HEADS, 1), (2 * RET_HEADS, LANES))
        r1 = _scan_call(mqkv, rqkv, gcol, grow, rd,
                        zeros_like_state((2, 2, LANES, LANES)), zeros_like_state((2, 2, 1, LANES)),
                        zeros_like_state((2, ML_HEADS, 1, LANES)), zeros_like_state((2, 2, LANES, LANES)), 0, s1)
        c0 = _pair_blockdiag(state_mlstm_c[:, l])
        n0 = state_mlstm_n[:, l].reshape(b2, 2, 2, 1, LANES)
        m0 = jnp.broadcast_to(state_mlstm_m[:, l][..., None, None], (b2, 2, ML_HEADS, 1, LANES))
        s0 = _pair_blockdiag(state_ret_s[:, l])
        r2 = _scan_call(mqkv, rqkv, gcol, grow, rd, c0, n0, m0, s0, n1, s2)
        states.append((k[:n1].reshape(b1, s1, N_KV_HEADS, HEAD_DIM), v[:n1].reshape(b1, s1, N_KV_HEADS, HEAD_DIM),
                       _pair_unblock(r1[4]), r1[5].reshape(b1, 2, ML_HEADS, HEAD_DIM), r1[6][..., 0, 0],
                       _pair_unblock(r1[7])))

        moe = l % 2 == 1
        jj = l // 2
        router = jnp.pad(moe_router[jj], ((0, 0), (0, LANES - N_EXPERTS))) if moe else None
        outs = _merge_call(x, mod, att1, att2, r1[:4], r2[:4], mo, rg, bg, wb[l], wo[l], norm_ffn[l][None, :],
                           router, n1, s2)
        final = l == depth - 1
        if moe:
            x1, h2, route = outs
            src, dst, tile_e = _route_tables(route, n1 + n2)
            yy = _moe_routed_call(h2, src, dst, tile_e, moe_w_gate[jj].astype(BF16), moe_w_up[jj].astype(BF16),
                                  moe_w_down[jj].astype(BF16))
            x = _moe_combine_call(x1, mod, route, yy, fn, n1, s2, final)
        else:
            x1, h2 = outs
            x = _ffn_call(h2, x1, mod, ffn_w_gate[jj].astype(BF16), ffn_w_up[jj].astype(BF16),
                          ffn_w_down[jj].astype(BF16), fn, n1, s2, final)

    y_prompt = x[:n1].reshape(b1, s1, d)
    y_sample = x[n1:].reshape(b2, s2, d)
    stack = lambda i: jnp.stack([s[i] for s in states], axis=1)
    return (y_prompt, y_sample, stack(0), stack(1), stack(2), stack(3), stack(4), stack(5))
```

```python
import functools

import jax
import jax.numpy as jnp
from jax import lax
from jax.experimental import pallas as pl
from jax.experimental.pallas import tpu as pltpu

F32 = jnp.float32
BF16 = jnp.bfloat16

N_HEADS = 8
N_KV_HEADS = 2
HEAD_DIM = 64
ML_HEADS = 4
RET_HEADS = 4
GRID_W = 64
CHUNK = 128
ROPE_BASE = 10000.0
EPS = 1e-6
N_EXPERTS = 8
LANES = 128
HALF = 64

ATT_Q_W = N_HEADS * HEAD_DIM
ATT_KV_W = N_KV_HEADS * HEAD_DIM
ML_W = ML_HEADS * HEAD_DIM
RET_W = RET_HEADS * HEAD_DIM
N_GATES = 4 * ML_HEADS

TOK_TILE = 256
SCAN_SEQS = 2
FFN_TILE = 512
MOE_TILE = 512
ATT_Q_TILE = 256
ATT_K_TILE = 512
LOG2E = 1.4426950408889634
Q_SCALE = 0.125 * LOG2E
VMEM_LIMIT = 56 * 1024 * 1024

O_AQ, O_AK, O_AV = 0, 512, 640
O_MQ, O_MK, O_MV, O_MO = 768, 1024, 1280, 1536
O_RQ, O_RK, O_RV, O_RG = 1792, 2048, 2304, 2560
O_BG = 2816
O_MG = 5888
N_IN_PAD = 6016


def _cparams(sem, vmem=VMEM_LIMIT):
    return pltpu.CompilerParams(dimension_semantics=sem, vmem_limit_bytes=vmem)


def _lane_iota(shape=(1, LANES)):
    return lax.broadcasted_iota(jnp.int32, shape, len(shape) - 1)


def _head_mean_matrix():
    r = lax.broadcasted_iota(jnp.int32, (LANES, LANES), 0) >> 6
    c = lax.broadcasted_iota(jnp.int32, (LANES, LANES), 1) >> 6
    return jnp.where(r == c, 1.0 / HALF, 0.0).astype(BF16)


def _head_rms(xs, mean_mat):
    t = xs[0].shape[0]
    sq = jnp.concatenate([x * x for x in xs], axis=0)
    hi = sq.astype(BF16)
    lo = (sq - hi.astype(F32)).astype(BF16)
    ms = jnp.dot(jnp.concatenate([hi, lo], axis=0), mean_mat, preferred_element_type=F32)
    ms = ms[:len(xs) * t] + ms[len(xs) * t:]
    return [x * lax.rsqrt(ms[i * t:(i + 1) * t] + EPS) for i, x in enumerate(xs)]


def _rope(y, cos, sin, lane):
    up = pltpu.roll(y, LANES - 16, 1)
    dn = pltpu.roll(y, 16, 1)
    partner = jnp.where((lane & 31) < 16, up, dn)
    return y * cos + partner * sin


def _log_sigmoid(x):
    return jnp.minimum(x, 0.0) - jnp.log1p(jnp.exp(-jnp.abs(x)))


def _sigmoid(x):
    return 1.0 / (1.0 + jnp.exp(-x))


def _store_token_tiles(ref, x):
    for j in range(x.shape[1] // LANES):
        ref[:, j, :] = x[:, j * LANES:(j + 1) * LANES]


def _load_token_tiles(ref):
    return jnp.concatenate([ref[:, j, :] for j in range(ref.shape[1])], axis=1)


def _split3(x):
    h = x.astype(BF16)
    r = x - h.astype(F32)
    m = r.astype(BF16)
    l = (r - m.astype(F32)).astype(BF16)
    return h, m, l


def _ada_kernel(c_ref, w_ref, b_ref, o_ref):
    c = c_ref[...]
    a = (c * _sigmoid(c)).astype(BF16)
    o_ref[...] = jnp.dot(a, w_ref[...].astype(BF16), preferred_element_type=F32) + b_ref[...]


def _ada_call(c8, w_ada, b_ada):
    depth, d, n6 = w_ada.shape
    tn = 1536
    return pl.pallas_call(
        _ada_kernel,
        out_shape=jax.ShapeDtypeStruct((depth, 8, n6), F32),
        grid=(depth, n6 // tn),
        in_specs=[
            pl.BlockSpec((8, d), lambda l, j: (0, 0)),
            pl.BlockSpec((None, d, tn), lambda l, j: (l, 0, j)),
            pl.BlockSpec((None, 1, tn), lambda l, j: (l, 0, j)),
        ],
        out_specs=pl.BlockSpec((None, 8, tn), lambda l, j: (l, 0, j)),
        compiler_params=_cparams(("arbitrary", "arbitrary")),
        name="ada_mod",
    )(c8, w_ada, b_ada.reshape(depth, 1, n6))


def _inproj_kernel(x_ref, mod_ref, gn_ref, w_ref, b_ref, qg_ref, kg_ref, cos_ref, sin_ref,
                   q_ref, k_ref, v_ref, kb_ref, vb_ref, mqkv_ref, rqkv_ref, mo_ref, rg_ref, bg_ref, gcol_ref,
                   grow_ref):
    x = x_ref[...]
    ms = jnp.mean(x * x, axis=-1, keepdims=True)
    h = x * lax.rsqrt(ms + EPS) * gn_ref[...]
    h = h * (1.0 + mod_ref[1:2, :]) + mod_ref[0:1, :]
    hb = h.astype(BF16)

    def seg(a, b):
        return jnp.dot(hb, w_ref[:, a:b], preferred_element_type=F32) + b_ref[:, a:b]

    lane = _lane_iota()
    mean_mat = _head_mean_matrix()
    cos = cos_ref[...]
    sin = sin_ref[...]
    n_q = ATT_Q_W // LANES
    za = seg(O_AQ, O_MQ)
    normed = _head_rms([za[:, s * LANES:(s + 1) * LANES] for s in range(n_q + 1)], mean_mat)
    for s in range(n_q):
        q_ref[s] = (_rope(normed[s] * qg_ref[...], cos, sin, lane) * Q_SCALE).astype(BF16)
    y = _rope(normed[n_q] * kg_ref[...], cos, sin, lane)
    k_ref[...] = y
    kb_ref[...] = y.astype(BF16)
    y = za[:, O_AV:O_AV + LANES]
    v_ref[...] = y
    vb_ref[...] = y.astype(BF16)

    zm = seg(O_MQ, O_RQ)
    mqkv_ref[:, 0:ML_W] = zm[:, 0:ML_W].astype(BF16)
    mqkv_ref[:, ML_W:2 * ML_W] = (zm[:, ML_W:2 * ML_W] * 0.125).astype(BF16)
    mqkv_ref[:, 2 * ML_W:3 * ML_W] = zm[:, 2 * ML_W:3 * ML_W].astype(BF16)
    mo_ref[...] = zm[:, 3 * ML_W:]

    zr = seg(O_RQ, O_BG)
    for s in range(RET_W // LANES):
        sl = slice(s * LANES, (s + 1) * LANES)
        rqkv_ref[:, sl] = _rope(zr[:, sl], cos, sin, lane).astype(BF16)
        slk = slice(RET_W + s * LANES, RET_W + (s + 1) * LANES)
        rqkv_ref[:, slk] = (_rope(zr[:, slk], cos, sin, lane) * 0.125).astype(BF16)
    rqkv_ref[:, 2 * RET_W:3 * RET_W] = zr[:, 2 * RET_W:3 * RET_W].astype(BF16)
    rg_ref[...] = zr[:, 3 * RET_W:]

    zg = seg(O_BG, N_IN_PAD)
    bg_ref[...] = zg[:, :O_MG - O_BG]

    g = zg[:, O_MG - O_BG:]
    is_f = ((lane >> 2) & 1) == 1
    g = jnp.where(is_f, _log_sigmoid(g), g)
    gcol_ref[...] = g
    grow_ref[...] = g.T[0:N_GATES, :]


def _inproj_call(x, mod, gn, w, b, qg, kg, cos_t, sin_t, n1, s2):
    n, d = x.shape
    tm = TOK_TILE
    n1t = n1 // tm
    t2 = s2 // tm

    def mod_row(t):
        return jnp.where(t < n1t, 0, 1 + (t - n1t) // t2)

    def tab_row(t):
        return jnp.where(t < n1t, 0, 1 + (t - n1t) % t2)

    row = lambda w_: pl.BlockSpec((tm, w_), lambda t: (t, 0))
    const = lambda shp: pl.BlockSpec(shp, lambda t: (0,) * len(shp))
    n_slabs = ATT_Q_W // LANES
    out_shape = [
        jax.ShapeDtypeStruct((n_slabs, n, LANES), BF16),
        jax.ShapeDtypeStruct((n, ATT_KV_W), F32),
        jax.ShapeDtypeStruct((n, ATT_KV_W), F32),
        jax.ShapeDtypeStruct((n, ATT_KV_W), BF16),
        jax.ShapeDtypeStruct((n, ATT_KV_W), BF16),
        jax.ShapeDtypeStruct((n, 3 * ML_W), BF16),
        jax.ShapeDtypeStruct((n, 3 * RET_W), BF16),
        jax.ShapeDtypeStruct((n, ML_W), F32),
        jax.ShapeDtypeStruct((n, RET_W), F32),
        jax.ShapeDtypeStruct((n, 3 * d), F32),
        jax.ShapeDtypeStruct((n, LANES), F32),
        jax.ShapeDtypeStruct((N_GATES, n), F32),
    ]
    out_specs = [pl.BlockSpec((n_slabs, tm, LANES), lambda t: (0, t, 0)), row(ATT_KV_W), row(ATT_KV_W),
                 row(ATT_KV_W), row(ATT_KV_W), row(3 * ML_W), row(3 * RET_W), row(ML_W),
                 row(RET_W), row(3 * d), row(LANES), pl.BlockSpec((N_GATES, tm), lambda t: (0, t))]
    return pl.pallas_call(
        _inproj_kernel,
        out_shape=out_shape,
        grid=(n // tm,),
        in_specs=[
            row(d),
            pl.BlockSpec((None, 6, d), lambda t: (mod_row(t), 0, 0)),
            const((1, d)),
            const((d, N_IN_PAD)),
            const((1, N_IN_PAD)),
            const((1, LANES)),
            const((1, LANES)),
            pl.BlockSpec((tm, LANES), lambda t: (tab_row(t), 0)),
            pl.BlockSpec((tm, LANES), lambda t: (tab_row(t), 0)),
        ],
        out_specs=out_specs,
        compiler_params=_cparams(("arbitrary",)),
        name="in_proj",
    )(x, mod, gn, w, b, qg, kg, cos_t, sin_t)


def _attn_kernel(q_ref, kt_ref, v_ref, o_ref, s_sc, *, tk):
    tq = q_ref.shape[1]
    nk = kt_ref.shape[1] // tk
    lo = _lane_iota() < HALF

    def slab(j, carry):
        qs = q_ref[j]
        outs = []
        for half in range(2):
            qm = jnp.where(lo if half == 0 else jnp.logical_not(lo), qs, jnp.zeros_like(qs))
            mx = None
            for c in range(nk):
                s = jnp.dot(qm, kt_ref[:, c * tk:(c + 1) * tk], preferred_element_type=F32)
                s_sc[half, :, c * tk:(c + 1) * tk] = s
                cm = jnp.max(s, axis=-1, keepdims=True)
                mx = cm if mx is None else jnp.maximum(mx, cm)
            l = jnp.zeros((tq, 1), F32)
            acc = jnp.zeros((tq, LANES), F32)
            for c in range(nk):
                p = jnp.exp2(s_sc[half, :, c * tk:(c + 1) * tk] - mx)
                l += jnp.sum(p, axis=-1, keepdims=True)
                acc += jnp.dot(p.astype(BF16), v_ref[c * tk:(c + 1) * tk, :], preferred_element_type=F32)
            outs.append(acc / l)
        o_ref[j] = jnp.where(lo, outs[0], outs[1]).astype(o_ref.dtype)
        return carry

    lax.fori_loop(0, q_ref.shape[0], slab, 0)


def _attn_call(q, kt, v, tok_off, sq):
    n_slabs = q.shape[0]
    b, sk, _ = v.shape
    tq = min(ATT_Q_TILE, sq)
    tk = min(ATT_K_TILE, sk)
    nq = sq // tq
    off = tok_off // tq
    return pl.pallas_call(
        functools.partial(_attn_kernel, tk=tk),
        out_shape=jax.ShapeDtypeStruct((n_slabs, b * sq, LANES), BF16),
        grid=(b, nq),
        in_specs=[
            pl.BlockSpec((n_slabs, tq, LANES), lambda i, j: (0, off + i * nq + j, 0)),
            pl.BlockSpec((None, ATT_KV_W, sk), lambda i, j: (i, 0, 0)),
            pl.BlockSpec((None, sk, ATT_KV_W), lambda i, j: (i, 0, 0)),
        ],
        out_specs=pl.BlockSpec((n_slabs, tq, LANES), lambda i, j: (0, i * nq + j, 0)),
        scratch_shapes=[pltpu.VMEM((2, tq, sk), F32)],
        compiler_params=_cparams(("arbitrary", "arbitrary")),
        name="attention",
    )(q, kt, v)


def _scan_kernel(*refs, nb):
    L = CHUNK
    seq_in = refs[:8 * nb]
    rd_ref, c0_ref, n0_ref, m0_ref, s0_ref = refs[8 * nb:8 * nb + 5]
    o0 = 8 * nb + 5
    hf_ref, hb_ref, of_ref, ob_ref, cout_ref, nout_ref, mout_ref, sout_ref = refs[o0:o0 + 8]
    c_all, n_all, m_all, s_all, dec_st, qdec_st, kdec_st, cdec_st = refs[o0 + 8:]
    b_idx = pl.program_id(0)
    j = pl.program_id(1)
    nc = pl.num_programs(1)
    lane = _lane_iota()
    lo = lane < HALF
    row_i = lax.broadcasted_iota(jnp.int32, (L, L), 0)
    col_i = lax.broadcasted_iota(jnp.int32, (L, L), 1)
    blockmask = (row_i >> 6) == (col_i >> 6)
    causal = (row_i >= col_i, row_i <= col_i)

    @pl.when(jnp.logical_and(b_idx == 0, j == 0))
    def _init_tables():
        lg = _log_sigmoid(rd_ref[...])
        pos = lax.broadcasted_iota(jnp.int32, (L, 1), 0).astype(F32)
        diff = (row_i - col_i).astype(F32)
        for d in range(2):
            sd = diff if d == 0 else -diff
            for p in range(2):
                qd, kd, cd = [], [], []
                for e in range(2):
                    r = d * RET_HEADS + 2 * p + e
                    g = lg[r:r + 1, 0:1]
                    dec_st[d, p, e * L:(e + 1) * L, :] = jnp.where(sd >= 0, jnp.exp(g * jnp.maximum(sd, 0.0)), 0.0)
                    if d == 0:
                        qd.append(jnp.exp(g * (pos + 1.0)))
                        kd.append(jnp.exp(g * (L - 1.0 - pos)))
                    else:
                        qd.append(jnp.exp(g * (L - pos)))
                        kd.append(jnp.exp(g * pos))
                    cd.append(jnp.exp(g * float(L)))
                qdec_st[d, p] = jnp.where(lo, qd[0], qd[1])
                kdec_st[d, p] = jnp.where(lo, kd[0], kd[1])
                cdec_st[d, p] = jnp.where(lo, cd[0], cd[1])

    @pl.when(j == 0)
    def _load_state():
        c_all[...] = c0_ref[...]
        n_all[...] = n0_ref[...]
        m_all[...] = m0_ref[...]
        s_all[...] = s0_ref[...]

    tri = (row_i >= col_i).astype(BF16)
    triu = (row_i <= col_i).astype(BF16)

    def cumsums(gc_ref, gr_ref, d):
        a_col, a_row = (tri, triu) if d == 0 else (triu, tri)
        col3 = jnp.dot(a_col, jnp.concatenate(_split3(gc_ref[...]), axis=1), preferred_element_type=F32)
        row3 = jnp.dot(jnp.concatenate(_split3(gr_ref[...]), axis=0), a_row, preferred_element_type=F32)
        col = col3[:, 0:LANES] + col3[:, LANES:2 * LANES] + col3[:, 2 * LANES:]
        rowv = row3[0:N_GATES] + row3[N_GATES:2 * N_GATES] + row3[2 * N_GATES:]
        return col, rowv

    pair = lambda x0, x1: jnp.where(lo, x0, x1)

    keys = [(bb, d, p) for bb in range(nb) for d in range(2) for p in range(2)]
    c_old = {k: c_all[k] for k in keys}
    n_old = {k: n_all[k] for k in keys}
    m_old = {k: m_all[k] for k in keys}
    s_old = {k: s_all[k] for k in keys}
    c_new, n_new, m_new_st, s_new = {}, {}, {}, {}

    for bb in range(nb):
        mf_ref, mb_ref, rf_ref, rb_ref, gcf_ref, gcb_ref, grf_ref, grb_ref = seq_in[8 * bb:8 * bb + 8]
        for d, (m_ref, r_ref, gc_ref, gr_ref, h_out, o_out) in enumerate(
                ((mf_ref, rf_ref, gcf_ref, grf_ref, hf_ref, of_ref),
                 (mb_ref, rb_ref, gcb_ref, grb_ref, hb_ref, ob_ref))):
            gcol = gc_ref[...]
            grow = gr_ref[...]
            cum_col, cum_row = cumsums(gc_ref, gr_ref, d)
            gi = 2 * ML_HEADS * d
            gf = gi + ML_HEADS
            last = L - 1 if d == 0 else 0
            for p in range(2):
                sl = slice(p * LANES, (p + 1) * LANES)
                q2 = m_ref[:, p * LANES:(p + 1) * LANES]
                k2 = m_ref[:, ML_W + p * LANES:ML_W + (p + 1) * LANES]
                v2 = m_ref[:, 2 * ML_W + p * LANES:2 * ML_W + (p + 1) * LANES]
                c2 = c_old[bb, d, p]
                n2 = n_old[bb, d, p]
                m2 = m_old[bb, d, p]
                q_c = jnp.dot(q2, c2.astype(BF16), preferred_element_type=F32)
                q_n = q2.astype(F32) * n2
                hds = (2 * p, 2 * p + 1)
                i_cols = [gcol[:, gi + hd:gi + hd + 1] for hd in hds]
                b_cols = [cum_col[:, gf + hd:gf + hd + 1] for hd in hds]
                b2 = pair(*b_cols)
                i2 = pair(*i_cols)
                a2 = b2 + m2
                dms = []
                for e, hd in enumerate(hds):
                    i_row = grow[gi + hd:gi + hd + 1, :]
                    b_row = cum_row[gf + hd:gf + hd + 1, :]
                    dms.append(jnp.where(causal[d], b_cols[e] - b_row + i_row, -jnp.inf))
                mt2 = jnp.maximum(a2, pair(*[jnp.max(dm, axis=-1, keepdims=True) for dm in dms]))
                wa2 = jnp.exp(a2 - mt2)
                qs = jnp.concatenate([jnp.where(lo, q2, jnp.zeros_like(q2)),
                                      jnp.where(lo, jnp.zeros_like(q2), q2)], axis=0)
                wd = jnp.concatenate([jnp.exp(dms[e] - mt2[:, e * HALF:e * HALF + 1]) for e in range(2)], axis=0)
                s = lax.dot_general(qs, k2, (((1,), (1,)), ((), ())), preferred_element_type=F32) * wd
                sv = jnp.dot(s.astype(BF16), v2, preferred_element_type=F32)
                rs = jnp.sum(s, axis=-1, keepdims=True)
                qns = [jnp.sum(jnp.where(lo if e == 0 else jnp.logical_not(lo), q_n, 0.0), axis=-1, keepdims=True)
                       for e in range(2)]
                den2 = pair(rs[:L], rs[L:]) + wa2 * pair(*qns)
                dd2 = jnp.maximum(jnp.abs(den2), jnp.exp(-mt2))
                h_out[bb, :, sl] = (pair(sv[:L], sv[L:]) + wa2 * q_c) / dd2
                b_last = b2[last:last + 1, :]
                g2 = b_last - b2 + i2
                m_new = jnp.maximum(b_last + m2, jnp.max(g2, axis=0, keepdims=True))
                wc2 = jnp.exp(b_last + m2 - m_new)
                kw = k2.astype(F32) * jnp.exp(g2 - m_new)
                upd = lax.dot_general(kw.astype(BF16), v2, (((0,), (0,)), ((), ())), preferred_element_type=F32)
                c_new[bb, d, p] = jnp.where(blockmask, wc2 * c2 + upd, 0.0)
                n_new[bb, d, p] = wc2 * n2 + jnp.sum(kw, axis=0, keepdims=True)
                m_new_st[bb, d, p] = m_new
                q2 = r_ref[:, p * LANES:(p + 1) * LANES]
                k2 = r_ref[:, RET_W + p * LANES:RET_W + (p + 1) * LANES]
                v2 = r_ref[:, 2 * RET_W + p * LANES:2 * RET_W + (p + 1) * LANES]
                s2 = s_old[bb, d, p]
                q_s = jnp.dot(q2, s2.astype(BF16), preferred_element_type=F32) * qdec_st[d, p]
                qs = jnp.concatenate([jnp.where(lo, q2, jnp.zeros_like(q2)),
                                      jnp.where(lo, jnp.zeros_like(q2), q2)], axis=0)
                s = lax.dot_general(qs, k2, (((1,), (1,)), ((), ())), preferred_element_type=F32) * dec_st[d, p]
                sv = jnp.dot(s.astype(BF16), v2, preferred_element_type=F32)
                o_out[bb, :, sl] = pair(sv[:L], sv[L:]) + q_s
                kw = k2.astype(F32) * kdec_st[d, p]
                upd = lax.dot_general(kw.astype(BF16), v2, (((0,), (0,)), ((), ())), preferred_element_type=F32)
                s_new[bb, d, p] = jnp.where(blockmask, cdec_st[d, p] * s2 + upd, 0.0)

    for k in keys:
        c_all[k] = c_new[k]
        n_all[k] = n_new[k]
        m_all[k] = m_new_st[k]
        s_all[k] = s_new[k]

    @pl.when(j == nc - 1)
    def _store_state():
        cout_ref[...] = c_all[...]
        nout_ref[...] = n_all[...]
        mout_ref[...] = m_all[...]
        sout_ref[...] = s_all[...]


def _scan_call(mqkv, rqkv, gcol, grow, rd, c0, n0, m0, s0, tok_off, s):
    b = c0.shape[0]
    nb = SCAN_SEQS
    L = CHUNK
    nc = s // L
    off = tok_off // L
    fwd = lambda w_, bb: pl.BlockSpec((L, w_), lambda i, j: (off + (i * nb + bb) * nc + j, 0))
    bwd = lambda w_, bb: pl.BlockSpec((L, w_), lambda i, j: (off + (i * nb + bb) * nc + nc - 1 - j, 0))
    st = lambda shp: pl.BlockSpec((nb,) + shp, lambda i, j: (i,) + (0,) * len(shp))
    pair_mat = (2, 2, LANES, LANES)
    pair_vec = (2, 2, 1, LANES)
    seq_in_specs, seq_args = [], []
    for bb in range(nb):
        seq_in_specs += [
            fwd(3 * ML_W, bb), bwd(3 * ML_W, bb), fwd(3 * RET_W, bb), bwd(3 * RET_W, bb),
            fwd(LANES, bb), bwd(LANES, bb),
            pl.BlockSpec((N_GATES, L), lambda i, j, bb=bb: (0, off + (i * nb + bb) * nc + j)),
            pl.BlockSpec((N_GATES, L), lambda i, j, bb=bb: (0, off + (i * nb + bb) * nc + nc - 1 - j)),
        ]
        seq_args += [mqkv, mqkv, rqkv, rqkv, gcol, gcol, grow, grow]
    seq_out_shape = [jax.ShapeDtypeStruct((b, s, w_), F32) for w_ in (ML_W, ML_W, RET_W, RET_W)]
    seq_out_specs = [pl.BlockSpec((nb, L, ML_W), lambda i, j: (i, j, 0)),
                     pl.BlockSpec((nb, L, ML_W), lambda i, j: (i, nc - 1 - j, 0)),
                     pl.BlockSpec((nb, L, RET_W), lambda i, j: (i, j, 0)),
                     pl.BlockSpec((nb, L, RET_W), lambda i, j: (i, nc - 1 - j, 0))]
    state_shape = [
        jax.ShapeDtypeStruct((b,) + pair_mat, F32), jax.ShapeDtypeStruct((b,) + pair_vec, F32),
        jax.ShapeDtypeStruct((b,) + pair_vec, F32), jax.ShapeDtypeStruct((b,) + pair_mat, F32),
    ]
    state_specs = [st(pair_mat), st(pair_vec), st(pair_vec), st(pair_mat)]
    outs = pl.pallas_call(
        functools.partial(_scan_kernel, nb=nb),
        out_shape=seq_out_shape + state_shape,
        grid=(b // nb, nc),
        in_specs=seq_in_specs + [pl.BlockSpec((8, LANES), lambda i, j: (0, 0))] + state_specs,
        out_specs=seq_out_specs + state_specs,
        scratch_shapes=[
            pltpu.VMEM((nb,) + pair_mat, F32), pltpu.VMEM((nb,) + pair_vec, F32),
            pltpu.VMEM((nb,) + pair_vec, F32), pltpu.VMEM((nb,) + pair_mat, F32),
            pltpu.VMEM((2, 2, 2 * L, L), F32), pltpu.VMEM((2, 2, L, LANES), F32),
            pltpu.VMEM((2, 2, L, LANES), F32), pltpu.VMEM((2, 2, 1, LANES), F32),
        ],
        compiler_params=_cparams(("arbitrary", "arbitrary")),
        name="scan_mixers",
    )(*seq_args, rd, c0, n0, m0, s0)
    return [o.reshape(b * s, o.shape[-1]) for o in outs[:4]] + list(outs[4:])


def _merge_kernel(x_ref, mod_ref, att1_ref, att2_ref, hf1_ref, hb1_ref, of1_ref, ob1_ref, hf2_ref, hb2_ref,
                  of2_ref, ob2_ref, mo_ref, rg_ref, bg_ref, wb_ref, wo_ref, gn_ref, *rest, moe, n1_tiles):
    sc1_refs = (hf1_ref, hb1_ref, of1_ref, ob1_ref)
    sc2_refs = (hf2_ref, hb2_ref, of2_ref, ob2_ref)
    if moe:
        router_ref, x1_ref, h2_ref, gates_ref = rest
    else:
        x1_ref, h2_ref = rest
    d = x_ref.shape[1]
    first = pl.program_id(0) < n1_tiles
    pick = lambda r1, r2, idx: jnp.where(first, r1[idx], r2[idx])
    mean_mat = _head_mean_matrix()
    att = jnp.concatenate([pick(att1_ref, att2_ref, s) for s in range(ATT_Q_W // LANES)], axis=1)
    y = jnp.dot(att, wb_ref[0:ATT_Q_W, :], preferred_element_type=F32) * _sigmoid(bg_ref[:, 0:d])
    n_t = ML_W // LANES
    sums = []
    for s in range(n_t):
        sl = (slice(None), slice(s * LANES, (s + 1) * LANES))
        hf, hb, of, ob = (pick(r1, r2, sl) for r1, r2 in zip(sc1_refs, sc2_refs))
        sums += [hf + hb, of + ob]
    normed = _head_rms(sums, mean_mat)
    ml = jnp.concatenate([(normed[2 * s] * _sigmoid(mo_ref[:, s * LANES:(s + 1) * LANES])).astype(BF16)
                          for s in range(n_t)], axis=1)
    rgs = [rg_ref[:, s * LANES:(s + 1) * LANES] for s in range(n_t)]
    ret = jnp.concatenate([(normed[2 * s + 1] * (rgs[s] * _sigmoid(rgs[s]))).astype(BF16)
                           for s in range(n_t)], axis=1)
    y += jnp.dot(ml, wb_ref[ATT_Q_W:ATT_Q_W + ML_W, :], preferred_element_type=F32) * _sigmoid(bg_ref[:, d:2 * d])
    y += jnp.dot(ret, wb_ref[ATT_Q_W + ML_W:, :], preferred_element_type=F32) * _sigmoid(bg_ref[:, 2 * d:3 * d])
    y = jnp.dot(y.astype(BF16), wo_ref[...], preferred_element_type=F32)
    x1 = x_ref[...] + mod_ref[2:3, :] * y
    x1_ref[...] = x1
    ms = jnp.mean(x1 * x1, axis=-1, keepdims=True)
    h2 = x1 * lax.rsqrt(ms + EPS) * gn_ref[...]
    h2 = h2 * (1.0 + mod_ref[4:5, :]) + mod_ref[3:4, :]
    if moe:
        _store_token_tiles(h2_ref, h2)
    else:
        h2_ref[...] = h2.astype(BF16)
    if moe:
        hh = h2.astype(BF16)
        hl = (h2 - hh.astype(F32)).astype(BF16)
        r = router_ref[...]
        rh = r.astype(BF16)
        rl = (r - rh.astype(F32)).astype(BF16)
        tm = h2.shape[0]
        prod = jnp.dot(jnp.concatenate([hh, hl], axis=0), jnp.concatenate([rh, rl], axis=1),
                       preferred_element_type=F32)
        logits = (prod[:tm, :LANES] + prod[tm:, :LANES]) + (prod[:tm, LANES:] + prod[tm:, LANES:])
        lane = _lane_iota()
        lg = jnp.where(lane < N_EXPERTS, logits, -jnp.inf)
        m1 = jnp.max(lg, axis=-1, keepdims=True)
        i1 = jnp.min(jnp.where(lg == m1, lane, LANES), axis=-1, keepdims=True)
        sel1 = lane == i1
        lg2 = jnp.where(sel1, -jnp.inf, lg)
        m2 = jnp.max(lg2, axis=-1, keepdims=True)
        i2 = jnp.min(jnp.where(lg2 == m2, lane, LANES), axis=-1, keepdims=True)
        e2 = jnp.exp(m2 - m1)
        den = 1.0 + e2
        gates_ref[...] = jnp.where(lane == 0, i1.astype(F32), jnp.where(
            lane == 1, i2.astype(F32), jnp.where(lane == 2, 1.0 / den, jnp.where(lane == 3, e2 / den, 0.0))))


def _merge_call(x, mod, att1, att2, scan1, scan2, mo, rg, bg, wb, wo, gn, router, n1, s2):
    n, d = x.shape
    tm = TOK_TILE
    n1t = n1 // tm
    t2 = s2 // tm
    moe = router is not None
    n_slabs = att1.shape[0]

    def mod_row(t):
        return jnp.where(t < n1t, 0, 1 + (t - n1t) // t2)

    g1 = lambda t: jnp.minimum(t, n1t - 1)
    g2 = lambda t: jnp.maximum(t - n1t, 0)
    row = lambda w_: pl.BlockSpec((tm, w_), lambda t: (t, 0))
    row1 = lambda w_: pl.BlockSpec((tm, w_), lambda t: (g1(t), 0))
    row2 = lambda w_: pl.BlockSpec((tm, w_), lambda t: (g2(t), 0))
    const = lambda shp: pl.BlockSpec(shp, lambda t: (0,) * len(shp))
    in_specs = [row(d), pl.BlockSpec((None, 6, d), lambda t: (mod_row(t), 0, 0)),
                pl.BlockSpec((n_slabs, tm, LANES), lambda t: (0, g1(t), 0)),
                pl.BlockSpec((n_slabs, tm, LANES), lambda t: (0, g2(t), 0)),
                row1(ML_W), row1(ML_W), row1(RET_W), row1(RET_W),
                row2(ML_W), row2(ML_W), row2(RET_W), row2(RET_W),
                row(ML_W), row(RET_W), row(3 * d),
                const((d, d)), const((d, d)), const((1, d))]
    args = [x, mod, att1, att2, *scan1, *scan2, mo, rg, bg, wb, wo, gn]
    if moe:
        out_shape = [jax.ShapeDtypeStruct((n, d), F32), jax.ShapeDtypeStruct((n, d // LANES, LANES), F32)]
        out_specs = [row(d), pl.BlockSpec((tm, d // LANES, LANES), lambda t: (t, 0, 0))]
    else:
        out_shape = [jax.ShapeDtypeStruct((n, d), F32), jax.ShapeDtypeStruct((n, d), BF16)]
        out_specs = [row(d), row(d)]
    if moe:
        in_specs.append(const((d, LANES)))
        args.append(router)
        out_shape.append(jax.ShapeDtypeStruct((n, LANES), F32))
        out_specs.append(row(LANES))
    return pl.pallas_call(
        functools.partial(_merge_kernel, moe=moe, n1_tiles=n1t),
        out_shape=out_shape,
        grid=(n // tm,),
        in_specs=in_specs,
        out_specs=out_specs,
        compiler_params=_cparams(("arbitrary",)),
        name="merge_out",
    )(*args)


def _final_norm(x, g):
    ms = jnp.mean(x * x, axis=-1, keepdims=True)
    return x * lax.rsqrt(ms + EPS) * g


def _ffn_kernel(h_ref, x1_ref, mod_ref, wg_ref, wu_ref, wd_ref, fn_ref, o_ref, *, n_chunks, final):
    h = h_ref[...]
    f = wg_ref.shape[1]
    fc = f // n_chunks
    acc = jnp.zeros(o_ref.shape, F32)
    for c in range(n_chunks):
        sl = slice(c * fc, (c + 1) * fc)
        g = jnp.dot(h, wg_ref[:, sl], preferred_element_type=F32)
        u = jnp.dot(h, wu_ref[:, sl], preferred_element_type=F32)
        a = (g * _sigmoid(g) * u).astype(BF16)
        acc += jnp.dot(a, wd_ref[sl, :], preferred_element_type=F32)
    x2 = x1_ref[...] + mod_ref[5:6, :] * acc
    o_ref[...] = _final_norm(x2, fn_ref[...]) if final else x2


def _ffn_call(h2, x1, mod, wg, wu, wd, fn, n1, s2, final):
    n, d = x1.shape
    f = wg.shape[1]
    tm = FFN_TILE

    def mod_row(t):
        return jnp.where(t * tm < n1, 0, 1 + (t * tm - n1) // s2)

    row = lambda dt: pl.BlockSpec((tm, d), lambda t: (t, 0))
    const = lambda shp: pl.BlockSpec(shp, lambda t: (0,) * len(shp), pipeline_mode=pl.Buffered(1))
    return pl.pallas_call(
        functools.partial(_ffn_kernel, n_chunks=2, final=final),
        out_shape=jax.ShapeDtypeStruct((n, d), F32),
        grid=(n // tm,),
        in_specs=[row(BF16), row(F32), pl.BlockSpec((None, 6, d), lambda t: (mod_row(t), 0, 0)),
                  const((d, f)), const((d, f)), const((f, d)), pl.BlockSpec((1, d), lambda t: (0, 0))],
        out_specs=row(F32),
        compiler_params=_cparams(("arbitrary",)),
        name="ffn_dense",
    )(h2, x1, mod, wg, wu, wd, fn)


def _route_tables(route, n):
    tmx = MOE_TILE
    e = jnp.concatenate([route[:, 0], route[:, 1]]).astype(jnp.int32)
    oh = (e[:, None] == jnp.arange(N_EXPERTS, dtype=jnp.int32)[None, :]).astype(jnp.int32)
    cs = jnp.cumsum(oh, axis=0)
    counts = cs[-1]
    rank = jnp.sum(oh * cs, axis=1) - 1
    padded = ((counts + tmx - 1) // tmx) * tmx
    pend = jnp.cumsum(padded)
    pos = jnp.sum(oh * (pend - padded)[None, :], axis=1) + rank
    n_rows = 2 * n + N_EXPERTS * tmx
    n_tiles = n_rows // tmx
    inv = jnp.full((n_rows,), -1, jnp.int32).at[pos].set(jnp.arange(2 * n, dtype=jnp.int32), unique_indices=True)
    valid = inv >= 0
    src = jnp.concatenate([jnp.where(valid, inv % n, 0), jnp.zeros((2 * tmx,), jnp.int32)])
    dump = 2 * n + jnp.arange(n_rows, dtype=jnp.int32) % (2 * tmx)
    dst = jnp.concatenate([2 * n + tmx + jnp.arange(tmx, dtype=jnp.int32), jnp.where(valid, inv, dump)])
    tile_e = jnp.sum((jnp.arange(n_tiles + 1, dtype=jnp.int32) * tmx)[:, None] >= pend[None, :], axis=1)
    tile_e = jnp.minimum(tile_e, N_EXPERTS - 1).astype(jnp.int32)
    return src.reshape(n_tiles + 2, 1, tmx), dst.reshape(n_tiles + 1, 1, tmx), tile_e


def _moe_routed_kernel(te_ref, src_ref, srcn_ref, dst_ref, h_hbm, wg_ref, wu_ref, wd_ref, yy_hbm,
                       xbuf, ybuf, sem_in, sem_out):
    del te_ref
    tmx = MOE_TILE
    t = pl.program_id(0)
    last = pl.num_programs(0) - 1
    slot = t % 2
    other = 1 - slot

    def gather(idx_ref, s):
        for r in range(tmx):
            pltpu.make_async_copy(h_hbm.at[idx_ref[0, r]], xbuf.at[s, r], sem_in.at[s]).start()

    def wait_gather(s):
        pltpu.make_async_copy(h_hbm.at[pl.ds(0, tmx)], xbuf.at[s], sem_in.at[s]).wait()

    def wait_scatter(s):
        pltpu.make_async_copy(ybuf.at[s], yy_hbm.at[pl.ds(0, tmx)], sem_out.at[s]).wait()

    @pl.when(t == 0)
    def _prologue():
        ybuf[...] = jnp.zeros_like(ybuf)
        gather(src_ref, 0)

    @pl.when(t >= 1)
    def _free_ybuf():
        wait_scatter(slot)

    wait_gather(slot)
    gather(srcn_ref, other)
    for r in range(tmx):
        pltpu.make_async_copy(ybuf.at[other, r], yy_hbm.at[dst_ref[0, r]], sem_out.at[other]).start()
    x = _load_token_tiles(xbuf.at[slot]).astype(BF16)
    g = jnp.dot(x, wg_ref[...], preferred_element_type=F32)
    u = jnp.dot(x, wu_ref[...], preferred_element_type=F32)
    a = (g * _sigmoid(g) * u).astype(BF16)
    _store_token_tiles(ybuf.at[slot], jnp.dot(a, wd_ref[...], preferred_element_type=F32))

    @pl.when(t == last)
    def _drain():
        wait_gather(other)
        wait_scatter(other)


def _moe_routed_call(h2, src, dst, tile_e, wg, wu, wd):
    n, nj, _ = h2.shape
    d = nj * LANES
    ne, _, f = wg.shape
    tmx = MOE_TILE
    n_tiles = tile_e.shape[0] - 1
    smem_blk = lambda fn_: pl.BlockSpec((None, 1, tmx), fn_, memory_space=pltpu.SMEM)
    grid_spec = pltpu.PrefetchScalarGridSpec(
        num_scalar_prefetch=1,
        grid=(n_tiles + 1,),
        in_specs=[
            smem_blk(lambda t, te: (t, 0, 0)),
            smem_blk(lambda t, te: (t + 1, 0, 0)),
            smem_blk(lambda t, te: (t, 0, 0)),
            pl.BlockSpec(memory_space=pl.ANY),
            pl.BlockSpec((None, d, f), lambda t, te: (te[t], 0, 0)),
            pl.BlockSpec((None, d, f), lambda t, te: (te[t], 0, 0)),
            pl.BlockSpec((None, f, d), lambda t, te: (te[t], 0, 0)),
        ],
        out_specs=pl.BlockSpec(memory_space=pl.ANY),
        scratch_shapes=[pltpu.VMEM((2, tmx, nj, LANES), F32), pltpu.VMEM((2, tmx, nj, LANES), F32),
                        pltpu.SemaphoreType.DMA((2,)), pltpu.SemaphoreType.DMA((2,))],
    )
    return pl.pallas_call(
        _moe_routed_kernel,
        out_shape=jax.ShapeDtypeStruct((2 * n + 2 * tmx, nj, LANES), F32),
        grid_spec=grid_spec,
        compiler_params=_cparams(("arbitrary",)),
        name="moe_experts",
    )(tile_e, src, src, dst, h2, wg, wu, wd)


def _moe_combine_kernel(x1_ref, mod_ref, route_ref, y1_ref, y2_ref, fn_ref, o_ref, *, final):
    w1 = route_ref[:, 2:3]
    w2 = route_ref[:, 3:4]
    y = w1 * _load_token_tiles(y1_ref) + w2 * _load_token_tiles(y2_ref)
    x2 = x1_ref[...] + mod_ref[5:6, :] * y
    o_ref[...] = _final_norm(x2, fn_ref[...]) if final else x2


def _moe_combine_call(x1, mod, route, yy, fn, n1, s2, final):
    n, d = x1.shape
    tm = FFN_TILE
    nt = n // tm

    def mod_row(t):
        return jnp.where(t * tm < n1, 0, 1 + (t * tm - n1) // s2)

    row = lambda w_: pl.BlockSpec((tm, w_), lambda t: (t, 0))
    return pl.pallas_call(
        functools.partial(_moe_combine_kernel, final=final),
        out_shape=jax.ShapeDtypeStruct((n, d), F32),
        grid=(nt,),
        in_specs=[row(d), pl.BlockSpec((None, 6, d), lambda t: (mod_row(t), 0, 0)), row(LANES),
                  pl.BlockSpec((tm, d // LANES, LANES), lambda t: (t, 0, 0)),
                  pl.BlockSpec((tm, d // LANES, LANES), lambda t: (nt + t, 0, 0)),
                  pl.BlockSpec((1, d), lambda t: (0, 0))],
        out_specs=row(d),
        compiler_params=_cparams(("arbitrary",)),
        name="moe_combine",
    )(x1, mod, route, yy, yy, fn)


def _pair_q_heads(a, axis):
    g = N_HEADS // N_KV_HEADS
    shp = a.shape
    a = a.reshape(shp[:axis] + (N_KV_HEADS, g, HEAD_DIM) + shp[axis + 1:])
    a = jnp.swapaxes(a, axis, axis + 1)
    return a.reshape(shp)


def _prep_in_weights(w_in, b_in):
    g0 = ATT_Q_W + 2 * ATT_KV_W + 4 * ML_W
    pad = N_IN_PAD - w_in.shape[-1]

    def reorder(a):
        parts = [_pair_q_heads(a[..., :ATT_Q_W], a.ndim - 1), a[..., ATT_Q_W:g0], a[..., g0 + N_GATES:],
                 a[..., g0:g0 + N_GATES], jnp.zeros(a.shape[:-1] + (pad,), a.dtype)]
        return jnp.concatenate(parts, axis=-1)

    return reorder(w_in).astype(BF16), reorder(b_in)[:, None, :]


def _rope_tables(s2, tm):
    pos = jnp.arange(s2)
    rowp = (pos // GRID_W).astype(F32)
    colp = (pos % GRID_W).astype(F32)
    quarter = HEAD_DIM // 4
    inv = ROPE_BASE ** (-jnp.arange(quarter, dtype=F32) / quarter)
    ang_r = rowp[:, None] * inv
    ang_c = colp[:, None] * inv
    cos_h = jnp.concatenate([jnp.cos(ang_r), jnp.cos(ang_r), jnp.cos(ang_c), jnp.cos(ang_c)], axis=1)
    sin_h = jnp.concatenate([-jnp.sin(ang_r), jnp.sin(ang_r), -jnp.sin(ang_c), jnp.sin(ang_c)], axis=1)
    cos_t = jnp.concatenate([cos_h, cos_h], axis=1)
    sin_t = jnp.concatenate([sin_h, sin_h], axis=1)
    cos_t = jnp.concatenate([jnp.ones((tm, LANES), F32), cos_t], axis=0)
    sin_t = jnp.concatenate([jnp.zeros((tm, LANES), F32), sin_t], axis=0)
    return cos_t, sin_t


def _pair_blockdiag(m):
    b, nd, h, d, _ = m.shape
    m = m.reshape(b, nd, h // 2, 2, d, d)
    z = jnp.zeros_like(m[:, :, :, 0])
    top = jnp.concatenate([m[:, :, :, 0], z], axis=-1)
    bot = jnp.concatenate([z, m[:, :, :, 1]], axis=-1)
    return jnp.concatenate([top, bot], axis=-2)


def _pair_unblock(m):
    b, nd, p, _, _ = m.shape
    a = m[:, :, :, :HALF, :HALF]
    c = m[:, :, :, HALF:, HALF:]
    return jnp.stack([a, c], axis=3).reshape(b, nd, 2 * p, HALF, HALF)


def kernel(x_prompt, x_sample, c, cache_attn_k, cache_attn_v, state_mlstm_c, state_mlstm_n, state_mlstm_m,
           state_ret_s, c_ctx, w_ada, b_ada, norm_mix, norm_ffn, w_in, b_in, q_norm, k_norm, ret_decay,
           w_branch, w_out, ffn_w_gate, ffn_w_up, ffn_w_down, moe_router, moe_w_gate, moe_w_up, moe_w_down,
           final_norm):
    b1, s1, d = x_prompt.shape
    b2, s2, _ = x_sample.shape
    depth = w_in.shape[0]
    n1, n2 = b1 * s1, b2 * s2
    assert s1 % TOK_TILE == 0 and s2 % FFN_TILE == 0 and n1 % FFN_TILE == 0 and b2 + 1 <= 8

    x = jnp.concatenate([x_prompt.reshape(n1, d), x_sample.reshape(n2, d)], axis=0)
    c8 = jnp.concatenate([c_ctx[None, :], c, jnp.zeros((8 - 1 - b2, d), F32)], axis=0)
    mod_all = _ada_call(c8, w_ada, b_ada).reshape(depth, 8, 6, d)

    w_in_p, b_in_p = _prep_in_weights(w_in, b_in)
    cos_t, sin_t = _rope_tables(s2, TOK_TILE)
    wb = jnp.concatenate([_pair_q_heads(w_branch[:, :ATT_Q_W], 1), w_branch[:, ATT_Q_W:]], axis=1).astype(BF16)
    wo = w_out.astype(BF16)
    fn = final_norm[None, :]

    zeros_like_state = lambda shp: jnp.zeros((b1,) + shp, F32)
    states = []
    for l in range(depth):
        mod = mod_all[l]
        qg = jnp.tile(q_norm[l], 2)[None, :]
        kg = jnp.tile(k_norm[l], 2)[None, :]
        q, k, v, kb, vb, mqkv, rqkv, mo, rg, bg, gcol, grow = _inproj_call(
            x, mod, norm_mix[l][None, :], w_in_p[l], b_in_p[l], qg, kg, cos_t, sin_t, n1, s2)

        kt1 = jnp.swapaxes(kb[:n1].reshape(b1, s1, ATT_KV_W), 1, 2)
        att1 = _attn_call(q, kt1, vb[:n1].reshape(b1, s1, ATT_KV_W), 0, s1)
        k2 = jnp.concatenate([cache_attn_k[:, l].reshape(b2, -1, ATT_KV_W).astype(BF16),
                              kb[n1:].reshape(b2, s2, ATT_KV_W)], axis=1)
        v2 = jnp.concatenate([cache_attn_v[:, l].reshape(b2, -1, ATT_KV_W).astype(BF16),
                              vb[n1:].reshape(b2, s2, ATT_KV_W)], axis=1)
        att2 = _attn_call(q, jnp.swapaxes(k2, 1, 2), v2, n1, s2)

        rd = jnp.broadcast_to(ret_decay[l].reshape(2 * RET_HEADS, 1), (2 * RET_HEADS, LANES))
        r1 = _scan_call(mqkv, rqkv, gcol, grow, rd,
                        zeros_like_state((2, 2, LANES, LANES)), zeros_like_state((2, 2, 1, LANES)),
                        zeros_like_state((2, 2, 1, LANES)), zeros_like_state((2, 2, LANES, LANES)), 0, s1)
        c0 = _pair_blockdiag(state_mlstm_c[:, l])
        n0 = state_mlstm_n[:, l].reshape(b2, 2, 2, 1, LANES)
        m0 = jnp.repeat(state_mlstm_m[:, l], HALF, axis=-1).reshape(b2, 2, 2, 1, LANES)
        s0 = _pair_blockdiag(state_ret_s[:, l])
        r2 = _scan_call(mqkv, rqkv, gcol, grow, rd, c0, n0, m0, s0, n1, s2)
        states.append((k[:n1].reshape(b1, s1, N_KV_HEADS, HEAD_DIM), v[:n1].reshape(b1, s1, N_KV_HEADS, HEAD_DIM),
                       _pair_unblock(r1[4]), r1[5].reshape(b1, 2, ML_HEADS, HEAD_DIM), r1[6][:, :, :, 0, ::HALF].reshape(b1, 2, ML_HEADS),
                       _pair_unblock(r1[7])))

        moe = l % 2 == 1
        jj = l // 2
        router = jnp.pad(moe_router[jj], ((0, 0), (0, LANES - N_EXPERTS))) if moe else None
        outs = _merge_call(x, mod, att1, att2, r1[:4], r2[:4], mo, rg, bg, wb[l], wo[l], norm_ffn[l][None, :],
                           router, n1, s2)
        final = l == depth - 1
        if moe:
            x1, h2, route = outs
            src, dst, tile_e = _route_tables(route, n1 + n2)
            yy = _moe_routed_call(h2, src, dst, tile_e, moe_w_gate[jj].astype(BF16), moe_w_up[jj].astype(BF16),
                                  moe_w_down[jj].astype(BF16))
            x = _moe_combine_call(x1, mod, route, yy, fn, n1, s2, final)
        else:
            x1, h2 = outs
            x = _ffn_call(h2, x1, mod, ffn_w_gate[jj].astype(BF16), ffn_w_up[jj].astype(BF16),
                          ffn_w_down[jj].astype(BF16), fn, n1, s2, final)

    y_prompt = x[:n1].reshape(b1, s1, d)
    y_sample = x[n1:].reshape(b2, s2, d)
    stack = lambda i: jnp.stack([s[i] for s in states], axis=1)
    return (y_prompt, y_sample, stack(0), stack(1), stack(2), stack(3), stack(4), stack(5))
```

```python
import functools

import jax
import jax.numpy as jnp
from jax import lax
from jax.experimental import pallas as pl
from jax.experimental.pallas import tpu as pltpu

F32 = jnp.float32
BF16 = jnp.bfloat16

N_HEADS = 8
N_KV_HEADS = 2
HEAD_DIM = 64
ML_HEADS = 4
RET_HEADS = 4
GRID_W = 64
CHUNK = 128
ROPE_BASE = 10000.0
EPS = 1e-6
N_EXPERTS = 8
LANES = 128
HALF = 64

ATT_Q_W = N_HEADS * HEAD_DIM
ATT_KV_W = N_KV_HEADS * HEAD_DIM
ML_W = ML_HEADS * HEAD_DIM
RET_W = RET_HEADS * HEAD_DIM
N_GATES = 4 * ML_HEADS

TOK_TILE = 256
SCAN_SEQS = 2
FFN_TILE = 512
MOE_TILE = 512
ATT_Q_TILE = 256
ATT_K_TILE = 512
LOG2E = 1.4426950408889634
Q_SCALE = 0.125 * LOG2E
VMEM_LIMIT = 56 * 1024 * 1024

O_AQ, O_AK, O_AV = 0, 512, 640
O_MQ, O_MK, O_MV, O_MO = 768, 1024, 1280, 1536
O_RQ, O_RK, O_RV, O_RG = 1792, 2048, 2304, 2560
O_BG = 2816
O_MG = 5888
N_IN_PAD = 6016


def _cparams(sem, vmem=VMEM_LIMIT):
    return pltpu.CompilerParams(dimension_semantics=sem, vmem_limit_bytes=vmem)


def _lane_iota(shape=(1, LANES)):
    return lax.broadcasted_iota(jnp.int32, shape, len(shape) - 1)


def _head_mean_matrix():
    r = lax.broadcasted_iota(jnp.int32, (LANES, LANES), 0) >> 6
    c = lax.broadcasted_iota(jnp.int32, (LANES, LANES), 1) >> 6
    return jnp.where(r == c, 1.0 / HALF, 0.0).astype(BF16)


def _head_rms(xs, mean_mat):
    t = xs[0].shape[0]
    sq = jnp.concatenate([x * x for x in xs], axis=0)
    hi = sq.astype(BF16)
    lo = (sq - hi.astype(F32)).astype(BF16)
    ms = jnp.dot(jnp.concatenate([hi, lo], axis=0), mean_mat, preferred_element_type=F32)
    ms = ms[:len(xs) * t] + ms[len(xs) * t:]
    return [x * lax.rsqrt(ms[i * t:(i + 1) * t] + EPS) for i, x in enumerate(xs)]


def _rope(y, cos, sin, lane):
    up = pltpu.roll(y, LANES - 16, 1)
    dn = pltpu.roll(y, 16, 1)
    partner = jnp.where((lane & 31) < 16, up, dn)
    return y * cos + partner * sin


def _log_sigmoid(x):
    return jnp.minimum(x, 0.0) - jnp.log1p(jnp.exp(-jnp.abs(x)))


def _sigmoid(x):
    return 1.0 / (1.0 + jnp.exp(-x))


def _store_token_tiles(ref, x):
    for j in range(x.shape[1] // LANES):
        ref[:, j, :] = x[:, j * LANES:(j + 1) * LANES]


def _load_token_tiles(ref):
    return jnp.concatenate([ref[:, j, :] for j in range(ref.shape[1])], axis=1)


def _split3(x):
    h = x.astype(BF16)
    r = x - h.astype(F32)
    m = r.astype(BF16)
    l = (r - m.astype(F32)).astype(BF16)
    return h, m, l


def _ada_kernel(c_ref, w_ref, b_ref, o_ref):
    c = c_ref[...]
    a = (c * _sigmoid(c)).astype(BF16)
    o_ref[...] = jnp.dot(a, w_ref[...].astype(BF16), preferred_element_type=F32) + b_ref[...]


def _ada_call(c8, w_ada, b_ada):
    depth, d, n6 = w_ada.shape
    tn = 1536
    return pl.pallas_call(
        _ada_kernel,
        out_shape=jax.ShapeDtypeStruct((depth, 8, n6), F32),
        grid=(depth, n6 // tn),
        in_specs=[
            pl.BlockSpec((8, d), lambda l, j: (0, 0)),
            pl.BlockSpec((None, d, tn), lambda l, j: (l, 0, j)),
            pl.BlockSpec((None, 1, tn), lambda l, j: (l, 0, j)),
        ],
        out_specs=pl.BlockSpec((None, 8, tn), lambda l, j: (l, 0, j)),
        compiler_params=_cparams(("arbitrary", "arbitrary")),
        name="ada_mod",
    )(c8, w_ada, b_ada.reshape(depth, 1, n6))


def _inproj_kernel(x_ref, mod_ref, gn_ref, w_ref, b_ref, qg_ref, kg_ref, cos_ref, sin_ref,
                   q_ref, k_ref, v_ref, kb_ref, vb_ref, mqkv_ref, rqkv_ref, mo_ref, rg_ref, bg_ref, gcol_ref,
                   grow_ref):
    x = x_ref[...]
    ms = jnp.mean(x * x, axis=-1, keepdims=True)
    h = x * lax.rsqrt(ms + EPS) * gn_ref[...]
    h = h * (1.0 + mod_ref[1:2, :]) + mod_ref[0:1, :]
    hb = h.astype(BF16)

    def seg(a, b):
        return jnp.dot(hb, w_ref[:, a:b], preferred_element_type=F32) + b_ref[:, a:b]

    lane = _lane_iota()
    mean_mat = _head_mean_matrix()
    cos = cos_ref[...]
    sin = sin_ref[...]
    n_q = ATT_Q_W // LANES
    za = seg(O_AQ, O_MQ)
    normed = _head_rms([za[:, s * LANES:(s + 1) * LANES] for s in range(n_q + 1)], mean_mat)
    for s in range(n_q):
        q_ref[s] = (_rope(normed[s] * qg_ref[...], cos, sin, lane) * Q_SCALE).astype(BF16)
    y = _rope(normed[n_q] * kg_ref[...], cos, sin, lane)
    k_ref[...] = y
    kb_ref[...] = y.astype(BF16)
    y = za[:, O_AV:O_AV + LANES]
    v_ref[...] = y
    vb_ref[...] = y.astype(BF16)

    zm = seg(O_MQ, O_RQ)
    mqkv_ref[:, 0:ML_W] = zm[:, 0:ML_W].astype(BF16)
    mqkv_ref[:, ML_W:2 * ML_W] = (zm[:, ML_W:2 * ML_W] * 0.125).astype(BF16)
    mqkv_ref[:, 2 * ML_W:3 * ML_W] = zm[:, 2 * ML_W:3 * ML_W].astype(BF16)
    mo_ref[...] = zm[:, 3 * ML_W:]

    zr = seg(O_RQ, O_BG)
    for s in range(RET_W // LANES):
        sl = slice(s * LANES, (s + 1) * LANES)
        rqkv_ref[:, sl] = _rope(zr[:, sl], cos, sin, lane).astype(BF16)
        slk = slice(RET_W + s * LANES, RET_W + (s + 1) * LANES)
        rqkv_ref[:, slk] = (_rope(zr[:, slk], cos, sin, lane) * 0.125).astype(BF16)
    rqkv_ref[:, 2 * RET_W:3 * RET_W] = zr[:, 2 * RET_W:3 * RET_W].astype(BF16)
    rg_ref[...] = zr[:, 3 * RET_W:]

    zg = seg(O_BG, N_IN_PAD)
    bg_ref[...] = zg[:, :O_MG - O_BG]

    g = zg[:, O_MG - O_BG:]
    is_f = ((lane >> 2) & 1) == 1
    g = jnp.where(is_f, _log_sigmoid(g), g)
    gcol_ref[...] = g
    grow_ref[...] = g.T[0:N_GATES, :]


def _inproj_call(x, mod, gn, w, b, qg, kg, cos_t, sin_t, n1, s2):
    n, d = x.shape
    tm = TOK_TILE
    n1t = n1 // tm
    t2 = s2 // tm

    def mod_row(t):
        return jnp.where(t < n1t, 0, 1 + (t - n1t) // t2)

    def tab_row(t):
        return jnp.where(t < n1t, 0, 1 + (t - n1t) % t2)

    row = lambda w_: pl.BlockSpec((tm, w_), lambda t: (t, 0))
    const = lambda shp: pl.BlockSpec(shp, lambda t: (0,) * len(shp))
    n_slabs = ATT_Q_W // LANES
    out_shape = [
        jax.ShapeDtypeStruct((n_slabs, n, LANES), BF16),
        jax.ShapeDtypeStruct((n, ATT_KV_W), F32),
        jax.ShapeDtypeStruct((n, ATT_KV_W), F32),
        jax.ShapeDtypeStruct((n, ATT_KV_W), BF16),
        jax.ShapeDtypeStruct((n, ATT_KV_W), BF16),
        jax.ShapeDtypeStruct((n, 3 * ML_W), BF16),
        jax.ShapeDtypeStruct((n, 3 * RET_W), BF16),
        jax.ShapeDtypeStruct((n, ML_W), F32),
        jax.ShapeDtypeStruct((n, RET_W), F32),
        jax.ShapeDtypeStruct((n, 3 * d), F32),
        jax.ShapeDtypeStruct((n, LANES), F32),
        jax.ShapeDtypeStruct((N_GATES, n), F32),
    ]
    out_specs = [pl.BlockSpec((n_slabs, tm, LANES), lambda t: (0, t, 0)), row(ATT_KV_W), row(ATT_KV_W),
                 row(ATT_KV_W), row(ATT_KV_W), row(3 * ML_W), row(3 * RET_W), row(ML_W),
                 row(RET_W), row(3 * d), row(LANES), pl.BlockSpec((N_GATES, tm), lambda t: (0, t))]
    return pl.pallas_call(
        _inproj_kernel,
        out_shape=out_shape,
        grid=(n // tm,),
        in_specs=[
            row(d),
            pl.BlockSpec((None, 6, d), lambda t: (mod_row(t), 0, 0)),
            const((1, d)),
            const((d, N_IN_PAD)),
            const((1, N_IN_PAD)),
            const((1, LANES)),
            const((1, LANES)),
            pl.BlockSpec((tm, LANES), lambda t: (tab_row(t), 0)),
            pl.BlockSpec((tm, LANES), lambda t: (tab_row(t), 0)),
        ],
        out_specs=out_specs,
        compiler_params=_cparams(("arbitrary",)),
        name="in_proj",
    )(x, mod, gn, w, b, qg, kg, cos_t, sin_t)


def _attn_kernel(q_ref, k_ref, vt_ref, o_ref, s_sc, *, tk):
    tq = q_ref.shape[1]
    nk = k_ref.shape[0] // tk
    row_lo = lax.broadcasted_iota(jnp.int32, (LANES, 1), 0) < HALF

    def slab(j, carry):
        qt = q_ref[j].astype(F32).T.astype(BF16)
        outs = []
        for half in range(2):
            keep = row_lo if half == 0 else jnp.logical_not(row_lo)
            qm = jnp.where(keep, qt, jnp.zeros_like(qt))
            mx = None
            for c in range(nk):
                s = jnp.dot(k_ref[c * tk:(c + 1) * tk, :], qm, preferred_element_type=F32)
                s_sc[half, c * tk:(c + 1) * tk, :] = s
                cm = jnp.max(s, axis=0, keepdims=True)
                mx = cm if mx is None else jnp.maximum(mx, cm)
            l = jnp.zeros((1, tq), F32)
            acc = jnp.zeros((LANES, tq), F32)
            for c in range(nk):
                p = jnp.exp2(s_sc[half, c * tk:(c + 1) * tk, :] - mx)
                l += jnp.sum(p, axis=0, keepdims=True)
                acc += jnp.dot(vt_ref[:, c * tk:(c + 1) * tk], p.astype(BF16), preferred_element_type=F32)
            outs.append(acc * (1.0 / l))
        o_ref[j] = jnp.where(row_lo, outs[0], outs[1]).T.astype(o_ref.dtype)
        return carry

    lax.fori_loop(0, q_ref.shape[0], slab, 0)


def _attn_call(q, k, vt, tok_off, sq):
    n_slabs = q.shape[0]
    b, sk, _ = k.shape
    tq = min(ATT_Q_TILE, sq)
    tk = min(ATT_K_TILE, sk)
    nq = sq // tq
    off = tok_off // tq
    return pl.pallas_call(
        functools.partial(_attn_kernel, tk=tk),
        out_shape=jax.ShapeDtypeStruct((n_slabs, b * sq, LANES), BF16),
        grid=(b, nq),
        in_specs=[
            pl.BlockSpec((n_slabs, tq, LANES), lambda i, j: (0, off + i * nq + j, 0)),
            pl.BlockSpec((None, sk, ATT_KV_W), lambda i, j: (i, 0, 0)),
            pl.BlockSpec((None, ATT_KV_W, sk), lambda i, j: (i, 0, 0)),
        ],
        out_specs=pl.BlockSpec((n_slabs, tq, LANES), lambda i, j: (0, i * nq + j, 0)),
        scratch_shapes=[pltpu.VMEM((2, sk, tq), F32)],
        compiler_params=_cparams(("arbitrary", "arbitrary")),
        name="attention",
    )(q, k, vt)


def _scan_kernel(*refs, nb):
    L = CHUNK
    seq_in = refs[:8 * nb]
    rd_ref, c0_ref, n0_ref, m0_ref, s0_ref = refs[8 * nb:8 * nb + 5]
    o0 = 8 * nb + 5
    hf_ref, hb_ref, of_ref, ob_ref, cout_ref, nout_ref, mout_ref, sout_ref = refs[o0:o0 + 8]
    c_all, n_all, m_all, s_all, dec_st, qdec_st, kdec_st, cdec_st = refs[o0 + 8:]
    b_idx = pl.program_id(0)
    j = pl.program_id(1)
    nc = pl.num_programs(1)
    lane = _lane_iota()
    lo = lane < HALF
    row_i = lax.broadcasted_iota(jnp.int32, (L, L), 0)
    col_i = lax.broadcasted_iota(jnp.int32, (L, L), 1)
    blockmask = (row_i >> 6) == (col_i >> 6)
    causal = (row_i >= col_i, row_i <= col_i)

    @pl.when(jnp.logical_and(b_idx == 0, j == 0))
    def _init_tables():
        lg = _log_sigmoid(rd_ref[...])
        pos = lax.broadcasted_iota(jnp.int32, (L, 1), 0).astype(F32)
        diff = (row_i - col_i).astype(F32)
        for d in range(2):
            sd = diff if d == 0 else -diff
            for p in range(2):
                qd, kd, cd = [], [], []
                for e in range(2):
                    r = d * RET_HEADS + 2 * p + e
                    g = lg[r:r + 1, 0:1]
                    dec_st[d, p, e * L:(e + 1) * L, :] = jnp.where(sd >= 0, jnp.exp(g * jnp.maximum(sd, 0.0)), 0.0)
                    if d == 0:
                        qd.append(jnp.exp(g * (pos + 1.0)))
                        kd.append(jnp.exp(g * (L - 1.0 - pos)))
                    else:
                        qd.append(jnp.exp(g * (L - pos)))
                        kd.append(jnp.exp(g * pos))
                    cd.append(jnp.exp(g * float(L)))
                qdec_st[d, p] = jnp.where(lo, qd[0], qd[1])
                kdec_st[d, p] = jnp.where(lo, kd[0], kd[1])
                cdec_st[d, p] = jnp.where(lo, cd[0], cd[1])

    @pl.when(j == 0)
    def _load_state():
        c_all[...] = c0_ref[...]
        n_all[...] = n0_ref[...]
        m_all[...] = m0_ref[...]
        s_all[...] = s0_ref[...]

    tri = (row_i >= col_i).astype(BF16)
    triu = (row_i <= col_i).astype(BF16)

    def cumsums(gc_ref, gr_ref, d):
        a_col, a_row = (tri, triu) if d == 0 else (triu, tri)
        col3 = jnp.dot(a_col, jnp.concatenate(_split3(gc_ref[...]), axis=1), preferred_element_type=F32)
        row3 = jnp.dot(jnp.concatenate(_split3(gr_ref[...]), axis=0), a_row, preferred_element_type=F32)
        col = col3[:, 0:LANES] + col3[:, LANES:2 * LANES] + col3[:, 2 * LANES:]
        rowv = row3[0:N_GATES] + row3[N_GATES:2 * N_GATES] + row3[2 * N_GATES:]
        return col, rowv

    pair = lambda x0, x1: jnp.where(lo, x0, x1)
    lo2 = (_lane_iota((1, 2 * LANES)) & (LANES - 1)) < HALF

    keys = [(bb, d, p) for bb in range(nb) for d in range(2) for p in range(2)]
    c_old = {k: c_all[k] for k in keys}
    n_old = {k: n_all[k] for k in keys}
    m_old = {k: m_all[k] for k in keys}
    s_old = {k: s_all[k] for k in keys}
    c_new, n_new, m_new_st, s_new = {}, {}, {}, {}

    for bb in range(nb):
        mf_ref, mb_ref, rf_ref, rb_ref, gcf_ref, gcb_ref, grf_ref, grb_ref = seq_in[8 * bb:8 * bb + 8]
        for d, (m_ref, r_ref, gc_ref, gr_ref, h_out, o_out) in enumerate(
                ((mf_ref, rf_ref, gcf_ref, grf_ref, hf_ref, of_ref),
                 (mb_ref, rb_ref, gcb_ref, grb_ref, hb_ref, ob_ref))):
            gcol = gc_ref[...]
            grow = gr_ref[...]
            cum_col, cum_row = cumsums(gc_ref, gr_ref, d)
            gi = 2 * ML_HEADS * d
            gf = gi + ML_HEADS
            last = L - 1 if d == 0 else 0
            for p in range(2):
                sl = slice(p * LANES, (p + 1) * LANES)
                mq, mk, mv = (m_ref[:, i * ML_W + p * LANES:i * ML_W + (p + 1) * LANES] for i in range(3))
                rq, rk, rv = (r_ref[:, i * RET_W + p * LANES:i * RET_W + (p + 1) * LANES] for i in range(3))
                c2 = c_old[bb, d, p]
                n2 = n_old[bb, d, p]
                m2 = m_old[bb, d, p]
                s2 = s_old[bb, d, p]
                zero = jnp.zeros_like(mq)
                zero_st = jnp.zeros((LANES, LANES), BF16)
                bdiag = lambda x0, x1, z: jnp.concatenate(
                    [jnp.concatenate([x0, z], axis=1), jnp.concatenate([z, x1], axis=1)], axis=0)
                q_cat = jnp.concatenate([mq, rq], axis=1)
                q_st = jnp.dot(q_cat, bdiag(c2.astype(BF16), s2.astype(BF16), zero_st), preferred_element_type=F32)
                q_c = q_st[:, :LANES]
                q_s = q_st[:, LANES:] * qdec_st[d, p]
                q_n = mq.astype(F32) * n2
                hds = (2 * p, 2 * p + 1)
                i_cols = [gcol[:, gi + hd:gi + hd + 1] for hd in hds]
                b_cols = [cum_col[:, gf + hd:gf + hd + 1] for hd in hds]
                b2 = pair(*b_cols)
                i2 = pair(*i_cols)
                a2 = b2 + m2
                dms = []
                for e, hd in enumerate(hds):
                    i_row = grow[gi + hd:gi + hd + 1, :]
                    b_row = cum_row[gf + hd:gf + hd + 1, :]
                    dms.append(jnp.where(causal[d], b_cols[e] - b_row + i_row, -jnp.inf))
                mt2 = jnp.maximum(a2, pair(*[jnp.max(dm, axis=-1, keepdims=True) for dm in dms]))
                wa2 = jnp.exp(a2 - mt2)
                qs = jnp.concatenate([jnp.where(lo2, q_cat, jnp.zeros_like(q_cat)),
                                      jnp.where(lo2, jnp.zeros_like(q_cat), q_cat)], axis=0)
                scores = lax.dot_general(qs, bdiag(mk, rk, zero), (((1,), (1,)), ((), ())),
                                         preferred_element_type=F32)
                wd = jnp.concatenate([jnp.exp(dms[e] - mt2[:, e * HALF:e * HALF + 1]) for e in range(2)], axis=0)
                s_m = scores[:, :L] * wd
                s_r = scores[:, L:] * dec_st[d, p]
                sv = jnp.dot(jnp.concatenate([s_m, s_r], axis=1).astype(BF16), bdiag(mv, rv, zero),
                             preferred_element_type=F32)
                rs = jnp.sum(s_m, axis=-1, keepdims=True)
                qns = [jnp.sum(jnp.where(lo if e == 0 else jnp.logical_not(lo), q_n, 0.0), axis=-1, keepdims=True)
                       for e in range(2)]
                den2 = pair(rs[:L], rs[L:]) + wa2 * pair(*qns)
                dd2 = jnp.maximum(jnp.abs(den2), jnp.exp(-mt2))
                h_out[bb, :, sl] = (pair(sv[:L, :LANES], sv[L:, :LANES]) + wa2 * q_c) / dd2
                o_out[bb, :, sl] = pair(sv[:L, LANES:], sv[L:, LANES:]) + q_s
                b_last = b2[last:last + 1, :]
                g2 = b_last - b2 + i2
                m_new = jnp.maximum(b_last + m2, jnp.max(g2, axis=0, keepdims=True))
                wc2 = jnp.exp(b_last + m2 - m_new)
                kw_m = mk.astype(F32) * jnp.exp(g2 - m_new)
                kw_r = rk.astype(F32) * kdec_st[d, p]
                upd = lax.dot_general(jnp.concatenate([kw_m, kw_r], axis=1).astype(BF16),
                                      jnp.concatenate([mv, rv], axis=1), (((0,), (0,)), ((), ())),
                                      preferred_element_type=F32)
                c_new[bb, d, p] = jnp.where(blockmask, wc2 * c2 + upd[:LANES, :LANES], 0.0)
                n_new[bb, d, p] = wc2 * n2 + jnp.sum(kw_m, axis=0, keepdims=True)
                m_new_st[bb, d, p] = m_new
                s_new[bb, d, p] = jnp.where(blockmask, cdec_st[d, p] * s2 + upd[LANES:, LANES:], 0.0)

    for k in keys:
        c_all[k] = c_new[k]
        n_all[k] = n_new[k]
        m_all[k] = m_new_st[k]
        s_all[k] = s_new[k]

    @pl.when(j == nc - 1)
    def _store_state():
        cout_ref[...] = c_all[...]
        nout_ref[...] = n_all[...]
        mout_ref[...] = m_all[...]
        sout_ref[...] = s_all[...]


def _scan_call(mqkv, rqkv, gcol, grow, rd, c0, n0, m0, s0, tok_off, s):
    b = c0.shape[0]
    nb = SCAN_SEQS
    L = CHUNK
    nc = s // L
    off = tok_off // L
    fwd = lambda w_, bb: pl.BlockSpec((L, w_), lambda i, j: (off + (i * nb + bb) * nc + j, 0))
    bwd = lambda w_, bb: pl.BlockSpec((L, w_), lambda i, j: (off + (i * nb + bb) * nc + nc - 1 - j, 0))
    st = lambda shp: pl.BlockSpec((nb,) + shp, lambda i, j: (i,) + (0,) * len(shp))
    pair_mat = (2, 2, LANES, LANES)
    pair_vec = (2, 2, 1, LANES)
    seq_in_specs, seq_args = [], []
    for bb in range(nb):
        seq_in_specs += [
            fwd(3 * ML_W, bb), bwd(3 * ML_W, bb), fwd(3 * RET_W, bb), bwd(3 * RET_W, bb),
            fwd(LANES, bb), bwd(LANES, bb),
            pl.BlockSpec((N_GATES, L), lambda i, j, bb=bb: (0, off + (i * nb + bb) * nc + j)),
            pl.BlockSpec((N_GATES, L), lambda i, j, bb=bb: (0, off + (i * nb + bb) * nc + nc - 1 - j)),
        ]
        seq_args += [mqkv, mqkv, rqkv, rqkv, gcol, gcol, grow, grow]
    seq_out_shape = [jax.ShapeDtypeStruct((b, s, w_), F32) for w_ in (ML_W, ML_W, RET_W, RET_W)]
    seq_out_specs = [pl.BlockSpec((nb, L, ML_W), lambda i, j: (i, j, 0)),
                     pl.BlockSpec((nb, L, ML_W), lambda i, j: (i, nc - 1 - j, 0)),
                     pl.BlockSpec((nb, L, RET_W), lambda i, j: (i, j, 0)),
                     pl.BlockSpec((nb, L, RET_W), lambda i, j: (i, nc - 1 - j, 0))]
    state_shape = [
        jax.ShapeDtypeStruct((b,) + pair_mat, F32), jax.ShapeDtypeStruct((b,) + pair_vec, F32),
        jax.ShapeDtypeStruct((b,) + pair_vec, F32), jax.ShapeDtypeStruct((b,) + pair_mat, F32),
    ]
    state_specs = [st(pair_mat), st(pair_vec), st(pair_vec), st(pair_mat)]
    outs = pl.pallas_call(
        functools.partial(_scan_kernel, nb=nb),
        out_shape=seq_out_shape + state_shape,
        grid=(b // nb, nc),
        in_specs=seq_in_specs + [pl.BlockSpec((8, LANES), lambda i, j: (0, 0))] + state_specs,
        out_specs=seq_out_specs + state_specs,
        scratch_shapes=[
            pltpu.VMEM((nb,) + pair_mat, F32), pltpu.VMEM((nb,) + pair_vec, F32),
            pltpu.VMEM((nb,) + pair_vec, F32), pltpu.VMEM((nb,) + pair_mat, F32),
            pltpu.VMEM((2, 2, 2 * L, L), F32), pltpu.VMEM((2, 2, L, LANES), F32),
            pltpu.VMEM((2, 2, L, LANES), F32), pltpu.VMEM((2, 2, 1, LANES), F32),
        ],
        compiler_params=_cparams(("arbitrary", "arbitrary")),
        name="scan_mixers",
    )(*seq_args, rd, c0, n0, m0, s0)
    return [o.reshape(b * s, o.shape[-1]) for o in outs[:4]] + list(outs[4:])


def _merge_kernel(x_ref, mod_ref, att1_ref, att2_ref, hf1_ref, hb1_ref, of1_ref, ob1_ref, hf2_ref, hb2_ref,
                  of2_ref, ob2_ref, mo_ref, rg_ref, bg_ref, wb_ref, wo_ref, gn_ref, *rest, moe, n1_tiles):
    sc1_refs = (hf1_ref, hb1_ref, of1_ref, ob1_ref)
    sc2_refs = (hf2_ref, hb2_ref, of2_ref, ob2_ref)
    if moe:
        router_ref, x1_ref, h2_ref, gates_ref = rest
    else:
        x1_ref, h2_ref = rest
    d = x_ref.shape[1]
    first = pl.program_id(0) < n1_tiles
    pick = lambda r1, r2, idx: jnp.where(first, r1[idx], r2[idx])
    mean_mat = _head_mean_matrix()
    att = jnp.concatenate([pick(att1_ref, att2_ref, s) for s in range(ATT_Q_W // LANES)], axis=1)
    y = jnp.dot(att, wb_ref[0:ATT_Q_W, :], preferred_element_type=F32) * _sigmoid(bg_ref[:, 0:d])
    n_t = ML_W // LANES
    sums = []
    for s in range(n_t):
        sl = (slice(None), slice(s * LANES, (s + 1) * LANES))
        hf, hb, of, ob = (pick(r1, r2, sl) for r1, r2 in zip(sc1_refs, sc2_refs))
        sums += [hf + hb, of + ob]
    normed = _head_rms(sums, mean_mat)
    ml = jnp.concatenate([(normed[2 * s] * _sigmoid(mo_ref[:, s * LANES:(s + 1) * LANES])).astype(BF16)
                          for s in range(n_t)], axis=1)
    rgs = [rg_ref[:, s * LANES:(s + 1) * LANES] for s in range(n_t)]
    ret = jnp.concatenate([(normed[2 * s + 1] * (rgs[s] * _sigmoid(rgs[s]))).astype(BF16)
                           for s in range(n_t)], axis=1)
    y += jnp.dot(ml, wb_ref[ATT_Q_W:ATT_Q_W + ML_W, :], preferred_element_type=F32) * _sigmoid(bg_ref[:, d:2 * d])
    y += jnp.dot(ret, wb_ref[ATT_Q_W + ML_W:, :], preferred_element_type=F32) * _sigmoid(bg_ref[:, 2 * d:3 * d])
    y = jnp.dot(y.astype(BF16), wo_ref[...], preferred_element_type=F32)
    x1 = x_ref[...] + mod_ref[2:3, :] * y
    x1_ref[...] = x1
    ms = jnp.mean(x1 * x1, axis=-1, keepdims=True)
    h2 = x1 * lax.rsqrt(ms + EPS) * gn_ref[...]
    h2 = h2 * (1.0 + mod_ref[4:5, :]) + mod_ref[3:4, :]
    if moe:
        _store_token_tiles(h2_ref, h2)
    else:
        h2_ref[...] = h2.astype(BF16)
    if moe:
        hh = h2.astype(BF16)
        hl = (h2 - hh.astype(F32)).astype(BF16)
        r = router_ref[...]
        rh = r.astype(BF16)
        rl = (r - rh.astype(F32)).astype(BF16)
        tm = h2.shape[0]
        prod = jnp.dot(jnp.concatenate([hh, hl], axis=0), jnp.concatenate([rh, rl], axis=1),
                       preferred_element_type=F32)
        logits = (prod[:tm, :LANES] + prod[tm:, :LANES]) + (prod[:tm, LANES:] + prod[tm:, LANES:])
        lane = _lane_iota()
        lg = jnp.where(lane < N_EXPERTS, logits, -jnp.inf)
        m1 = jnp.max(lg, axis=-1, keepdims=True)
        i1 = jnp.min(jnp.where(lg == m1, lane, LANES), axis=-1, keepdims=True)
        sel1 = lane == i1
        lg2 = jnp.where(sel1, -jnp.inf, lg)
        m2 = jnp.max(lg2, axis=-1, keepdims=True)
        i2 = jnp.min(jnp.where(lg2 == m2, lane, LANES), axis=-1, keepdims=True)
        e2 = jnp.exp(m2 - m1)
        den = 1.0 + e2
        gates_ref[...] = jnp.where(lane == 0, i1.astype(F32), jnp.where(
            lane == 1, i2.astype(F32), jnp.where(lane == 2, 1.0 / den, jnp.where(lane == 3, e2 / den, 0.0))))


def _merge_call(x, mod, att1, att2, scan1, scan2, mo, rg, bg, wb, wo, gn, router, n1, s2):
    n, d = x.shape
    tm = TOK_TILE
    n1t = n1 // tm
    t2 = s2 // tm
    moe = router is not None
    n_slabs = att1.shape[0]

    def mod_row(t):
        return jnp.where(t < n1t, 0, 1 + (t - n1t) // t2)

    g1 = lambda t: jnp.minimum(t, n1t - 1)
    g2 = lambda t: jnp.maximum(t - n1t, 0)
    row = lambda w_: pl.BlockSpec((tm, w_), lambda t: (t, 0))
    row1 = lambda w_: pl.BlockSpec((tm, w_), lambda t: (g1(t), 0))
    row2 = lambda w_: pl.BlockSpec((tm, w_), lambda t: (g2(t), 0))
    const = lambda shp: pl.BlockSpec(shp, lambda t: (0,) * len(shp))
    in_specs = [row(d), pl.BlockSpec((None, 6, d), lambda t: (mod_row(t), 0, 0)),
                pl.BlockSpec((n_slabs, tm, LANES), lambda t: (0, g1(t), 0)),
                pl.BlockSpec((n_slabs, tm, LANES), lambda t: (0, g2(t), 0)),
                row1(ML_W), row1(ML_W), row1(RET_W), row1(RET_W),
                row2(ML_W), row2(ML_W), row2(RET_W), row2(RET_W),
                row(ML_W), row(RET_W), row(3 * d),
                const((d, d)), const((d, d)), const((1, d))]
    args = [x, mod, att1, att2, *scan1, *scan2, mo, rg, bg, wb, wo, gn]
    if moe:
        out_shape = [jax.ShapeDtypeStruct((n, d), F32), jax.ShapeDtypeStruct((n, d // LANES, LANES), F32)]
        out_specs = [row(d), pl.BlockSpec((tm, d // LANES, LANES), lambda t: (t, 0, 0))]
    else:
        out_shape = [jax.ShapeDtypeStruct((n, d), F32), jax.ShapeDtypeStruct((n, d), BF16)]
        out_specs = [row(d), row(d)]
    if moe:
        in_specs.append(const((d, LANES)))
        args.append(router)
        out_shape.append(jax.ShapeDtypeStruct((n, LANES), F32))
        out_specs.append(row(LANES))
    return pl.pallas_call(
        functools.partial(_merge_kernel, moe=moe, n1_tiles=n1t),
        out_shape=out_shape,
        grid=(n // tm,),
        in_specs=in_specs,
        out_specs=out_specs,
        compiler_params=_cparams(("arbitrary",)),
        name="merge_out",
    )(*args)


def _final_norm(x, g):
    ms = jnp.mean(x * x, axis=-1, keepdims=True)
    return x * lax.rsqrt(ms + EPS) * g


def _ffn_kernel(h_ref, x1_ref, mod_ref, wg_ref, wu_ref, wd_ref, fn_ref, o_ref, *, n_chunks, final):
    h = h_ref[...]
    f = wg_ref.shape[1]
    fc = f // n_chunks
    acc = jnp.zeros(o_ref.shape, F32)
    for c in range(n_chunks):
        sl = slice(c * fc, (c + 1) * fc)
        g = jnp.dot(h, wg_ref[:, sl], preferred_element_type=F32)
        u = jnp.dot(h, wu_ref[:, sl], preferred_element_type=F32)
        a = (g * _sigmoid(g) * u).astype(BF16)
        acc += jnp.dot(a, wd_ref[sl, :], preferred_element_type=F32)
    x2 = x1_ref[...] + mod_ref[5:6, :] * acc
    o_ref[...] = _final_norm(x2, fn_ref[...]) if final else x2


def _ffn_call(h2, x1, mod, wg, wu, wd, fn, n1, s2, final):
    n, d = x1.shape
    f = wg.shape[1]
    tm = FFN_TILE

    def mod_row(t):
        return jnp.where(t * tm < n1, 0, 1 + (t * tm - n1) // s2)

    row = lambda dt: pl.BlockSpec((tm, d), lambda t: (t, 0))
    const = lambda shp: pl.BlockSpec(shp, lambda t: (0,) * len(shp), pipeline_mode=pl.Buffered(1))
    return pl.pallas_call(
        functools.partial(_ffn_kernel, n_chunks=2, final=final),
        out_shape=jax.ShapeDtypeStruct((n, d), F32),
        grid=(n // tm,),
        in_specs=[row(BF16), row(F32), pl.BlockSpec((None, 6, d), lambda t: (mod_row(t), 0, 0)),
                  const((d, f)), const((d, f)), const((f, d)), pl.BlockSpec((1, d), lambda t: (0, 0))],
        out_specs=row(F32),
        compiler_params=_cparams(("arbitrary",)),
        name="ffn_dense",
    )(h2, x1, mod, wg, wu, wd, fn)


def _route_tables(route, n):
    tmx = MOE_TILE
    e = jnp.concatenate([route[:, 0], route[:, 1]]).astype(jnp.int32)
    oh = (e[:, None] == jnp.arange(N_EXPERTS, dtype=jnp.int32)[None, :]).astype(jnp.int32)
    cs = jnp.cumsum(oh, axis=0)
    counts = cs[-1]
    rank = jnp.sum(oh * cs, axis=1) - 1
    padded = ((counts + tmx - 1) // tmx) * tmx
    pend = jnp.cumsum(padded)
    pos = jnp.sum(oh * (pend - padded)[None, :], axis=1) + rank
    n_rows = 2 * n + N_EXPERTS * tmx
    n_tiles = n_rows // tmx
    inv = jnp.full((n_rows,), -1, jnp.int32).at[pos].set(jnp.arange(2 * n, dtype=jnp.int32), unique_indices=True)
    valid = inv >= 0
    src = jnp.concatenate([jnp.where(valid, inv % n, 0), jnp.zeros((2 * tmx,), jnp.int32)])
    dump = 2 * n + jnp.arange(n_rows, dtype=jnp.int32) % (2 * tmx)
    dst = jnp.concatenate([2 * n + tmx + jnp.arange(tmx, dtype=jnp.int32), jnp.where(valid, inv, dump)])
    tile_e = jnp.sum((jnp.arange(n_tiles + 1, dtype=jnp.int32) * tmx)[:, None] >= pend[None, :], axis=1)
    tile_e = jnp.minimum(tile_e, N_EXPERTS - 1).astype(jnp.int32)
    return src.reshape(n_tiles + 2, 1, tmx), dst.reshape(n_tiles + 1, 1, tmx), tile_e


def _moe_routed_kernel(te_ref, src_ref, srcn_ref, dst_ref, h_hbm, wg_ref, wu_ref, wd_ref, yy_hbm,
                       xbuf, ybuf, sem_in, sem_out):
    del te_ref
    tmx = MOE_TILE
    t = pl.program_id(0)
    last = pl.num_programs(0) - 1
    slot = t % 2
    other = 1 - slot

    def gather(idx_ref, s):
        for r in range(tmx):
            pltpu.make_async_copy(h_hbm.at[idx_ref[0, r]], xbuf.at[s, r], sem_in.at[s]).start(priority=r % 2)

    def wait_gather(s):
        pltpu.make_async_copy(h_hbm.at[pl.ds(0, tmx)], xbuf.at[s], sem_in.at[s]).wait()

    def wait_scatter(s):
        pltpu.make_async_copy(ybuf.at[s], yy_hbm.at[pl.ds(0, tmx)], sem_out.at[s]).wait()

    @pl.when(t == 0)
    def _prologue():
        ybuf[...] = jnp.zeros_like(ybuf)
        gather(src_ref, 0)

    @pl.when(t >= 1)
    def _free_ybuf():
        wait_scatter(slot)

    wait_gather(slot)
    gather(srcn_ref, other)
    for r in range(tmx):
        pltpu.make_async_copy(ybuf.at[other, r], yy_hbm.at[dst_ref[0, r]], sem_out.at[other]).start(priority=r % 2)
    x = _load_token_tiles(xbuf.at[slot]).astype(BF16)
    g = jnp.dot(x, wg_ref[...], preferred_element_type=F32)
    u = jnp.dot(x, wu_ref[...], preferred_element_type=F32)
    a = (g * _sigmoid(g) * u).astype(BF16)
    _store_token_tiles(ybuf.at[slot], jnp.dot(a, wd_ref[...], preferred_element_type=F32))

    @pl.when(t == last)
    def _drain():
        wait_gather(other)
        wait_scatter(other)


def _moe_routed_call(h2, src, dst, tile_e, wg, wu, wd):
    n, nj, _ = h2.shape
    d = nj * LANES
    ne, _, f = wg.shape
    tmx = MOE_TILE
    n_tiles = tile_e.shape[0] - 1
    smem_blk = lambda fn_: pl.BlockSpec((None, 1, tmx), fn_, memory_space=pltpu.SMEM)
    grid_spec = pltpu.PrefetchScalarGridSpec(
        num_scalar_prefetch=1,
        grid=(n_tiles + 1,),
        in_specs=[
            smem_blk(lambda t, te: (t, 0, 0)),
            smem_blk(lambda t, te: (t + 1, 0, 0)),
            smem_blk(lambda t, te: (t, 0, 0)),
            pl.BlockSpec(memory_space=pl.ANY),
            pl.BlockSpec((None, d, f), lambda t, te: (te[t], 0, 0)),
            pl.BlockSpec((None, d, f), lambda t, te: (te[t], 0, 0)),
            pl.BlockSpec((None, f, d), lambda t, te: (te[t], 0, 0)),
        ],
        out_specs=pl.BlockSpec(memory_space=pl.ANY),
        scratch_shapes=[pltpu.VMEM((2, tmx, nj, LANES), F32), pltpu.VMEM((2, tmx, nj, LANES), F32),
                        pltpu.SemaphoreType.DMA((2,)), pltpu.SemaphoreType.DMA((2,))],
    )
    return pl.pallas_call(
        _moe_routed_kernel,
        out_shape=jax.ShapeDtypeStruct((2 * n + 2 * tmx, nj, LANES), F32),
        grid_spec=grid_spec,
        compiler_params=_cparams(("arbitrary",)),
        name="moe_experts",
    )(tile_e, src, src, dst, h2, wg, wu, wd)


def _moe_combine_kernel(x1_ref, mod_ref, route_ref, y1_ref, y2_ref, fn_ref, o_ref, *, final):
    w1 = route_ref[:, 2:3]
    w2 = route_ref[:, 3:4]
    y = w1 * _load_token_tiles(y1_ref) + w2 * _load_token_tiles(y2_ref)
    x2 = x1_ref[...] + mod_ref[5:6, :] * y
    o_ref[...] = _final_norm(x2, fn_ref[...]) if final else x2


def _moe_combine_call(x1, mod, route, yy, fn, n1, s2, final):
    n, d = x1.shape
    tm = FFN_TILE
    nt = n // tm

    def mod_row(t):
        return jnp.where(t * tm < n1, 0, 1 + (t * tm - n1) // s2)

    row = lambda w_: pl.BlockSpec((tm, w_), lambda t: (t, 0))
    return pl.pallas_call(
        functools.partial(_moe_combine_kernel, final=final),
        out_shape=jax.ShapeDtypeStruct((n, d), F32),
        grid=(nt,),
        in_specs=[row(d), pl.BlockSpec((None, 6, d), lambda t: (mod_row(t), 0, 0)), row(LANES),
                  pl.BlockSpec((tm, d // LANES, LANES), lambda t: (t, 0, 0)),
                  pl.BlockSpec((tm, d // LANES, LANES), lambda t: (nt + t, 0, 0)),
                  pl.BlockSpec((1, d), lambda t: (0, 0))],
        out_specs=row(d),
        compiler_params=_cparams(("arbitrary",)),
        name="moe_combine",
    )(x1, mod, route, yy, yy, fn)


def _pair_q_heads(a, axis):
    g = N_HEADS // N_KV_HEADS
    shp = a.shape
    a = a.reshape(shp[:axis] + (N_KV_HEADS, g, HEAD_DIM) + shp[axis + 1:])
    a = jnp.swapaxes(a, axis, axis + 1)
    return a.reshape(shp)


def _prep_in_weights(w_in, b_in):
    g0 = ATT_Q_W + 2 * ATT_KV_W + 4 * ML_W
    pad = N_IN_PAD - w_in.shape[-1]

    def reorder(a):
        parts = [_pair_q_heads(a[..., :ATT_Q_W], a.ndim - 1), a[..., ATT_Q_W:g0], a[..., g0 + N_GATES:],
                 a[..., g0:g0 + N_GATES], jnp.zeros(a.shape[:-1] + (pad,), a.dtype)]
        return jnp.concatenate(parts, axis=-1)

    return reorder(w_in).astype(BF16), reorder(b_in)[:, None, :]


def _rope_tables(s2, tm):
    pos = jnp.arange(s2)
    rowp = (pos // GRID_W).astype(F32)
    colp = (pos % GRID_W).astype(F32)
    quarter = HEAD_DIM // 4
    inv = ROPE_BASE ** (-jnp.arange(quarter, dtype=F32) / quarter)
    ang_r = rowp[:, None] * inv
    ang_c = colp[:, None] * inv
    cos_h = jnp.concatenate([jnp.cos(ang_r), jnp.cos(ang_r), jnp.cos(ang_c), jnp.cos(ang_c)], axis=1)
    sin_h = jnp.concatenate([-jnp.sin(ang_r), jnp.sin(ang_r), -jnp.sin(ang_c), jnp.sin(ang_c)], axis=1)
    cos_t = jnp.concatenate([cos_h, cos_h], axis=1)
    sin_t = jnp.concatenate([sin_h, sin_h], axis=1)
    cos_t = jnp.concatenate([jnp.ones((tm, LANES), F32), cos_t], axis=0)
    sin_t = jnp.concatenate([jnp.zeros((tm, LANES), F32), sin_t], axis=0)
    return cos_t, sin_t


def _pair_blockdiag(m):
    b, nd, h, d, _ = m.shape
    m = m.reshape(b, nd, h // 2, 2, d, d)
    z = jnp.zeros_like(m[:, :, :, 0])
    top = jnp.concatenate([m[:, :, :, 0], z], axis=-1)
    bot = jnp.concatenate([z, m[:, :, :, 1]], axis=-1)
    return jnp.concatenate([top, bot], axis=-2)


def _pair_unblock(m):
    b, nd, p, _, _ = m.shape
    a = m[:, :, :, :HALF, :HALF]
    c = m[:, :, :, HALF:, HALF:]
    return jnp.stack([a, c], axis=3).reshape(b, nd, 2 * p, HALF, HALF)


def kernel(x_prompt, x_sample, c, cache_attn_k, cache_attn_v, state_mlstm_c, state_mlstm_n, state_mlstm_m,
           state_ret_s, c_ctx, w_ada, b_ada, norm_mix, norm_ffn, w_in, b_in, q_norm, k_norm, ret_decay,
           w_branch, w_out, ffn_w_gate, ffn_w_up, ffn_w_down, moe_router, moe_w_gate, moe_w_up, moe_w_down,
           final_norm):
    b1, s1, d = x_prompt.shape
    b2, s2, _ = x_sample.shape
    depth = w_in.shape[0]
    n1, n2 = b1 * s1, b2 * s2
    assert s1 % TOK_TILE == 0 and s2 % FFN_TILE == 0 and n1 % FFN_TILE == 0 and b2 + 1 <= 8

    x = jnp.concatenate([x_prompt.reshape(n1, d), x_sample.reshape(n2, d)], axis=0)
    c8 = jnp.concatenate([c_ctx[None, :], c, jnp.zeros((8 - 1 - b2, d), F32)], axis=0)
    mod_all = _ada_call(c8, w_ada, b_ada).reshape(depth, 8, 6, d)

    w_in_p, b_in_p = _prep_in_weights(w_in, b_in)
    cos_t, sin_t = _rope_tables(s2, TOK_TILE)
    wb = jnp.concatenate([_pair_q_heads(w_branch[:, :ATT_Q_W], 1), w_branch[:, ATT_Q_W:]], axis=1).astype(BF16)
    wo = w_out.astype(BF16)
    fn = final_norm[None, :]

    zeros_like_state = lambda shp: jnp.zeros((b1,) + shp, F32)
    states = []
    for l in range(depth):
        mod = mod_all[l]
        qg = jnp.tile(q_norm[l], 2)[None, :]
        kg = jnp.tile(k_norm[l], 2)[None, :]
        q, k, v, kb, vb, mqkv, rqkv, mo, rg, bg, gcol, grow = _inproj_call(
            x, mod, norm_mix[l][None, :], w_in_p[l], b_in_p[l], qg, kg, cos_t, sin_t, n1, s2)

        vt1 = jnp.swapaxes(vb[:n1].reshape(b1, s1, ATT_KV_W), 1, 2)
        att1 = _attn_call(q, kb[:n1].reshape(b1, s1, ATT_KV_W), vt1, 0, s1)
        k2 = jnp.concatenate([cache_attn_k[:, l].reshape(b2, -1, ATT_KV_W).astype(BF16),
                              kb[n1:].reshape(b2, s2, ATT_KV_W)], axis=1)
        v2 = jnp.concatenate([cache_attn_v[:, l].reshape(b2, -1, ATT_KV_W).astype(BF16),
                              vb[n1:].reshape(b2, s2, ATT_KV_W)], axis=1)
        att2 = _attn_call(q, k2, jnp.swapaxes(v2, 1, 2), n1, s2)

        rd = jnp.broadcast_to(ret_decay[l].reshape(2 * RET_HEADS, 1), (2 * RET_HEADS, LANES))
        r1 = _scan_call(mqkv, rqkv, gcol, grow, rd,
                        zeros_like_state((2, 2, LANES, LANES)), zeros_like_state((2, 2, 1, LANES)),
                        zeros_like_state((2, 2, 1, LANES)), zeros_like_state((2, 2, LANES, LANES)), 0, s1)
        c0 = _pair_blockdiag(state_mlstm_c[:, l])
        n0 = state_mlstm_n[:, l].reshape(b2, 2, 2, 1, LANES)
        m0 = jnp.repeat(state_mlstm_m[:, l], HALF, axis=-1).reshape(b2, 2, 2, 1, LANES)
        s0 = _pair_blockdiag(state_ret_s[:, l])
        r2 = _scan_call(mqkv, rqkv, gcol, grow, rd, c0, n0, m0, s0, n1, s2)
        states.append((k[:n1].reshape(b1, s1, N_KV_HEADS, HEAD_DIM), v[:n1].reshape(b1, s1, N_KV_HEADS, HEAD_DIM),
                       _pair_unblock(r1[4]), r1[5].reshape(b1, 2, ML_HEADS, HEAD_DIM), r1[6][:, :, :, 0, ::HALF].reshape(b1, 2, ML_HEADS),
                       _pair_unblock(r1[7])))

        moe = l % 2 == 1
        jj = l // 2
        router = jnp.pad(moe_router[jj], ((0, 0), (0, LANES - N_EXPERTS))) if moe else None
        outs = _merge_call(x, mod, att1, att2, r1[:4], r2[:4], mo, rg, bg, wb[l], wo[l], norm_ffn[l][None, :],
                           router, n1, s2)
        final = l == depth - 1
        if moe:
            x1, h2, route = outs
            src, dst, tile_e = _route_tables(route, n1 + n2)
            yy = _moe_routed_call(h2, src, dst, tile_e, moe_w_gate[jj].astype(BF16), moe_w_up[jj].astype(BF16),
                                  moe_w_down[jj].astype(BF16))
            x = _moe_combine_call(x1, mod, route, yy, fn, n1, s2, final)
        else:
            x1, h2 = outs
            x = _ffn_call(h2, x1, mod, ffn_w_gate[jj].astype(BF16), ffn_w_up[jj].astype(BF16),
                          ffn_w_down[jj].astype(BF16), fn, n1, s2, final)

    y_prompt = x[:n1].reshape(b1, s1, d)
    y_sample = x[n1:].reshape(b2, s2, d)
    stack = lambda i: jnp.stack([s[i] for s in states], axis=1)
    return (y_prompt, y_sample, stack(0), stack(1), stack(2), stack(3), stack(4), stack(5))
```

```python
import functools

import jax
import jax.numpy as jnp
from jax import lax
from jax.experimental import pallas as pl
from jax.experimental.pallas import tpu as pltpu

F32 = jnp.float32
BF16 = jnp.bfloat16

N_HEADS = 8
N_KV_HEADS = 2
HEAD_DIM = 64
ML_HEADS = 4
RET_HEADS = 4
GRID_W = 64
CHUNK = 128
ROPE_BASE = 10000.0
EPS = 1e-6
N_EXPERTS = 8
LANES = 128
HALF = 64

ATT_Q_W = N_HEADS * HEAD_DIM
ATT_KV_W = N_KV_HEADS * HEAD_DIM
ML_W = ML_HEADS * HEAD_DIM
RET_W = RET_HEADS * HEAD_DIM
N_GATES = 4 * ML_HEADS

TOK_TILE = 256
SCAN_SEQS = 2
FFN_TILE = 512
MOE_TILE = 512
ATT_Q_TILE = 256
ATT_K_TILE = 512
ATT_SHORT_SEQS = 4
LOG2E = 1.4426950408889634
Q_SCALE = 0.125 * LOG2E
VMEM_LIMIT = 56 * 1024 * 1024

O_AQ, O_AK, O_AV = 0, 512, 640
O_MQ, O_MK, O_MV, O_MO = 768, 1024, 1280, 1536
O_RQ, O_RK, O_RV, O_RG = 1792, 2048, 2304, 2560
O_BG = 2816
O_MG = 5888
N_IN_PAD = 6016


def _cparams(sem, vmem=VMEM_LIMIT):
    return pltpu.CompilerParams(dimension_semantics=sem, vmem_limit_bytes=vmem)


def _lane_iota(shape=(1, LANES)):
    return lax.broadcasted_iota(jnp.int32, shape, len(shape) - 1)


def _head_mean_matrix():
    r = lax.broadcasted_iota(jnp.int32, (LANES, LANES), 0) >> 6
    c = lax.broadcasted_iota(jnp.int32, (LANES, LANES), 1) >> 6
    return jnp.where(r == c, 1.0 / HALF, 0.0).astype(BF16)


def _head_rms(xs, mean_mat):
    t = xs[0].shape[0]
    sq = jnp.concatenate([x * x for x in xs], axis=0)
    hi = sq.astype(BF16)
    lo = (sq - hi.astype(F32)).astype(BF16)
    ms = jnp.dot(jnp.concatenate([hi, lo], axis=0), mean_mat, preferred_element_type=F32)
    ms = ms[:len(xs) * t] + ms[len(xs) * t:]
    return [x * lax.rsqrt(ms[i * t:(i + 1) * t] + EPS) for i, x in enumerate(xs)]


def _rope(y, cos, sin, lane):
    up = pltpu.roll(y, LANES - 16, 1)
    dn = pltpu.roll(y, 16, 1)
    partner = jnp.where((lane & 31) < 16, up, dn)
    return y * cos + partner * sin


def _log_sigmoid(x):
    return jnp.minimum(x, 0.0) - jnp.log1p(jnp.exp(-jnp.abs(x)))


def _sigmoid(x):
    return 1.0 / (1.0 + jnp.exp(-x))


def _store_token_tiles(ref, x):
    for j in range(x.shape[1] // LANES):
        ref[:, j, :] = x[:, j * LANES:(j + 1) * LANES]


def _load_token_tiles(ref):
    return jnp.concatenate([ref[:, j, :] for j in range(ref.shape[1])], axis=1)


def _split3(x):
    h = x.astype(BF16)
    r = x - h.astype(F32)
    m = r.astype(BF16)
    l = (r - m.astype(F32)).astype(BF16)
    return h, m, l


def _ada_kernel(c_ref, w_ref, b_ref, o_ref):
    c = c_ref[...]
    a = (c * _sigmoid(c)).astype(BF16)
    o_ref[...] = jnp.dot(a, w_ref[...].astype(BF16), preferred_element_type=F32) + b_ref[...]


def _ada_call(c8, w_ada, b_ada):
    depth, d, n6 = w_ada.shape
    tn = 1536
    return pl.pallas_call(
        _ada_kernel,
        out_shape=jax.ShapeDtypeStruct((depth, 8, n6), F32),
        grid=(depth, n6 // tn),
        in_specs=[
            pl.BlockSpec((8, d), lambda l, j: (0, 0)),
            pl.BlockSpec((None, d, tn), lambda l, j: (l, 0, j)),
            pl.BlockSpec((None, 1, tn), lambda l, j: (l, 0, j)),
        ],
        out_specs=pl.BlockSpec((None, 8, tn), lambda l, j: (l, 0, j)),
        compiler_params=_cparams(("arbitrary", "arbitrary")),
        name="ada_mod",
    )(c8, w_ada, b_ada.reshape(depth, 1, n6))


def _inproj_kernel(x_ref, mod_ref, gn_ref, w_ref, b_ref, qg_ref, kg_ref, cos_ref, sin_ref,
                   q_ref, k_ref, v_ref, kb_ref, vb_ref, mqkv_ref, rqkv_ref, mo_ref, rg_ref, bg_ref, gcol_ref,
                   grow_ref):
    x = x_ref[...]
    ms = jnp.mean(x * x, axis=-1, keepdims=True)
    h = x * lax.rsqrt(ms + EPS) * gn_ref[...]
    h = h * (1.0 + mod_ref[1:2, :]) + mod_ref[0:1, :]
    hb = h.astype(BF16)

    def seg(a, b):
        return jnp.dot(hb, w_ref[:, a:b], preferred_element_type=F32) + b_ref[:, a:b]

    lane = _lane_iota()
    mean_mat = _head_mean_matrix()
    cos = cos_ref[...]
    sin = sin_ref[...]
    n_q = ATT_Q_W // LANES
    za = seg(O_AQ, O_MQ)
    normed = _head_rms([za[:, s * LANES:(s + 1) * LANES] for s in range(n_q + 1)], mean_mat)
    for s in range(n_q):
        q_ref[s] = (_rope(normed[s] * qg_ref[...], cos, sin, lane) * Q_SCALE).astype(BF16)
    y = _rope(normed[n_q] * kg_ref[...], cos, sin, lane)
    k_ref[...] = y
    kb_ref[...] = y.astype(BF16)
    y = za[:, O_AV:O_AV + LANES]
    v_ref[...] = y
    vb_ref[...] = y.astype(BF16)

    zm = seg(O_MQ, O_RQ)
    mqkv_ref[:, 0:ML_W] = zm[:, 0:ML_W].astype(BF16)
    mqkv_ref[:, ML_W:2 * ML_W] = (zm[:, ML_W:2 * ML_W] * 0.125).astype(BF16)
    mqkv_ref[:, 2 * ML_W:3 * ML_W] = zm[:, 2 * ML_W:3 * ML_W].astype(BF16)
    mo_ref[...] = zm[:, 3 * ML_W:]

    zr = seg(O_RQ, O_BG)
    for s in range(RET_W // LANES):
        sl = slice(s * LANES, (s + 1) * LANES)
        rqkv_ref[:, sl] = _rope(zr[:, sl], cos, sin, lane).astype(BF16)
        slk = slice(RET_W + s * LANES, RET_W + (s + 1) * LANES)
        rqkv_ref[:, slk] = (_rope(zr[:, slk], cos, sin, lane) * 0.125).astype(BF16)
    rqkv_ref[:, 2 * RET_W:3 * RET_W] = zr[:, 2 * RET_W:3 * RET_W].astype(BF16)
    rg_ref[...] = zr[:, 3 * RET_W:]

    zg = seg(O_BG, N_IN_PAD)
    bg_ref[...] = zg[:, :O_MG - O_BG]

    g = zg[:, O_MG - O_BG:]
    is_f = ((lane >> 2) & 1) == 1
    g = jnp.where(is_f, _log_sigmoid(g), g)
    gcol_ref[...] = g
    grow_ref[...] = g.T[0:N_GATES, :]


def _inproj_call(x, mod, gn, w, b, qg, kg, cos_t, sin_t, n1, s2):
    n, d = x.shape
    tm = TOK_TILE
    n1t = n1 // tm
    t2 = s2 // tm

    def mod_row(t):
        return jnp.where(t < n1t, 0, 1 + (t - n1t) // t2)

    def tab_row(t):
        return jnp.where(t < n1t, 0, 1 + (t - n1t) % t2)

    row = lambda w_: pl.BlockSpec((tm, w_), lambda t: (t, 0))
    const = lambda shp: pl.BlockSpec(shp, lambda t: (0,) * len(shp))
    n_slabs = ATT_Q_W // LANES
    out_shape = [
        jax.ShapeDtypeStruct((n_slabs, n, LANES), BF16),
        jax.ShapeDtypeStruct((n, ATT_KV_W), F32),
        jax.ShapeDtypeStruct((n, ATT_KV_W), F32),
        jax.ShapeDtypeStruct((n, ATT_KV_W), BF16),
        jax.ShapeDtypeStruct((n, ATT_KV_W), BF16),
        jax.ShapeDtypeStruct((n, 3 * ML_W), BF16),
        jax.ShapeDtypeStruct((n, 3 * RET_W), BF16),
        jax.ShapeDtypeStruct((n, ML_W), F32),
        jax.ShapeDtypeStruct((n, RET_W), F32),
        jax.ShapeDtypeStruct((n, 3 * d), F32),
        jax.ShapeDtypeStruct((n, LANES), F32),
        jax.ShapeDtypeStruct((N_GATES, n), F32),
    ]
    out_specs = [pl.BlockSpec((n_slabs, tm, LANES), lambda t: (0, t, 0)), row(ATT_KV_W), row(ATT_KV_W),
                 row(ATT_KV_W), row(ATT_KV_W), row(3 * ML_W), row(3 * RET_W), row(ML_W),
                 row(RET_W), row(3 * d), row(LANES), pl.BlockSpec((N_GATES, tm), lambda t: (0, t))]
    return pl.pallas_call(
        _inproj_kernel,
        out_shape=out_shape,
        grid=(n // tm,),
        in_specs=[
            row(d),
            pl.BlockSpec((None, 6, d), lambda t: (mod_row(t), 0, 0)),
            const((1, d)),
            const((d, N_IN_PAD)),
            const((1, N_IN_PAD)),
            const((1, LANES)),
            const((1, LANES)),
            pl.BlockSpec((tm, LANES), lambda t: (tab_row(t), 0)),
            pl.BlockSpec((tm, LANES), lambda t: (tab_row(t), 0)),
        ],
        out_specs=out_specs,
        compiler_params=_cparams(("arbitrary",)),
        name="in_proj",
    )(x, mod, gn, w, b, qg, kg, cos_t, sin_t)


def _attn_kernel(q_ref, k_ref, vt_ref, o_ref, s_sc, *, tk, tq):
    nb = k_ref.shape[0]
    nk = k_ref.shape[1] // tk
    row_lo = lax.broadcasted_iota(jnp.int32, (LANES, 1), 0) < HALF

    for bi in range(nb):
        for j in range(q_ref.shape[0]):
            qt = q_ref[j, bi * tq:(bi + 1) * tq, :].astype(F32).T.astype(BF16)
            outs = []
            for half in range(2):
                keep = row_lo if half == 0 else jnp.logical_not(row_lo)
                qm = jnp.where(keep, qt, jnp.zeros_like(qt))
                mx = None
                for c in range(nk):
                    s = jnp.dot(k_ref[bi, c * tk:(c + 1) * tk, :], qm, preferred_element_type=F32)
                    s_sc[half, c * tk:(c + 1) * tk, :] = s
                    cm = jnp.max(s, axis=0, keepdims=True)
                    mx = cm if mx is None else jnp.maximum(mx, cm)
                l = jnp.zeros((1, tq), F32)
                acc = jnp.zeros((LANES, tq), F32)
                for c in range(nk):
                    p = jnp.exp2(s_sc[half, c * tk:(c + 1) * tk, :] - mx)
                    l += jnp.sum(p, axis=0, keepdims=True)
                    acc += jnp.dot(vt_ref[bi, :, c * tk:(c + 1) * tk], p.astype(BF16), preferred_element_type=F32)
                outs.append(acc * (1.0 / l))
            o_ref[j, bi * tq:(bi + 1) * tq, :] = jnp.where(row_lo, outs[0], outs[1]).T.astype(o_ref.dtype)


def _attn_call(q, k, vt, tok_off, sq):
    n_slabs = q.shape[0]
    b, sk, _ = k.shape
    tq = min(ATT_Q_TILE, sq)
    tk = min(ATT_K_TILE, sk)
    nq = sq // tq
    nb = ATT_SHORT_SEQS if (nq == 1 and b % ATT_SHORT_SEQS == 0) else 1
    off = tok_off // (nb * tq)
    return pl.pallas_call(
        functools.partial(_attn_kernel, tk=tk, tq=tq),
        out_shape=jax.ShapeDtypeStruct((n_slabs, b * sq, LANES), BF16),
        grid=(b // nb, nq),
        in_specs=[
            pl.BlockSpec((n_slabs, nb * tq, LANES), lambda i, j: (0, off + i * nq + j, 0)),
            pl.BlockSpec((nb, sk, ATT_KV_W), lambda i, j: (i, 0, 0)),
            pl.BlockSpec((nb, ATT_KV_W, sk), lambda i, j: (i, 0, 0)),
        ],
        out_specs=pl.BlockSpec((n_slabs, nb * tq, LANES), lambda i, j: (0, i * nq + j, 0)),
        scratch_shapes=[pltpu.VMEM((2, sk, tq), F32)],
        compiler_params=_cparams(("arbitrary", "arbitrary")),
        name="attention",
    )(q, k, vt)


def _scan_kernel(*refs, nb):
    L = CHUNK
    seq_in = refs[:8 * nb]
    rd_ref, c0_ref, n0_ref, m0_ref, s0_ref = refs[8 * nb:8 * nb + 5]
    o0 = 8 * nb + 5
    hf_ref, hb_ref, of_ref, ob_ref, cout_ref, nout_ref, mout_ref, sout_ref = refs[o0:o0 + 8]
    c_all, n_all, m_all, s_all, dec_st, qdec_st, kdec_st, cdec_st = refs[o0 + 8:]
    b_idx = pl.program_id(0)
    j = pl.program_id(1)
    nc = pl.num_programs(1)
    lane = _lane_iota()
    lo = lane < HALF
    row_i = lax.broadcasted_iota(jnp.int32, (L, L), 0)
    col_i = lax.broadcasted_iota(jnp.int32, (L, L), 1)
    blockmask = (row_i >> 6) == (col_i >> 6)
    causal = (row_i >= col_i, row_i <= col_i)

    @pl.when(jnp.logical_and(b_idx == 0, j == 0))
    def _init_tables():
        lg = _log_sigmoid(rd_ref[...])
        pos = lax.broadcasted_iota(jnp.int32, (L, 1), 0).astype(F32)
        diff = (row_i - col_i).astype(F32)
        for d in range(2):
            sd = diff if d == 0 else -diff
            for p in range(2):
                qd, kd, cd = [], [], []
                for e in range(2):
                    r = d * RET_HEADS + 2 * p + e
                    g = lg[r:r + 1, 0:1]
                    dec_st[d, p, e * L:(e + 1) * L, :] = jnp.where(sd >= 0, jnp.exp(g * jnp.maximum(sd, 0.0)), 0.0)
                    if d == 0:
                        qd.append(jnp.exp(g * (pos + 1.0)))
                        kd.append(jnp.exp(g * (L - 1.0 - pos)))
                    else:
                        qd.append(jnp.exp(g * (L - pos)))
                        kd.append(jnp.exp(g * pos))
                    cd.append(jnp.exp(g * float(L)))
                qdec_st[d, p] = jnp.where(lo, qd[0], qd[1])
                kdec_st[d, p] = jnp.where(lo, kd[0], kd[1])
                cdec_st[d, p] = jnp.where(lo, cd[0], cd[1])

    @pl.when(j == 0)
    def _load_state():
        c_all[...] = c0_ref[...]
        n_all[...] = n0_ref[...]
        m_all[...] = m0_ref[...]
        s_all[...] = s0_ref[...]

    tri = (row_i >= col_i).astype(BF16)
    triu = (row_i <= col_i).astype(BF16)

    def cumsums(gc_ref, gr_ref, d):
        a_col, a_row = (tri, triu) if d == 0 else (triu, tri)
        col3 = jnp.dot(a_col, jnp.concatenate(_split3(gc_ref[...]), axis=1), preferred_element_type=F32)
        row3 = jnp.dot(jnp.concatenate(_split3(gr_ref[...]), axis=0), a_row, preferred_element_type=F32)
        col = col3[:, 0:LANES] + col3[:, LANES:2 * LANES] + col3[:, 2 * LANES:]
        rowv = row3[0:N_GATES] + row3[N_GATES:2 * N_GATES] + row3[2 * N_GATES:]
        return col, rowv

    pair = lambda x0, x1: jnp.where(lo, x0, x1)
    lo2 = (_lane_iota((1, 2 * LANES)) & (LANES - 1)) < HALF

    keys = [(bb, d, p) for bb in range(nb) for d in range(2) for p in range(2)]
    c_old = {k: c_all[k] for k in keys}
    n_old = {k: n_all[k] for k in keys}
    m_old = {k: m_all[k] for k in keys}
    s_old = {k: s_all[k] for k in keys}
    c_new, n_new, m_new_st, s_new = {}, {}, {}, {}

    for bb in range(nb):
        mf_ref, mb_ref, rf_ref, rb_ref, gcf_ref, gcb_ref, grf_ref, grb_ref = seq_in[8 * bb:8 * bb + 8]
        for d, (m_ref, r_ref, gc_ref, gr_ref, h_out, o_out) in enumerate(
                ((mf_ref, rf_ref, gcf_ref, grf_ref, hf_ref, of_ref),
                 (mb_ref, rb_ref, gcb_ref, grb_ref, hb_ref, ob_ref))):
            gcol = gc_ref[...]
            grow = gr_ref[...]
            cum_col, cum_row = cumsums(gc_ref, gr_ref, d)
            gi = 2 * ML_HEADS * d
            gf = gi + ML_HEADS
            last = L - 1 if d == 0 else 0
            for p in range(2):
                sl = slice(p * LANES, (p + 1) * LANES)
                mq, mk, mv = (m_ref[:, i * ML_W + p * LANES:i * ML_W + (p + 1) * LANES] for i in range(3))
                rq, rk, rv = (r_ref[:, i * RET_W + p * LANES:i * RET_W + (p + 1) * LANES] for i in range(3))
                c2 = c_old[bb, d, p]
                n2 = n_old[bb, d, p]
                m2 = m_old[bb, d, p]
                s2 = s_old[bb, d, p]
                zero = jnp.zeros_like(mq)
                zero_st = jnp.zeros((LANES, LANES), BF16)
                bdiag = lambda x0, x1, z: jnp.concatenate(
                    [jnp.concatenate([x0, z], axis=1), jnp.concatenate([z, x1], axis=1)], axis=0)
                q_cat = jnp.concatenate([mq, rq], axis=1)
                q_st = jnp.dot(q_cat, bdiag(c2.astype(BF16), s2.astype(BF16), zero_st), preferred_element_type=F32)
                q_c = q_st[:, :LANES]
                q_s = q_st[:, LANES:] * qdec_st[d, p]
                q_n = mq.astype(F32) * n2
                hds = (2 * p, 2 * p + 1)
                i_cols = [gcol[:, gi + hd:gi + hd + 1] for hd in hds]
                b_cols = [cum_col[:, gf + hd:gf + hd + 1] for hd in hds]
                b2 = pair(*b_cols)
                i2 = pair(*i_cols)
                a2 = b2 + m2
                dms = []
                for e, hd in enumerate(hds):
                    i_row = grow[gi + hd:gi + hd + 1, :]
                    b_row = cum_row[gf + hd:gf + hd + 1, :]
                    dms.append(jnp.where(causal[d], b_cols[e] - b_row + i_row, -jnp.inf))
                mt2 = jnp.maximum(a2, pair(*[jnp.max(dm, axis=-1, keepdims=True) for dm in dms]))
                wa2 = jnp.exp(a2 - mt2)
                qs = jnp.concatenate([jnp.where(lo2, q_cat, jnp.zeros_like(q_cat)),
                                      jnp.where(lo2, jnp.zeros_like(q_cat), q_cat)], axis=0)
                scores = lax.dot_general(qs, bdiag(mk, rk, zero), (((1,), (1,)), ((), ())),
                                         preferred_element_type=F32)
                wd = jnp.concatenate([jnp.exp(dms[e] - mt2[:, e * HALF:e * HALF + 1]) for e in range(2)], axis=0)
                s_m = scores[:, :L] * wd
                s_r = scores[:, L:] * dec_st[d, p]
                sv = jnp.dot(jnp.concatenate([s_m, s_r], axis=1).astype(BF16), bdiag(mv, rv, zero),
                             preferred_element_type=F32)
                rs = jnp.sum(s_m, axis=-1, keepdims=True)
                qns = [jnp.sum(jnp.where(lo if e == 0 else jnp.logical_not(lo), q_n, 0.0), axis=-1, keepdims=True)
                       for e in range(2)]
                den2 = pair(rs[:L], rs[L:]) + wa2 * pair(*qns)
                dd2 = jnp.maximum(jnp.abs(den2), jnp.exp(-mt2))
                h_out[bb, :, sl] = (pair(sv[:L, :LANES], sv[L:, :LANES]) + wa2 * q_c) / dd2
                o_out[bb, :, sl] = pair(sv[:L, LANES:], sv[L:, LANES:]) + q_s
                b_last = b2[last:last + 1, :]
                g2 = b_last - b2 + i2
                m_new = jnp.maximum(b_last + m2, jnp.max(g2, axis=0, keepdims=True))
                wc2 = jnp.exp(b_last + m2 - m_new)
                kw_m = mk.astype(F32) * jnp.exp(g2 - m_new)
                kw_r = rk.astype(F32) * kdec_st[d, p]
                upd = lax.dot_general(jnp.concatenate([kw_m, kw_r], axis=1).astype(BF16),
                                      jnp.concatenate([mv, rv], axis=1), (((0,), (0,)), ((), ())),
                                      preferred_element_type=F32)
                c_new[bb, d, p] = jnp.where(blockmask, wc2 * c2 + upd[:LANES, :LANES], 0.0)
                n_new[bb, d, p] = wc2 * n2 + jnp.sum(kw_m, axis=0, keepdims=True)
                m_new_st[bb, d, p] = m_new
                s_new[bb, d, p] = jnp.where(blockmask, cdec_st[d, p] * s2 + upd[LANES:, LANES:], 0.0)

    for k in keys:
        c_all[k] = c_new[k]
        n_all[k] = n_new[k]
        m_all[k] = m_new_st[k]
        s_all[k] = s_new[k]

    @pl.when(j == nc - 1)
    def _store_state():
        cout_ref[...] = c_all[...]
        nout_ref[...] = n_all[...]
        mout_ref[...] = m_all[...]
        sout_ref[...] = s_all[...]


def _scan_call(mqkv, rqkv, gcol, grow, rd, c0, n0, m0, s0, tok_off, s):
    b = c0.shape[0]
    nb = SCAN_SEQS
    L = CHUNK
    nc = s // L
    off = tok_off // L
    fwd = lambda w_, bb: pl.BlockSpec((L, w_), lambda i, j: (off + (i * nb + bb) * nc + j, 0))
    bwd = lambda w_, bb: pl.BlockSpec((L, w_), lambda i, j: (off + (i * nb + bb) * nc + nc - 1 - j, 0))
    st = lambda shp: pl.BlockSpec((nb,) + shp, lambda i, j: (i,) + (0,) * len(shp))
    pair_mat = (2, 2, LANES, LANES)
    pair_vec = (2, 2, 1, LANES)
    seq_in_specs, seq_args = [], []
    for bb in range(nb):
        seq_in_specs += [
            fwd(3 * ML_W, bb), bwd(3 * ML_W, bb), fwd(3 * RET_W, bb), bwd(3 * RET_W, bb),
            fwd(LANES, bb), bwd(LANES, bb),
            pl.BlockSpec((N_GATES, L), lambda i, j, bb=bb: (0, off + (i * nb + bb) * nc + j)),
            pl.BlockSpec((N_GATES, L), lambda i, j, bb=bb: (0, off + (i * nb + bb) * nc + nc - 1 - j)),
        ]
        seq_args += [mqkv, mqkv, rqkv, rqkv, gcol, gcol, grow, grow]
    seq_out_shape = [jax.ShapeDtypeStruct((b, s, w_), F32) for w_ in (ML_W, ML_W, RET_W, RET_W)]
    seq_out_specs = [pl.BlockSpec((nb, L, ML_W), lambda i, j: (i, j, 0)),
                     pl.BlockSpec((nb, L, ML_W), lambda i, j: (i, nc - 1 - j, 0)),
                     pl.BlockSpec((nb, L, RET_W), lambda i, j: (i, j, 0)),
                     pl.BlockSpec((nb, L, RET_W), lambda i, j: (i, nc - 1 - j, 0))]
    state_shape = [
        jax.ShapeDtypeStruct((b,) + pair_mat, F32), jax.ShapeDtypeStruct((b,) + pair_vec, F32),
        jax.ShapeDtypeStruct((b,) + pair_vec, F32), jax.ShapeDtypeStruct((b,) + pair_mat, F32),
    ]
    state_specs = [st(pair_mat), st(pair_vec), st(pair_vec), st(pair_mat)]
    outs = pl.pallas_call(
        functools.partial(_scan_kernel, nb=nb),
        out_shape=seq_out_shape + state_shape,
        grid=(b // nb, nc),
        in_specs=seq_in_specs + [pl.BlockSpec((8, LANES), lambda i, j: (0, 0))] + state_specs,
        out_specs=seq_out_specs + state_specs,
        scratch_shapes=[
            pltpu.VMEM((nb,) + pair_mat, F32), pltpu.VMEM((nb,) + pair_vec, F32),
            pltpu.VMEM((nb,) + pair_vec, F32), pltpu.VMEM((nb,) + pair_mat, F32),
            pltpu.VMEM((2, 2, 2 * L, L), F32), pltpu.VMEM((2, 2, L, LANES), F32),
            pltpu.VMEM((2, 2, L, LANES), F32), pltpu.VMEM((2, 2, 1, LANES), F32),
        ],
        compiler_params=_cparams(("arbitrary", "arbitrary")),
        name="scan_mixers",
    )(*seq_args, rd, c0, n0, m0, s0)
    return [o.reshape(b * s, o.shape[-1]) for o in outs[:4]] + list(outs[4:])


def _merge_kernel(x_ref, mod_ref, att1_ref, att2_ref, hf1_ref, hb1_ref, of1_ref, ob1_ref, hf2_ref, hb2_ref,
                  of2_ref, ob2_ref, mo_ref, rg_ref, bg_ref, wb_ref, wo_ref, gn_ref, *rest, moe, n1_tiles):
    sc1_refs = (hf1_ref, hb1_ref, of1_ref, ob1_ref)
    sc2_refs = (hf2_ref, hb2_ref, of2_ref, ob2_ref)
    if moe:
        router_ref, x1_ref, h2_ref, gates_ref = rest
    else:
        x1_ref, h2_ref = rest
    d = x_ref.shape[1]
    first = pl.program_id(0) < n1_tiles
    pick = lambda r1, r2, idx: jnp.where(first, r1[idx], r2[idx])
    mean_mat = _head_mean_matrix()
    att = jnp.concatenate([pick(att1_ref, att2_ref, s) for s in range(ATT_Q_W // LANES)], axis=1)
    y = jnp.dot(att, wb_ref[0:ATT_Q_W, :], preferred_element_type=F32) * _sigmoid(bg_ref[:, 0:d])
    n_t = ML_W // LANES
    sums = []
    for s in range(n_t):
        sl = (slice(None), slice(s * LANES, (s + 1) * LANES))
        hf, hb, of, ob = (pick(r1, r2, sl) for r1, r2 in zip(sc1_refs, sc2_refs))
        sums += [hf + hb, of + ob]
    normed = _head_rms(sums, mean_mat)
    ml = jnp.concatenate([(normed[2 * s] * _sigmoid(mo_ref[:, s * LANES:(s + 1) * LANES])).astype(BF16)
                          for s in range(n_t)], axis=1)
    rgs = [rg_ref[:, s * LANES:(s + 1) * LANES] for s in range(n_t)]
    ret = jnp.concatenate([(normed[2 * s + 1] * (rgs[s] * _sigmoid(rgs[s]))).astype(BF16)
                           for s in range(n_t)], axis=1)
    y += jnp.dot(ml, wb_ref[ATT_Q_W:ATT_Q_W + ML_W, :], preferred_element_type=F32) * _sigmoid(bg_ref[:, d:2 * d])
    y += jnp.dot(ret, wb_ref[ATT_Q_W + ML_W:, :], preferred_element_type=F32) * _sigmoid(bg_ref[:, 2 * d:3 * d])
    y = jnp.dot(y.astype(BF16), wo_ref[...], preferred_element_type=F32)
    x1 = x_ref[...] + mod_ref[2:3, :] * y
    x1_ref[...] = x1
    ms = jnp.mean(x1 * x1, axis=-1, keepdims=True)
    h2 = x1 * lax.rsqrt(ms + EPS) * gn_ref[...]
    h2 = h2 * (1.0 + mod_ref[4:5, :]) + mod_ref[3:4, :]
    if moe:
        _store_token_tiles(h2_ref, h2)
    else:
        h2_ref[...] = h2.astype(BF16)
    if moe:
        hh = h2.astype(BF16)
        hl = (h2 - hh.astype(F32)).astype(BF16)
        r = router_ref[...]
        rh = r.astype(BF16)
        rl = (r - rh.astype(F32)).astype(BF16)
        tm = h2.shape[0]
        prod = jnp.dot(jnp.concatenate([hh, hl], axis=0), jnp.concatenate([rh, rl], axis=1),
                       preferred_element_type=F32)
        logits = (prod[:tm, :LANES] + prod[tm:, :LANES]) + (prod[:tm, LANES:] + prod[tm:, LANES:])
        lane = _lane_iota()
        lg = jnp.where(lane < N_EXPERTS, logits, -jnp.inf)
        m1 = jnp.max(lg, axis=-1, keepdims=True)
        i1 = jnp.min(jnp.where(lg == m1, lane, LANES), axis=-1, keepdims=True)
        sel1 = lane == i1
        lg2 = jnp.where(sel1, -jnp.inf, lg)
        m2 = jnp.max(lg2, axis=-1, keepdims=True)
        i2 = jnp.min(jnp.where(lg2 == m2, lane, LANES), axis=-1, keepdims=True)
        e2 = jnp.exp(m2 - m1)
        den = 1.0 + e2
        gates_ref[...] = jnp.where(lane == 0, i1.astype(F32), jnp.where(
            lane == 1, i2.astype(F32), jnp.where(lane == 2, 1.0 / den, jnp.where(lane == 3, e2 / den, 0.0))))


def _merge_call(x, mod, att1, att2, scan1, scan2, mo, rg, bg, wb, wo, gn, router, n1, s2):
    n, d = x.shape
    tm = TOK_TILE
    n1t = n1 // tm
    t2 = s2 // tm
    moe = router is not None
    n_slabs = att1.shape[0]

    def mod_row(t):
        return jnp.where(t < n1t, 0, 1 + (t - n1t) // t2)

    g1 = lambda t: jnp.minimum(t, n1t - 1)
    g2 = lambda t: jnp.maximum(t - n1t, 0)
    row = lambda w_: pl.BlockSpec((tm, w_), lambda t: (t, 0))
    row1 = lambda w_: pl.BlockSpec((tm, w_), lambda t: (g1(t), 0))
    row2 = lambda w_: pl.BlockSpec((tm, w_), lambda t: (g2(t), 0))
    const = lambda shp: pl.BlockSpec(shp, lambda t: (0,) * len(shp))
    in_specs = [row(d), pl.BlockSpec((None, 6, d), lambda t: (mod_row(t), 0, 0)),
                pl.BlockSpec((n_slabs, tm, LANES), lambda t: (0, g1(t), 0)),
                pl.BlockSpec((n_slabs, tm, LANES), lambda t: (0, g2(t), 0)),
                row1(ML_W), row1(ML_W), row1(RET_W), row1(RET_W),
                row2(ML_W), row2(ML_W), row2(RET_W), row2(RET_W),
                row(ML_W), row(RET_W), row(3 * d),
                const((d, d)), const((d, d)), const((1, d))]
    args = [x, mod, att1, att2, *scan1, *scan2, mo, rg, bg, wb, wo, gn]
    if moe:
        out_shape = [jax.ShapeDtypeStruct((n, d), F32), jax.ShapeDtypeStruct((n, d // LANES, LANES), F32)]
        out_specs = [row(d), pl.BlockSpec((tm, d // LANES, LANES), lambda t: (t, 0, 0))]
    else:
        out_shape = [jax.ShapeDtypeStruct((n, d), F32), jax.ShapeDtypeStruct((n, d), BF16)]
        out_specs = [row(d), row(d)]
    if moe:
        in_specs.append(const((d, LANES)))
        args.append(router)
        out_shape.append(jax.ShapeDtypeStruct((n, LANES), F32))
        out_specs.append(row(LANES))
    return pl.pallas_call(
        functools.partial(_merge_kernel, moe=moe, n1_tiles=n1t),
        out_shape=out_shape,
        grid=(n // tm,),
        in_specs=in_specs,
        out_specs=out_specs,
        compiler_params=_cparams(("arbitrary",)),
        name="merge_out",
    )(*args)


def _final_norm(x, g):
    ms = jnp.mean(x * x, axis=-1, keepdims=True)
    return x * lax.rsqrt(ms + EPS) * g


def _ffn_kernel(h_ref, x1_ref, mod_ref, wg_ref, wu_ref, wd_ref, fn_ref, o_ref, *, n_chunks, final):
    h = h_ref[...]
    f = wg_ref.shape[1]
    fc = f // n_chunks
    acc = jnp.zeros(o_ref.shape, F32)
    for c in range(n_chunks):
        sl = slice(c * fc, (c + 1) * fc)
        g = jnp.dot(h, wg_ref[:, sl], preferred_element_type=F32)
        u = jnp.dot(h, wu_ref[:, sl], preferred_element_type=F32)
        a = (g * _sigmoid(g) * u).astype(BF16)
        acc += jnp.dot(a, wd_ref[sl, :], preferred_element_type=F32)
    x2 = x1_ref[...] + mod_ref[5:6, :] * acc
    o_ref[...] = _final_norm(x2, fn_ref[...]) if final else x2


def _ffn_call(h2, x1, mod, wg, wu, wd, fn, n1, s2, final):
    n, d = x1.shape
    f = wg.shape[1]
    tm = FFN_TILE

    def mod_row(t):
        return jnp.where(t * tm < n1, 0, 1 + (t * tm - n1) // s2)

    row = lambda dt: pl.BlockSpec((tm, d), lambda t: (t, 0))
    const = lambda shp: pl.BlockSpec(shp, lambda t: (0,) * len(shp), pipeline_mode=pl.Buffered(1))
    return pl.pallas_call(
        functools.partial(_ffn_kernel, n_chunks=2, final=final),
        out_shape=jax.ShapeDtypeStruct((n, d), F32),
        grid=(n // tm,),
        in_specs=[row(BF16), row(F32), pl.BlockSpec((None, 6, d), lambda t: (mod_row(t), 0, 0)),
                  const((d, f)), const((d, f)), const((f, d)), pl.BlockSpec((1, d), lambda t: (0, 0))],
        out_specs=row(F32),
        compiler_params=_cparams(("arbitrary",)),
        name="ffn_dense",
    )(h2, x1, mod, wg, wu, wd, fn)


def _route_tables(route, n):
    tmx = MOE_TILE
    e = jnp.concatenate([route[:, 0], route[:, 1]]).astype(jnp.int32)
    oh = (e[:, None] == jnp.arange(N_EXPERTS, dtype=jnp.int32)[None, :]).astype(jnp.int32)
    cs = jnp.cumsum(oh, axis=0)
    counts = cs[-1]
    rank = jnp.sum(oh * cs, axis=1) - 1
    padded = ((counts + tmx - 1) // tmx) * tmx
    pend = jnp.cumsum(padded)
    pos = jnp.sum(oh * (pend - padded)[None, :], axis=1) + rank
    n_rows = 2 * n + N_EXPERTS * tmx
    n_tiles = n_rows // tmx
    inv = jnp.full((n_rows,), -1, jnp.int32).at[pos].set(jnp.arange(2 * n, dtype=jnp.int32), unique_indices=True)
    valid = inv >= 0
    src = jnp.concatenate([jnp.where(valid, inv % n, 0), jnp.zeros((2 * tmx,), jnp.int32)])
    dump = 2 * n + jnp.arange(n_rows, dtype=jnp.int32) % (2 * tmx)
    dst = jnp.concatenate([2 * n + tmx + jnp.arange(tmx, dtype=jnp.int32), jnp.where(valid, inv, dump)])
    tile_e = jnp.sum((jnp.arange(n_tiles + 1, dtype=jnp.int32) * tmx)[:, None] >= pend[None, :], axis=1)
    tile_e = jnp.minimum(tile_e, N_EXPERTS - 1).astype(jnp.int32)
    return src.reshape(n_tiles + 2, 1, tmx), dst.reshape(n_tiles + 1, 1, tmx), tile_e


def _moe_routed_kernel(te_ref, src_ref, srcn_ref, dst_ref, h_hbm, wg_ref, wu_ref, wd_ref, yy_hbm,
                       xbuf, ybuf, sem_in, sem_out):
    del te_ref
    tmx = MOE_TILE
    t = pl.program_id(0)
    last = pl.num_programs(0) - 1
    slot = t % 2
    other = 1 - slot

    def gather(idx_ref, s):
        for r in range(tmx):
            pltpu.make_async_copy(h_hbm.at[idx_ref[0, r]], xbuf.at[s, r], sem_in.at[s]).start(priority=r % 2)

    def wait_gather(s):
        pltpu.make_async_copy(h_hbm.at[pl.ds(0, tmx)], xbuf.at[s], sem_in.at[s]).wait()

    def wait_scatter(s):
        pltpu.make_async_copy(ybuf.at[s], yy_hbm.at[pl.ds(0, tmx)], sem_out.at[s]).wait()

    @pl.when(t == 0)
    def _prologue():
        ybuf[...] = jnp.zeros_like(ybuf)
        gather(src_ref, 0)

    @pl.when(t >= 1)
    def _free_ybuf():
        wait_scatter(slot)

    wait_gather(slot)
    gather(srcn_ref, other)
    for r in range(tmx):
        pltpu.make_async_copy(ybuf.at[other, r], yy_hbm.at[dst_ref[0, r]], sem_out.at[other]).start(priority=r % 2)
    x = _load_token_tiles(xbuf.at[slot]).astype(BF16)
    g = jnp.dot(x, wg_ref[...], preferred_element_type=F32)
    u = jnp.dot(x, wu_ref[...], preferred_element_type=F32)
    a = (g * _sigmoid(g) * u).astype(BF16)
    _store_token_tiles(ybuf.at[slot], jnp.dot(a, wd_ref[...], preferred_element_type=F32))

    @pl.when(t == last)
    def _drain():
        wait_gather(other)
        wait_scatter(other)


def _moe_routed_call(h2, src, dst, tile_e, wg, wu, wd):
    n, nj, _ = h2.shape
    d = nj * LANES
    ne, _, f = wg.shape
    tmx = MOE_TILE
    n_tiles = tile_e.shape[0] - 1
    smem_blk = lambda fn_: pl.BlockSpec((None, 1, tmx), fn_, memory_space=pltpu.SMEM)
    grid_spec = pltpu.PrefetchScalarGridSpec(
        num_scalar_prefetch=1,
        grid=(n_tiles + 1,),
        in_specs=[
            smem_blk(lambda t, te: (t, 0, 0)),
            smem_blk(lambda t, te: (t + 1, 0, 0)),
            smem_blk(lambda t, te: (t, 0, 0)),
            pl.BlockSpec(memory_space=pl.ANY),
            pl.BlockSpec((None, d, f), lambda t, te: (te[t], 0, 0)),
            pl.BlockSpec((None, d, f), lambda t, te: (te[t], 0, 0)),
            pl.BlockSpec((None, f, d), lambda t, te: (te[t], 0, 0)),
        ],
        out_specs=pl.BlockSpec(memory_space=pl.ANY),
        scratch_shapes=[pltpu.VMEM((2, tmx, nj, LANES), F32), pltpu.VMEM((2, tmx, nj, LANES), F32),
                        pltpu.SemaphoreType.DMA((2,)), pltpu.SemaphoreType.DMA((2,))],
    )
    return pl.pallas_call(
        _moe_routed_kernel,
        out_shape=jax.ShapeDtypeStruct((2 * n + 2 * tmx, nj, LANES), F32),
        grid_spec=grid_spec,
        compiler_params=_cparams(("arbitrary",)),
        name="moe_experts",
    )(tile_e, src, src, dst, h2, wg, wu, wd)


def _moe_combine_kernel(x1_ref, mod_ref, route_ref, y1_ref, y2_ref, fn_ref, o_ref, *, final):
    w1 = route_ref[:, 2:3]
    w2 = route_ref[:, 3:4]
    y = w1 * _load_token_tiles(y1_ref) + w2 * _load_token_tiles(y2_ref)
    x2 = x1_ref[...] + mod_ref[5:6, :] * y
    o_ref[...] = _final_norm(x2, fn_ref[...]) if final else x2


def _moe_combine_call(x1, mod, route, yy, fn, n1, s2, final):
    n, d = x1.shape
    tm = FFN_TILE
    nt = n // tm

    def mod_row(t):
        return jnp.where(t * tm < n1, 0, 1 + (t * tm - n1) // s2)

    row = lambda w_: pl.BlockSpec((tm, w_), lambda t: (t, 0))
    return pl.pallas_call(
        functools.partial(_moe_combine_kernel, final=final),
        out_shape=jax.ShapeDtypeStruct((n, d), F32),
        grid=(nt,),
        in_specs=[row(d), pl.BlockSpec((None, 6, d), lambda t: (mod_row(t), 0, 0)), row(LANES),
                  pl.BlockSpec((tm, d // LANES, LANES), lambda t: (t, 0, 0)),
                  pl.BlockSpec((tm, d // LANES, LANES), lambda t: (nt + t, 0, 0)),
                  pl.BlockSpec((1, d), lambda t: (0, 0))],
        out_specs=row(d),
        compiler_params=_cparams(("arbitrary",)),
        name="moe_combine",
    )(x1, mod, route, yy, yy, fn)


def _pair_q_heads(a, axis):
    g = N_HEADS // N_KV_HEADS
    shp = a.shape
    a = a.reshape(shp[:axis] + (N_KV_HEADS, g, HEAD_DIM) + shp[axis + 1:])
    a = jnp.swapaxes(a, axis, axis + 1)
    return a.reshape(shp)


def _prep_in_weights(w_in, b_in):
    g0 = ATT_Q_W + 2 * ATT_KV_W + 4 * ML_W
    pad = N_IN_PAD - w_in.shape[-1]

    def reorder(a):
        parts = [_pair_q_heads(a[..., :ATT_Q_W], a.ndim - 1), a[..., ATT_Q_W:g0], a[..., g0 + N_GATES:],
                 a[..., g0:g0 + N_GATES], jnp.zeros(a.shape[:-1] + (pad,), a.dtype)]
        return jnp.concatenate(parts, axis=-1)

    return reorder(w_in).astype(BF16), reorder(b_in)[:, None, :]


def _rope_tables(s2, tm):
    pos = jnp.arange(s2)
    rowp = (pos // GRID_W).astype(F32)
    colp = (pos % GRID_W).astype(F32)
    quarter = HEAD_DIM // 4
    inv = ROPE_BASE ** (-jnp.arange(quarter, dtype=F32) / quarter)
    ang_r = rowp[:, None] * inv
    ang_c = colp[:, None] * inv
    cos_h = jnp.concatenate([jnp.cos(ang_r), jnp.cos(ang_r), jnp.cos(ang_c), jnp.cos(ang_c)], axis=1)
    sin_h = jnp.concatenate([-jnp.sin(ang_r), jnp.sin(ang_r), -jnp.sin(ang_c), jnp.sin(ang_c)], axis=1)
    cos_t = jnp.concatenate([cos_h, cos_h], axis=1)
    sin_t = jnp.concatenate([sin_h, sin_h], axis=1)
    cos_t = jnp.concatenate([jnp.ones((tm, LANES), F32), cos_t], axis=0)
    sin_t = jnp.concatenate([jnp.zeros((tm, LANES), F32), sin_t], axis=0)
    return cos_t, sin_t


def _pair_blockdiag(m):
    b, nd, h, d, _ = m.shape
    m = m.reshape(b, nd, h // 2, 2, d, d)
    z = jnp.zeros_like(m[:, :, :, 0])
    top = jnp.concatenate([m[:, :, :, 0], z], axis=-1)
    bot = jnp.concatenate([z, m[:, :, :, 1]], axis=-1)
    return jnp.concatenate([top, bot], axis=-2)


def _pair_unblock(m):
    b, nd, p, _, _ = m.shape
    a = m[:, :, :, :HALF, :HALF]
    c = m[:, :, :, HALF:, HALF:]
    return jnp.stack([a, c], axis=3).reshape(b, nd, 2 * p, HALF, HALF)


def kernel(x_prompt, x_sample, c, cache_attn_k, cache_attn_v, state_mlstm_c, state_mlstm_n, state_mlstm_m,
           state_ret_s, c_ctx, w_ada, b_ada, norm_mix, norm_ffn, w_in, b_in, q_norm, k_norm, ret_decay,
           w_branch, w_out, ffn_w_gate, ffn_w_up, ffn_w_down, moe_router, moe_w_gate, moe_w_up, moe_w_down,
           final_norm):
    b1, s1, d = x_prompt.shape
    b2, s2, _ = x_sample.shape
    depth = w_in.shape[0]
    n1, n2 = b1 * s1, b2 * s2
    assert s1 % TOK_TILE == 0 and s2 % FFN_TILE == 0 and n1 % FFN_TILE == 0 and b2 + 1 <= 8

    x = jnp.concatenate([x_prompt.reshape(n1, d), x_sample.reshape(n2, d)], axis=0)
    c8 = jnp.concatenate([c_ctx[None, :], c, jnp.zeros((8 - 1 - b2, d), F32)], axis=0)
    mod_all = _ada_call(c8, w_ada, b_ada).reshape(depth, 8, 6, d)

    w_in_p, b_in_p = _prep_in_weights(w_in, b_in)
    cos_t, sin_t = _rope_tables(s2, TOK_TILE)
    wb = jnp.concatenate([_pair_q_heads(w_branch[:, :ATT_Q_W], 1), w_branch[:, ATT_Q_W:]], axis=1).astype(BF16)
    wo = w_out.astype(BF16)
    fn = final_norm[None, :]

    zeros_like_state = lambda shp: jnp.zeros((b1,) + shp, F32)
    states = []
    for l in range(depth):
        mod = mod_all[l]
        qg = jnp.tile(q_norm[l], 2)[None, :]
        kg = jnp.tile(k_norm[l], 2)[None, :]
        q, k, v, kb, vb, mqkv, rqkv, mo, rg, bg, gcol, grow = _inproj_call(
            x, mod, norm_mix[l][None, :], w_in_p[l], b_in_p[l], qg, kg, cos_t, sin_t, n1, s2)

        vt1 = jnp.swapaxes(vb[:n1].reshape(b1, s1, ATT_KV_W), 1, 2)
        att1 = _attn_call(q, kb[:n1].reshape(b1, s1, ATT_KV_W), vt1, 0, s1)
        k2 = jnp.concatenate([cache_attn_k[:, l].reshape(b2, -1, ATT_KV_W).astype(BF16),
                              kb[n1:].reshape(b2, s2, ATT_KV_W)], axis=1)
        v2 = jnp.concatenate([cache_attn_v[:, l].reshape(b2, -1, ATT_KV_W).astype(BF16),
                              vb[n1:].reshape(b2, s2, ATT_KV_W)], axis=1)
        att2 = _attn_call(q, k2, jnp.swapaxes(v2, 1, 2), n1, s2)

        rd = jnp.broadcast_to(ret_decay[l].reshape(2 * RET_HEADS, 1), (2 * RET_HEADS, LANES))
        r1 = _scan_call(mqkv, rqkv, gcol, grow, rd,
                        zeros_like_state((2, 2, LANES, LANES)), zeros_like_state((2, 2, 1, LANES)),
                        zeros_like_state((2, 2, 1, LANES)), zeros_like_state((2, 2, LANES, LANES)), 0, s1)
        c0 = _pair_blockdiag(state_mlstm_c[:, l])
        n0 = state_mlstm_n[:, l].reshape(b2, 2, 2, 1, LANES)
        m0 = jnp.repeat(state_mlstm_m[:, l], HALF, axis=-1).reshape(b2, 2, 2, 1, LANES)
        s0 = _pair_blockdiag(state_ret_s[:, l])
        r2 = _scan_call(mqkv, rqkv, gcol, grow, rd, c0, n0, m0, s0, n1, s2)
        states.append((k[:n1].reshape(b1, s1, N_KV_HEADS, HEAD_DIM), v[:n1].reshape(b1, s1, N_KV_HEADS, HEAD_DIM),
                       _pair_unblock(r1[4]), r1[5].reshape(b1, 2, ML_HEADS, HEAD_DIM), r1[6][:, :, :, 0, ::HALF].reshape(b1, 2, ML_HEADS),
                       _pair_unblock(r1[7])))

        moe = l % 2 == 1
        jj = l // 2
        router = jnp.pad(moe_router[jj], ((0, 0), (0, LANES - N_EXPERTS))) if moe else None
        outs = _merge_call(x, mod, att1, att2, r1[:4], r2[:4], mo, rg, bg, wb[l], wo[l], norm_ffn[l][None, :],
                           router, n1, s2)
        final = l == depth - 1
        if moe:
            x1, h2, route = outs
            src, dst, tile_e = _route_tables(route, n1 + n2)
            yy = _moe_routed_call(h2, src, dst, tile_e, moe_w_gate[jj].astype(BF16), moe_w_up[jj].astype(BF16),
                                  moe_w_down[jj].astype(BF16))
            x = _moe_combine_call(x1, mod, route, yy, fn, n1, s2, final)
        else:
            x1, h2 = outs
            x = _ffn_call(h2, x1, mod, ffn_w_gate[jj].astype(BF16), ffn_w_up[jj].astype(BF16),
                          ffn_w_down[jj].astype(BF16), fn, n1, s2, final)

    y_prompt = x[:n1].reshape(b1, s1, d)
    y_sample = x[n1:].reshape(b2, s2, d)
    stack = lambda i: jnp.stack([s[i] for s in states], axis=1)
    return (y_prompt, y_sample, stack(0), stack(1), stack(2), stack(3), stack(4), stack(5))
```

```python
import functools

import jax
import jax.numpy as jnp
from jax import lax
from jax.experimental import pallas as pl
from jax.experimental.pallas import tpu as pltpu

F32 = jnp.float32
BF16 = jnp.bfloat16

N_HEADS = 8
N_KV_HEADS = 2
HEAD_DIM = 64
ML_HEADS = 4
RET_HEADS = 4
GRID_W = 64
CHUNK = 128
ROPE_BASE = 10000.0
EPS = 1e-6
N_EXPERTS = 8
LANES = 128
HALF = 64

ATT_Q_W = N_HEADS * HEAD_DIM
ATT_KV_W = N_KV_HEADS * HEAD_DIM
ML_W = ML_HEADS * HEAD_DIM
RET_W = RET_HEADS * HEAD_DIM
N_GATES = 4 * ML_HEADS

TOK_TILE = 512
SCAN_SEQS = 2
FFN_TILE = 512
MOE_TILE = 512
ATT_Q_TILE = 256
ATT_K_TILE = 512
ATT_SHORT_SEQS = 4
LOG2E = 1.4426950408889634
Q_SCALE = 0.125 * LOG2E
VMEM_LIMIT = 56 * 1024 * 1024

O_AQ, O_AK, O_AV = 0, 512, 640
O_MQ, O_MK, O_MV, O_MO = 768, 1024, 1280, 1536
O_RQ, O_RK, O_RV, O_RG = 1792, 2048, 2304, 2560
O_BG = 2816
O_MG = 5888
N_IN_PAD = 6016


def _cparams(sem, vmem=VMEM_LIMIT):
    return pltpu.CompilerParams(dimension_semantics=sem, vmem_limit_bytes=vmem)


def _lane_iota(shape=(1, LANES)):
    return lax.broadcasted_iota(jnp.int32, shape, len(shape) - 1)


def _head_mean_matrix():
    r = lax.broadcasted_iota(jnp.int32, (LANES, LANES), 0) >> 6
    c = lax.broadcasted_iota(jnp.int32, (LANES, LANES), 1) >> 6
    return jnp.where(r == c, 1.0 / HALF, 0.0).astype(BF16)


def _head_rms(xs, mean_mat):
    t = xs[0].shape[0]
    sq = jnp.concatenate([x * x for x in xs], axis=0)
    hi = sq.astype(BF16)
    lo = (sq - hi.astype(F32)).astype(BF16)
    ms = jnp.dot(jnp.concatenate([hi, lo], axis=0), mean_mat, preferred_element_type=F32)
    ms = ms[:len(xs) * t] + ms[len(xs) * t:]
    return [x * lax.rsqrt(ms[i * t:(i + 1) * t] + EPS) for i, x in enumerate(xs)]


def _rope(y, cos, sin, lane):
    up = pltpu.roll(y, LANES - 16, 1)
    dn = pltpu.roll(y, 16, 1)
    partner = jnp.where((lane & 31) < 16, up, dn)
    return y * cos + partner * sin


def _log_sigmoid(x):
    return jnp.minimum(x, 0.0) - jnp.log1p(jnp.exp(-jnp.abs(x)))


def _sigmoid(x):
    return 1.0 / (1.0 + jnp.exp(-x))


def _store_token_tiles(ref, x):
    for j in range(x.shape[1] // LANES):
        ref[:, j, :] = x[:, j * LANES:(j + 1) * LANES]


def _load_token_tiles(ref):
    return jnp.concatenate([ref[:, j, :] for j in range(ref.shape[1])], axis=1)


def _split3(x):
    h = x.astype(BF16)
    r = x - h.astype(F32)
    m = r.astype(BF16)
    l = (r - m.astype(F32)).astype(BF16)
    return h, m, l


def _ada_kernel(c_ref, w_ref, b_ref, o_ref):
    c = c_ref[...]
    a = (c * _sigmoid(c)).astype(BF16)
    o_ref[...] = jnp.dot(a, w_ref[...].astype(BF16), preferred_element_type=F32) + b_ref[...]


def _ada_call(c8, w_ada, b_ada):
    depth, d, n6 = w_ada.shape
    tn = 1536
    return pl.pallas_call(
        _ada_kernel,
        out_shape=jax.ShapeDtypeStruct((depth, 8, n6), F32),
        grid=(depth, n6 // tn),
        in_specs=[
            pl.BlockSpec((8, d), lambda l, j: (0, 0)),
            pl.BlockSpec((None, d, tn), lambda l, j: (l, 0, j)),
            pl.BlockSpec((None, 1, tn), lambda l, j: (l, 0, j)),
        ],
        out_specs=pl.BlockSpec((None, 8, tn), lambda l, j: (l, 0, j)),
        compiler_params=_cparams(("arbitrary", "arbitrary")),
        name="ada_mod",
    )(c8, w_ada, b_ada.reshape(depth, 1, n6))


def _inproj_kernel(x_ref, mod_ref, gn_ref, w_ref, b_ref, qg_ref, kg_ref, cos_ref, sin_ref,
                   q_ref, k_ref, v_ref, kb_ref, vb_ref, mqkv_ref, rqkv_ref, mo_ref, rg_ref, bg_ref, gcol_ref,
                   grow_ref):
    x = x_ref[...]
    ms = jnp.mean(x * x, axis=-1, keepdims=True)
    h = x * lax.rsqrt(ms + EPS) * gn_ref[...]
    h = h * (1.0 + mod_ref[1:2, :]) + mod_ref[0:1, :]
    hb = h.astype(BF16)

    def seg(a, b):
        return jnp.dot(hb, w_ref[:, a:b], preferred_element_type=F32) + b_ref[:, a:b]

    lane = _lane_iota()
    mean_mat = _head_mean_matrix()
    cos = cos_ref[...]
    sin = sin_ref[...]
    n_q = ATT_Q_W // LANES
    za = seg(O_AQ, O_MQ)
    normed = _head_rms([za[:, s * LANES:(s + 1) * LANES] for s in range(n_q + 1)], mean_mat)
    for s in range(n_q):
        q_ref[s] = (_rope(normed[s] * qg_ref[...], cos, sin, lane) * Q_SCALE).astype(BF16)
    y = _rope(normed[n_q] * kg_ref[...], cos, sin, lane)
    k_ref[...] = y
    kb_ref[...] = y.astype(BF16)
    y = za[:, O_AV:O_AV + LANES]
    v_ref[...] = y
    vb_ref[...] = y.astype(BF16)

    zm = seg(O_MQ, O_RQ)
    mqkv_ref[:, 0:ML_W] = zm[:, 0:ML_W].astype(BF16)
    mqkv_ref[:, ML_W:2 * ML_W] = (zm[:, ML_W:2 * ML_W] * 0.125).astype(BF16)
    mqkv_ref[:, 2 * ML_W:3 * ML_W] = zm[:, 2 * ML_W:3 * ML_W].astype(BF16)
    mo_ref[...] = zm[:, 3 * ML_W:]

    zr = seg(O_RQ, O_BG)
    for s in range(RET_W // LANES):
        sl = slice(s * LANES, (s + 1) * LANES)
        rqkv_ref[:, sl] = _rope(zr[:, sl], cos, sin, lane).astype(BF16)
        slk = slice(RET_W + s * LANES, RET_W + (s + 1) * LANES)
        rqkv_ref[:, slk] = (_rope(zr[:, slk], cos, sin, lane) * 0.125).astype(BF16)
    rqkv_ref[:, 2 * RET_W:3 * RET_W] = zr[:, 2 * RET_W:3 * RET_W].astype(BF16)
    rg_ref[...] = zr[:, 3 * RET_W:]

    zg = seg(O_BG, N_IN_PAD)
    bg_ref[...] = _sigmoid(zg[:, :O_MG - O_BG]).astype(BF16)

    g = zg[:, O_MG - O_BG:]
    is_f = ((lane >> 2) & 1) == 1
    g = jnp.where(is_f, _log_sigmoid(g), g)
    gcol_ref[...] = g
    grow_ref[...] = g.T[0:N_GATES, :]


def _inproj_call(x, mod, gn, w, b, qg, kg, cos_t, sin_t, n1, s2):
    n, d = x.shape
    tm = TOK_TILE
    n1t = n1 // tm
    t2 = s2 // tm

    def mod_row(t):
        return jnp.where(t < n1t, 0, 1 + (t - n1t) // t2)

    def tab_row(t):
        return jnp.where(t < n1t, 0, 1 + (t - n1t) % t2)

    row = lambda w_: pl.BlockSpec((tm, w_), lambda t: (t, 0))
    const = lambda shp: pl.BlockSpec(shp, lambda t: (0,) * len(shp))
    n_slabs = ATT_Q_W // LANES
    out_shape = [
        jax.ShapeDtypeStruct((n_slabs, n, LANES), BF16),
        jax.ShapeDtypeStruct((n, ATT_KV_W), F32),
        jax.ShapeDtypeStruct((n, ATT_KV_W), F32),
        jax.ShapeDtypeStruct((n, ATT_KV_W), BF16),
        jax.ShapeDtypeStruct((n, ATT_KV_W), BF16),
        jax.ShapeDtypeStruct((n, 3 * ML_W), BF16),
        jax.ShapeDtypeStruct((n, 3 * RET_W), BF16),
        jax.ShapeDtypeStruct((n, ML_W), F32),
        jax.ShapeDtypeStruct((n, RET_W), F32),
        jax.ShapeDtypeStruct((n, 3 * d), BF16),
        jax.ShapeDtypeStruct((n, LANES), F32),
        jax.ShapeDtypeStruct((N_GATES, n), F32),
    ]
    out_specs = [pl.BlockSpec((n_slabs, tm, LANES), lambda t: (0, t, 0)), row(ATT_KV_W), row(ATT_KV_W),
                 row(ATT_KV_W), row(ATT_KV_W), row(3 * ML_W), row(3 * RET_W), row(ML_W),
                 row(RET_W), row(3 * d), row(LANES), pl.BlockSpec((N_GATES, tm), lambda t: (0, t))]
    return pl.pallas_call(
        _inproj_kernel,
        out_shape=out_shape,
        grid=(n // tm,),
        in_specs=[
            row(d),
            pl.BlockSpec((None, 6, d), lambda t: (mod_row(t), 0, 0)),
            const((1, d)),
            pl.BlockSpec((d, N_IN_PAD), lambda t: (0, 0), pipeline_mode=pl.Buffered(1)),
            const((1, N_IN_PAD)),
            const((1, LANES)),
            const((1, LANES)),
            pl.BlockSpec((tm, LANES), lambda t: (tab_row(t), 0)),
            pl.BlockSpec((tm, LANES), lambda t: (tab_row(t), 0)),
        ],
        out_specs=out_specs,
        compiler_params=_cparams(("arbitrary",)),
        name="in_proj",
    )(x, mod, gn, w, b, qg, kg, cos_t, sin_t)


def _attn_kernel(q_ref, k_ref, vt_ref, o_ref, s_sc, *, tk, tq):
    nb = k_ref.shape[0]
    nk = k_ref.shape[1] // tk
    row_lo = lax.broadcasted_iota(jnp.int32, (LANES, 1), 0) < HALF

    for bi in range(nb):
        for j in range(q_ref.shape[0]):
            qt = q_ref[j, bi * tq:(bi + 1) * tq, :].astype(F32).T.astype(BF16)
            outs = []
            for half in range(2):
                keep = row_lo if half == 0 else jnp.logical_not(row_lo)
                qm = jnp.where(keep, qt, jnp.zeros_like(qt))
                mx = None
                for c in range(nk):
                    s = jnp.dot(k_ref[bi, c * tk:(c + 1) * tk, :], qm, preferred_element_type=F32)
                    s_sc[half, c * tk:(c + 1) * tk, :] = s
                    cm = jnp.max(s, axis=0, keepdims=True)
                    mx = cm if mx is None else jnp.maximum(mx, cm)
                l = jnp.zeros((1, tq), F32)
                acc = jnp.zeros((LANES, tq), F32)
                for c in range(nk):
                    p = jnp.exp2(s_sc[half, c * tk:(c + 1) * tk, :] - mx)
                    l += jnp.sum(p, axis=0, keepdims=True)
                    acc += jnp.dot(vt_ref[bi, :, c * tk:(c + 1) * tk], p.astype(BF16), preferred_element_type=F32)
                outs.append(acc * (1.0 / l))
            o_ref[j, bi * tq:(bi + 1) * tq, :] = jnp.where(row_lo, outs[0], outs[1]).T.astype(o_ref.dtype)


def _attn_call(q, k, vt, tok_off, sq):
    n_slabs = q.shape[0]
    b, sk, _ = k.shape
    tq = min(ATT_Q_TILE, sq)
    tk = min(ATT_K_TILE, sk)
    nq = sq // tq
    nb = ATT_SHORT_SEQS if (nq == 1 and b % ATT_SHORT_SEQS == 0) else 1
    off = tok_off // (nb * tq)
    return pl.pallas_call(
        functools.partial(_attn_kernel, tk=tk, tq=tq),
        out_shape=jax.ShapeDtypeStruct((n_slabs, b * sq, LANES), BF16),
        grid=(b // nb, nq),
        in_specs=[
            pl.BlockSpec((n_slabs, nb * tq, LANES), lambda i, j: (0, off + i * nq + j, 0)),
            pl.BlockSpec((nb, sk, ATT_KV_W), lambda i, j: (i, 0, 0)),
            pl.BlockSpec((nb, ATT_KV_W, sk), lambda i, j: (i, 0, 0)),
        ],
        out_specs=pl.BlockSpec((n_slabs, nb * tq, LANES), lambda i, j: (0, i * nq + j, 0)),
        scratch_shapes=[pltpu.VMEM((2, sk, tq), F32)],
        compiler_params=_cparams(("arbitrary", "arbitrary")),
        name="attention",
    )(q, k, vt)


def _scan_kernel(*refs, nb):
    L = CHUNK
    seq_in = refs[:8 * nb]
    rd_ref, c0_ref, n0_ref, m0_ref, s0_ref = refs[8 * nb:8 * nb + 5]
    o0 = 8 * nb + 5
    hf_ref, hb_ref, of_ref, ob_ref, cout_ref, nout_ref, mout_ref, sout_ref = refs[o0:o0 + 8]
    c_all, n_all, m_all, s_all, dec_st, qdec_st, kdec_st, cdec_st = refs[o0 + 8:]
    b_idx = pl.program_id(0)
    j = pl.program_id(1)
    nc = pl.num_programs(1)
    lane = _lane_iota()
    lo = lane < HALF
    row_i = lax.broadcasted_iota(jnp.int32, (L, L), 0)
    col_i = lax.broadcasted_iota(jnp.int32, (L, L), 1)
    blockmask = (row_i >> 6) == (col_i >> 6)
    causal = (row_i >= col_i, row_i <= col_i)

    @pl.when(jnp.logical_and(b_idx == 0, j == 0))
    def _init_tables():
        lg = _log_sigmoid(rd_ref[...])
        pos = lax.broadcasted_iota(jnp.int32, (L, 1), 0).astype(F32)
        diff = (row_i - col_i).astype(F32)
        for d in range(2):
            sd = diff if d == 0 else -diff
            for p in range(2):
                qd, kd, cd = [], [], []
                for e in range(2):
                    r = d * RET_HEADS + 2 * p + e
                    g = lg[r:r + 1, 0:1]
                    dec_st[d, p, e * L:(e + 1) * L, :] = jnp.where(sd >= 0, jnp.exp(g * jnp.maximum(sd, 0.0)), 0.0)
                    if d == 0:
                        qd.append(jnp.exp(g * (pos + 1.0)))
                        kd.append(jnp.exp(g * (L - 1.0 - pos)))
                    else:
                        qd.append(jnp.exp(g * (L - pos)))
                        kd.append(jnp.exp(g * pos))
                    cd.append(jnp.exp(g * float(L)))
                qdec_st[d, p] = jnp.where(lo, qd[0], qd[1])
                kdec_st[d, p] = jnp.where(lo, kd[0], kd[1])
                cdec_st[d, p] = jnp.where(lo, cd[0], cd[1])

    @pl.when(j == 0)
    def _load_state():
        c_all[...] = c0_ref[...]
        n_all[...] = n0_ref[...]
        m_all[...] = m0_ref[...]
        s_all[...] = s0_ref[...]

    tri = (row_i >= col_i).astype(BF16)
    triu = (row_i <= col_i).astype(BF16)

    def cumsums(gc_ref, gr_ref, d):
        a_col, a_row = (tri, triu) if d == 0 else (triu, tri)
        col3 = jnp.dot(a_col, jnp.concatenate(_split3(gc_ref[...]), axis=1), preferred_element_type=F32)
        row3 = jnp.dot(jnp.concatenate(_split3(gr_ref[...]), axis=0), a_row, preferred_element_type=F32)
        col = col3[:, 0:LANES] + col3[:, LANES:2 * LANES] + col3[:, 2 * LANES:]
        rowv = row3[0:N_GATES] + row3[N_GATES:2 * N_GATES] + row3[2 * N_GATES:]
        return col, rowv

    pair = lambda x0, x1: jnp.where(lo, x0, x1)
    lo2 = (_lane_iota((1, 2 * LANES)) & (LANES - 1)) < HALF

    keys = [(bb, d, p) for bb in range(nb) for d in range(2) for p in range(2)]
    c_old = {k: c_all[k] for k in keys}
    n_old = {k: n_all[k] for k in keys}
    m_old = {k: m_all[k] for k in keys}
    s_old = {k: s_all[k] for k in keys}
    c_new, n_new, m_new_st, s_new = {}, {}, {}, {}

    for bb in range(nb):
        mf_ref, mb_ref, rf_ref, rb_ref, gcf_ref, gcb_ref, grf_ref, grb_ref = seq_in[8 * bb:8 * bb + 8]
        for d, (m_ref, r_ref, gc_ref, gr_ref, h_out, o_out) in enumerate(
                ((mf_ref, rf_ref, gcf_ref, grf_ref, hf_ref, of_ref),
                 (mb_ref, rb_ref, gcb_ref, grb_ref, hb_ref, ob_ref))):
            gcol = gc_ref[...]
            grow = gr_ref[...]
            cum_col, cum_row = cumsums(gc_ref, gr_ref, d)
            gi = 2 * ML_HEADS * d
            gf = gi + ML_HEADS
            last = L - 1 if d == 0 else 0
            for p in range(2):
                sl = slice(p * LANES, (p + 1) * LANES)
                mq, mk, mv = (m_ref[:, i * ML_W + p * LANES:i * ML_W + (p + 1) * LANES] for i in range(3))
                rq, rk, rv = (r_ref[:, i * RET_W + p * LANES:i * RET_W + (p + 1) * LANES] for i in range(3))
                c2 = c_old[bb, d, p]
                n2 = n_old[bb, d, p]
                m2 = m_old[bb, d, p]
                s2 = s_old[bb, d, p]
                zero = jnp.zeros_like(mq)
                zero_st = jnp.zeros((LANES, LANES), BF16)
                bdiag = lambda x0, x1, z: jnp.concatenate(
                    [jnp.concatenate([x0, z], axis=1), jnp.concatenate([z, x1], axis=1)], axis=0)
                q_cat = jnp.concatenate([mq, rq], axis=1)
                q_st = jnp.dot(q_cat, bdiag(c2.astype(BF16), s2.astype(BF16), zero_st), preferred_element_type=F32)
                q_c = q_st[:, :LANES]
                q_s = q_st[:, LANES:] * qdec_st[d, p]
                q_n = mq.astype(F32) * n2
                hds = (2 * p, 2 * p + 1)
                i_cols = [gcol[:, gi + hd:gi + hd + 1] for hd in hds]
                b_cols = [cum_col[:, gf + hd:gf + hd + 1] for hd in hds]
                b2 = pair(*b_cols)
                i2 = pair(*i_cols)
                a2 = b2 + m2
                dms = []
                for e, hd in enumerate(hds):
                    i_row = grow[gi + hd:gi + hd + 1, :]
                    b_row = cum_row[gf + hd:gf + hd + 1, :]
                    dms.append(jnp.where(causal[d], b_cols[e] - b_row + i_row, -jnp.inf))
                mt2 = jnp.maximum(a2, pair(*[jnp.max(dm, axis=-1, keepdims=True) for dm in dms]))
                wa2 = jnp.exp(a2 - mt2)
                qs = jnp.concatenate([jnp.where(lo2, q_cat, jnp.zeros_like(q_cat)),
                                      jnp.where(lo2, jnp.zeros_like(q_cat), q_cat)], axis=0)
                scores = lax.dot_general(qs, bdiag(mk, rk, zero), (((1,), (1,)), ((), ())),
                                         preferred_element_type=F32)
                wd = jnp.concatenate([jnp.exp(dms[e] - mt2[:, e * HALF:e * HALF + 1]) for e in range(2)], axis=0)
                s_m = scores[:, :L] * wd
                s_r = scores[:, L:] * dec_st[d, p]
                sv = jnp.dot(jnp.concatenate([s_m, s_r], axis=1).astype(BF16), bdiag(mv, rv, zero),
                             preferred_element_type=F32)
                rs = jnp.sum(s_m, axis=-1, keepdims=True)
                qns = [jnp.sum(jnp.where(lo if e == 0 else jnp.logical_not(lo), q_n, 0.0), axis=-1, keepdims=True)
                       for e in range(2)]
                den2 = pair(rs[:L], rs[L:]) + wa2 * pair(*qns)
                dd2 = jnp.maximum(jnp.abs(den2), jnp.exp(-mt2))
                h_out[bb, :, sl] = (pair(sv[:L, :LANES], sv[L:, :LANES]) + wa2 * q_c) / dd2
                o_out[bb, :, sl] = pair(sv[:L, LANES:], sv[L:, LANES:]) + q_s
                b_last = b2[last:last + 1, :]
                g2 = b_last - b2 + i2
                m_new = jnp.maximum(b_last + m2, jnp.max(g2, axis=0, keepdims=True))
                wc2 = jnp.exp(b_last + m2 - m_new)
                kw_m = mk.astype(F32) * jnp.exp(g2 - m_new)
                kw_r = rk.astype(F32) * kdec_st[d, p]
                upd = lax.dot_general(jnp.concatenate([kw_m, kw_r], axis=1).astype(BF16),
                                      jnp.concatenate([mv, rv], axis=1), (((0,), (0,)), ((), ())),
                                      preferred_element_type=F32)
                c_new[bb, d, p] = jnp.where(blockmask, wc2 * c2 + upd[:LANES, :LANES], 0.0)
                n_new[bb, d, p] = wc2 * n2 + jnp.sum(kw_m, axis=0, keepdims=True)
                m_new_st[bb, d, p] = m_new
                s_new[bb, d, p] = jnp.where(blockmask, cdec_st[d, p] * s2 + upd[LANES:, LANES:], 0.0)

    for k in keys:
        c_all[k] = c_new[k]
        n_all[k] = n_new[k]
        m_all[k] = m_new_st[k]
        s_all[k] = s_new[k]

    @pl.when(j == nc - 1)
    def _store_state():
        cout_ref[...] = c_all[...]
        nout_ref[...] = n_all[...]
        mout_ref[...] = m_all[...]
        sout_ref[...] = s_all[...]


def _scan_call(mqkv, rqkv, gcol, grow, rd, c0, n0, m0, s0, tok_off, s):
    b = c0.shape[0]
    nb = SCAN_SEQS
    L = CHUNK
    nc = s // L
    off = tok_off // L
    fwd = lambda w_, bb: pl.BlockSpec((L, w_), lambda i, j: (off + (i * nb + bb) * nc + j, 0))
    bwd = lambda w_, bb: pl.BlockSpec((L, w_), lambda i, j: (off + (i * nb + bb) * nc + nc - 1 - j, 0))
    st = lambda shp: pl.BlockSpec((nb,) + shp, lambda i, j: (i,) + (0,) * len(shp))
    pair_mat = (2, 2, LANES, LANES)
    pair_vec = (2, 2, 1, LANES)
    seq_in_specs, seq_args = [], []
    for bb in range(nb):
        seq_in_specs += [
            fwd(3 * ML_W, bb), bwd(3 * ML_W, bb), fwd(3 * RET_W, bb), bwd(3 * RET_W, bb),
            fwd(LANES, bb), bwd(LANES, bb),
            pl.BlockSpec((N_GATES, L), lambda i, j, bb=bb: (0, off + (i * nb + bb) * nc + j)),
            pl.BlockSpec((N_GATES, L), lambda i, j, bb=bb: (0, off + (i * nb + bb) * nc + nc - 1 - j)),
        ]
        seq_args += [mqkv, mqkv, rqkv, rqkv, gcol, gcol, grow, grow]
    seq_out_shape = [jax.ShapeDtypeStruct((b, s, w_), F32) for w_ in (ML_W, ML_W, RET_W, RET_W)]
    seq_out_specs = [pl.BlockSpec((nb, L, ML_W), lambda i, j: (i, j, 0)),
                     pl.BlockSpec((nb, L, ML_W), lambda i, j: (i, nc - 1 - j, 0)),
                     pl.BlockSpec((nb, L, RET_W), lambda i, j: (i, j, 0)),
                     pl.BlockSpec((nb, L, RET_W), lambda i, j: (i, nc - 1 - j, 0))]
    state_shape = [
        jax.ShapeDtypeStruct((b,) + pair_mat, F32), jax.ShapeDtypeStruct((b,) + pair_vec, F32),
        jax.ShapeDtypeStruct((b,) + pair_vec, F32), jax.ShapeDtypeStruct((b,) + pair_mat, F32),
    ]
    state_specs = [st(pair_mat), st(pair_vec), st(pair_vec), st(pair_mat)]
    outs = pl.pallas_call(
        functools.partial(_scan_kernel, nb=nb),
        out_shape=seq_out_shape + state_shape,
        grid=(b // nb, nc),
        in_specs=seq_in_specs + [pl.BlockSpec((8, LANES), lambda i, j: (0, 0))] + state_specs,
        out_specs=seq_out_specs + state_specs,
        scratch_shapes=[
            pltpu.VMEM((nb,) + pair_mat, F32), pltpu.VMEM((nb,) + pair_vec, F32),
            pltpu.VMEM((nb,) + pair_vec, F32), pltpu.VMEM((nb,) + pair_mat, F32),
            pltpu.VMEM((2, 2, 2 * L, L), F32), pltpu.VMEM((2, 2, L, LANES), F32),
            pltpu.VMEM((2, 2, L, LANES), F32), pltpu.VMEM((2, 2, 1, LANES), F32),
        ],
        compiler_params=_cparams(("arbitrary", "arbitrary")),
        name="scan_mixers",
    )(*seq_args, rd, c0, n0, m0, s0)
    return [o.reshape(b * s, o.shape[-1]) for o in outs[:4]] + list(outs[4:])


def _merge_kernel(x_ref, mod_ref, att1_ref, att2_ref, hf1_ref, hb1_ref, of1_ref, ob1_ref, hf2_ref, hb2_ref,
                  of2_ref, ob2_ref, mo_ref, rg_ref, bg_ref, wb_ref, wo_ref, gn_ref, *rest, moe, n1_tiles):
    sc1_refs = (hf1_ref, hb1_ref, of1_ref, ob1_ref)
    sc2_refs = (hf2_ref, hb2_ref, of2_ref, ob2_ref)
    if moe:
        router_ref, x1_ref, h2_ref, gates_ref = rest
    else:
        x1_ref, h2_ref = rest
    d = x_ref.shape[1]
    first = pl.program_id(0) < n1_tiles
    pick = lambda r1, r2, idx: jnp.where(first, r1[idx], r2[idx])
    mean_mat = _head_mean_matrix()
    att = jnp.concatenate([pick(att1_ref, att2_ref, s) for s in range(ATT_Q_W // LANES)], axis=1)
    y = jnp.dot(att, wb_ref[0:ATT_Q_W, :], preferred_element_type=F32) * bg_ref[:, 0:d].astype(F32)
    n_t = ML_W // LANES
    sums = []
    for s in range(n_t):
        sl = (slice(None), slice(s * LANES, (s + 1) * LANES))
        hf, hb, of, ob = (pick(r1, r2, sl) for r1, r2 in zip(sc1_refs, sc2_refs))
        sums += [hf + hb, of + ob]
    normed = _head_rms(sums, mean_mat)
    ml = jnp.concatenate([(normed[2 * s] * _sigmoid(mo_ref[:, s * LANES:(s + 1) * LANES])).astype(BF16)
                          for s in range(n_t)], axis=1)
    rgs = [rg_ref[:, s * LANES:(s + 1) * LANES] for s in range(n_t)]
    ret = jnp.concatenate([(normed[2 * s + 1] * (rgs[s] * _sigmoid(rgs[s]))).astype(BF16)
                           for s in range(n_t)], axis=1)
    y += jnp.dot(ml, wb_ref[ATT_Q_W:ATT_Q_W + ML_W, :], preferred_element_type=F32) * bg_ref[:, d:2 * d].astype(F32)
    y += jnp.dot(ret, wb_ref[ATT_Q_W + ML_W:, :], preferred_element_type=F32) * bg_ref[:, 2 * d:3 * d].astype(F32)
    y = jnp.dot(y.astype(BF16), wo_ref[...], preferred_element_type=F32)
    x1 = x_ref[...] + mod_ref[2:3, :] * y
    x1_ref[...] = x1
    ms = jnp.mean(x1 * x1, axis=-1, keepdims=True)
    h2 = x1 * lax.rsqrt(ms + EPS) * gn_ref[...]
    h2 = h2 * (1.0 + mod_ref[4:5, :]) + mod_ref[3:4, :]
    if moe:
        _store_token_tiles(h2_ref, h2)
    else:
        h2_ref[...] = h2.astype(BF16)
    if moe:
        hh = h2.astype(BF16)
        hl = (h2 - hh.astype(F32)).astype(BF16)
        r = router_ref[...]
        rh = r.astype(BF16)
        rl = (r - rh.astype(F32)).astype(BF16)
        tm = h2.shape[0]
        prod = jnp.dot(jnp.concatenate([hh, hl], axis=0), jnp.concatenate([rh, rl], axis=1),
                       preferred_element_type=F32)
        logits = (prod[:tm, :LANES] + prod[tm:, :LANES]) + (prod[:tm, LANES:] + prod[tm:, LANES:])
        lane = _lane_iota()
        lg = jnp.where(lane < N_EXPERTS, logits, -jnp.inf)
        m1 = jnp.max(lg, axis=-1, keepdims=True)
        i1 = jnp.min(jnp.where(lg == m1, lane, LANES), axis=-1, keepdims=True)
        sel1 = lane == i1
        lg2 = jnp.where(sel1, -jnp.inf, lg)
        m2 = jnp.max(lg2, axis=-1, keepdims=True)
        i2 = jnp.min(jnp.where(lg2 == m2, lane, LANES), axis=-1, keepdims=True)
        e2 = jnp.exp(m2 - m1)
        den = 1.0 + e2
        gates_ref[...] = jnp.where(lane == 0, i1.astype(F32), jnp.where(
            lane == 1, i2.astype(F32), jnp.where(lane == 2, 1.0 / den, jnp.where(lane == 3, e2 / den, 0.0))))


def _merge_call(x, mod, att1, att2, scan1, scan2, mo, rg, bg, wb, wo, gn, router, n1, s2):
    n, d = x.shape
    tm = TOK_TILE
    n1t = n1 // tm
    t2 = s2 // tm
    moe = router is not None
    n_slabs = att1.shape[0]

    def mod_row(t):
        return jnp.where(t < n1t, 0, 1 + (t - n1t) // t2)

    g1 = lambda t: jnp.minimum(t, n1t - 1)
    g2 = lambda t: jnp.maximum(t - n1t, 0)
    row = lambda w_: pl.BlockSpec((tm, w_), lambda t: (t, 0))
    row1 = lambda w_: pl.BlockSpec((tm, w_), lambda t: (g1(t), 0))
    row2 = lambda w_: pl.BlockSpec((tm, w_), lambda t: (g2(t), 0))
    const = lambda shp: pl.BlockSpec(shp, lambda t: (0,) * len(shp))
    in_specs = [row(d), pl.BlockSpec((None, 6, d), lambda t: (mod_row(t), 0, 0)),
                pl.BlockSpec((n_slabs, tm, LANES), lambda t: (0, g1(t), 0)),
                pl.BlockSpec((n_slabs, tm, LANES), lambda t: (0, g2(t), 0)),
                row1(ML_W), row1(ML_W), row1(RET_W), row1(RET_W),
                row2(ML_W), row2(ML_W), row2(RET_W), row2(RET_W),
                row(ML_W), row(RET_W), row(3 * d),
                const((d, d)), const((d, d)), const((1, d))]
    args = [x, mod, att1, att2, *scan1, *scan2, mo, rg, bg, wb, wo, gn]
    if moe:
        out_shape = [jax.ShapeDtypeStruct((n, d), F32), jax.ShapeDtypeStruct((n, d // LANES, LANES), F32)]
        out_specs = [row(d), pl.BlockSpec((tm, d // LANES, LANES), lambda t: (t, 0, 0))]
    else:
        out_shape = [jax.ShapeDtypeStruct((n, d), F32), jax.ShapeDtypeStruct((n, d), BF16)]
        out_specs = [row(d), row(d)]
    if moe:
        in_specs.append(const((d, LANES)))
        args.append(router)
        out_shape.append(jax.ShapeDtypeStruct((n, LANES), F32))
        out_specs.append(row(LANES))
    return pl.pallas_call(
        functools.partial(_merge_kernel, moe=moe, n1_tiles=n1t),
        out_shape=out_shape,
        grid=(n // tm,),
        in_specs=in_specs,
        out_specs=out_specs,
        compiler_params=_cparams(("arbitrary",)),
        name="merge_out",
    )(*args)


def _final_norm(x, g):
    ms = jnp.mean(x * x, axis=-1, keepdims=True)
    return x * lax.rsqrt(ms + EPS) * g


def _ffn_kernel(h_ref, x1_ref, mod_ref, wg_ref, wu_ref, wd_ref, fn_ref, o_ref, *, n_chunks, final):
    h = h_ref[...]
    f = wg_ref.shape[1]
    fc = f // n_chunks
    acc = jnp.zeros(o_ref.shape, F32)
    for c in range(n_chunks):
        sl = slice(c * fc, (c + 1) * fc)
        g = jnp.dot(h, wg_ref[:, sl], preferred_element_type=F32)
        u = jnp.dot(h, wu_ref[:, sl], preferred_element_type=F32)
        a = (g * _sigmoid(g) * u).astype(BF16)
        acc += jnp.dot(a, wd_ref[sl, :], preferred_element_type=F32)
    x2 = x1_ref[...] + mod_ref[5:6, :] * acc
    o_ref[...] = _final_norm(x2, fn_ref[...]) if final else x2


def _ffn_call(h2, x1, mod, wg, wu, wd, fn, n1, s2, final):
    n, d = x1.shape
    f = wg.shape[1]
    tm = FFN_TILE

    def mod_row(t):
        return jnp.where(t * tm < n1, 0, 1 + (t * tm - n1) // s2)

    row = lambda dt: pl.BlockSpec((tm, d), lambda t: (t, 0))
    const = lambda shp: pl.BlockSpec(shp, lambda t: (0,) * len(shp), pipeline_mode=pl.Buffered(1))
    return pl.pallas_call(
        functools.partial(_ffn_kernel, n_chunks=2, final=final),
        out_shape=jax.ShapeDtypeStruct((n, d), F32),
        grid=(n // tm,),
        in_specs=[row(BF16), row(F32), pl.BlockSpec((None, 6, d), lambda t: (mod_row(t), 0, 0)),
                  const((d, f)), const((d, f)), const((f, d)), pl.BlockSpec((1, d), lambda t: (0, 0))],
        out_specs=row(F32),
        compiler_params=_cparams(("arbitrary",)),
        name="ffn_dense",
    )(h2, x1, mod, wg, wu, wd, fn)


def _route_tables(route, n):
    tmx = MOE_TILE
    e = jnp.concatenate([route[:, 0], route[:, 1]]).astype(jnp.int32)
    oh = (e[:, None] == jnp.arange(N_EXPERTS, dtype=jnp.int32)[None, :]).astype(jnp.int32)
    cs = jnp.cumsum(oh, axis=0)
    counts = cs[-1]
    rank = jnp.sum(oh * cs, axis=1) - 1
    padded = ((counts + tmx - 1) // tmx) * tmx
    pend = jnp.cumsum(padded)
    pos = jnp.sum(oh * (pend - padded)[None, :], axis=1) + rank
    n_rows = 2 * n + N_EXPERTS * tmx
    n_tiles = n_rows // tmx
    inv = jnp.full((n_rows,), -1, jnp.int32).at[pos].set(jnp.arange(2 * n, dtype=jnp.int32), unique_indices=True)
    valid = inv >= 0
    src = jnp.concatenate([jnp.where(valid, inv % n, 0), jnp.zeros((2 * tmx,), jnp.int32)])
    dump = 2 * n + jnp.arange(n_rows, dtype=jnp.int32) % (2 * tmx)
    dst = jnp.concatenate([2 * n + tmx + jnp.arange(tmx, dtype=jnp.int32), jnp.where(valid, inv, dump)])
    tile_e = jnp.sum((jnp.arange(n_tiles + 1, dtype=jnp.int32) * tmx)[:, None] >= pend[None, :], axis=1)
    tile_e = jnp.minimum(tile_e, N_EXPERTS - 1).astype(jnp.int32)
    return src.reshape(n_tiles + 2, 1, tmx), dst.reshape(n_tiles + 1, 1, tmx), tile_e


def _moe_routed_kernel(te_ref, src_ref, srcn_ref, dst_ref, h_hbm, wg_ref, wu_ref, wd_ref, yy_hbm,
                       xbuf, ybuf, sem_in, sem_out):
    del te_ref
    tmx = MOE_TILE
    t = pl.program_id(0)
    last = pl.num_programs(0) - 1
    slot = t % 2
    other = 1 - slot

    def gather(idx_ref, s):
        for r in range(tmx):
            pltpu.make_async_copy(h_hbm.at[idx_ref[0, r]], xbuf.at[s, r], sem_in.at[s]).start(priority=r % 2)

    def wait_gather(s):
        pltpu.make_async_copy(h_hbm.at[pl.ds(0, tmx)], xbuf.at[s], sem_in.at[s]).wait()

    def wait_scatter(s):
        pltpu.make_async_copy(ybuf.at[s], yy_hbm.at[pl.ds(0, tmx)], sem_out.at[s]).wait()

    @pl.when(t == 0)
    def _prologue():
        ybuf[...] = jnp.zeros_like(ybuf)
        gather(src_ref, 0)

    @pl.when(t >= 1)
    def _free_ybuf():
        wait_scatter(slot)

    wait_gather(slot)
    gather(srcn_ref, other)
    for r in range(tmx):
        pltpu.make_async_copy(ybuf.at[other, r], yy_hbm.at[dst_ref[0, r]], sem_out.at[other]).start(priority=r % 2)
    x = _load_token_tiles(xbuf.at[slot]).astype(BF16)
    g = jnp.dot(x, wg_ref[...], preferred_element_type=F32)
    u = jnp.dot(x, wu_ref[...], preferred_element_type=F32)
    a = (g * _sigmoid(g) * u).astype(BF16)
    _store_token_tiles(ybuf.at[slot], jnp.dot(a, wd_ref[...], preferred_element_type=F32))

    @pl.when(t == last)
    def _drain():
        wait_gather(other)
        wait_scatter(other)


def _moe_routed_call(h2, src, dst, tile_e, wg, wu, wd):
    n, nj, _ = h2.shape
    d = nj * LANES
    ne, _, f = wg.shape
    tmx = MOE_TILE
    n_tiles = tile_e.shape[0] - 1
    smem_blk = lambda fn_: pl.BlockSpec((None, 1, tmx), fn_, memory_space=pltpu.SMEM)
    grid_spec = pltpu.PrefetchScalarGridSpec(
        num_scalar_prefetch=1,
        grid=(n_tiles + 1,),
        in_specs=[
            smem_blk(lambda t, te: (t, 0, 0)),
            smem_blk(lambda t, te: (t + 1, 0, 0)),
            smem_blk(lambda t, te: (t, 0, 0)),
            pl.BlockSpec(memory_space=pl.ANY),
            pl.BlockSpec((None, d, f), lambda t, te: (te[t], 0, 0)),
            pl.BlockSpec((None, d, f), lambda t, te: (te[t], 0, 0)),
            pl.BlockSpec((None, f, d), lambda t, te: (te[t], 0, 0)),
        ],
        out_specs=pl.BlockSpec(memory_space=pl.ANY),
        scratch_shapes=[pltpu.VMEM((2, tmx, nj, LANES), F32), pltpu.VMEM((2, tmx, nj, LANES), F32),
                        pltpu.SemaphoreType.DMA((2,)), pltpu.SemaphoreType.DMA((2,))],
    )
    return pl.pallas_call(
        _moe_routed_kernel,
        out_shape=jax.ShapeDtypeStruct((2 * n + 2 * tmx, nj, LANES), F32),
        grid_spec=grid_spec,
        compiler_params=_cparams(("arbitrary",)),
        name="moe_experts",
    )(tile_e, src, src, dst, h2, wg, wu, wd)


def _moe_combine_kernel(x1_ref, mod_ref, route_ref, y1_ref, y2_ref, fn_ref, o_ref, *, final):
    w1 = route_ref[:, 2:3]
    w2 = route_ref[:, 3:4]
    y = w1 * _load_token_tiles(y1_ref) + w2 * _load_token_tiles(y2_ref)
    x2 = x1_ref[...] + mod_ref[5:6, :] * y
    o_ref[...] = _final_norm(x2, fn_ref[...]) if final else x2


def _moe_combine_call(x1, mod, route, yy, fn, n1, s2, final):
    n, d = x1.shape
    tm = FFN_TILE
    nt = n // tm

    def mod_row(t):
        return jnp.where(t * tm < n1, 0, 1 + (t * tm - n1) // s2)

    row = lambda w_: pl.BlockSpec((tm, w_), lambda t: (t, 0))
    return pl.pallas_call(
        functools.partial(_moe_combine_kernel, final=final),
        out_shape=jax.ShapeDtypeStruct((n, d), F32),
        grid=(nt,),
        in_specs=[row(d), pl.BlockSpec((None, 6, d), lambda t: (mod_row(t), 0, 0)), row(LANES),
                  pl.BlockSpec((tm, d // LANES, LANES), lambda t: (t, 0, 0)),
                  pl.BlockSpec((tm, d // LANES, LANES), lambda t: (nt + t, 0, 0)),
                  pl.BlockSpec((1, d), lambda t: (0, 0))],
        out_specs=row(d),
        compiler_params=_cparams(("arbitrary",)),
        name="moe_combine",
    )(x1, mod, route, yy, yy, fn)


def _pair_q_heads(a, axis):
    g = N_HEADS // N_KV_HEADS
    shp = a.shape
    a = a.reshape(shp[:axis] + (N_KV_HEADS, g, HEAD_DIM) + shp[axis + 1:])
    a = jnp.swapaxes(a, axis, axis + 1)
    return a.reshape(shp)


def _prep_in_weights(w_in, b_in):
    g0 = ATT_Q_W + 2 * ATT_KV_W + 4 * ML_W
    pad = N_IN_PAD - w_in.shape[-1]

    def reorder(a):
        parts = [_pair_q_heads(a[..., :ATT_Q_W], a.ndim - 1), a[..., ATT_Q_W:g0], a[..., g0 + N_GATES:],
                 a[..., g0:g0 + N_GATES], jnp.zeros(a.shape[:-1] + (pad,), a.dtype)]
        return jnp.concatenate(parts, axis=-1)

    return reorder(w_in).astype(BF16), reorder(b_in)[:, None, :]


def _rope_tables(s2, tm):
    pos = jnp.arange(s2)
    rowp = (pos // GRID_W).astype(F32)
    colp = (pos % GRID_W).astype(F32)
    quarter = HEAD_DIM // 4
    inv = ROPE_BASE ** (-jnp.arange(quarter, dtype=F32) / quarter)
    ang_r = rowp[:, None] * inv
    ang_c = colp[:, None] * inv
    cos_h = jnp.concatenate([jnp.cos(ang_r), jnp.cos(ang_r), jnp.cos(ang_c), jnp.cos(ang_c)], axis=1)
    sin_h = jnp.concatenate([-jnp.sin(ang_r), jnp.sin(ang_r), -jnp.sin(ang_c), jnp.sin(ang_c)], axis=1)
    cos_t = jnp.concatenate([cos_h, cos_h], axis=1)
    sin_t = jnp.concatenate([sin_h, sin_h], axis=1)
    cos_t = jnp.concatenate([jnp.ones((tm, LANES), F32), cos_t], axis=0)
    sin_t = jnp.concatenate([jnp.zeros((tm, LANES), F32), sin_t], axis=0)
    return cos_t, sin_t


def _pair_blockdiag(m):
    b, nd, h, d, _ = m.shape
    m = m.reshape(b, nd, h // 2, 2, d, d)
    z = jnp.zeros_like(m[:, :, :, 0])
    top = jnp.concatenate([m[:, :, :, 0], z], axis=-1)
    bot = jnp.concatenate([z, m[:, :, :, 1]], axis=-1)
    return jnp.concatenate([top, bot], axis=-2)


def _pair_unblock(m):
    b, nd, p, _, _ = m.shape
    a = m[:, :, :, :HALF, :HALF]
    c = m[:, :, :, HALF:, HALF:]
    return jnp.stack([a, c], axis=3).reshape(b, nd, 2 * p, HALF, HALF)


def kernel(x_prompt, x_sample, c, cache_attn_k, cache_attn_v, state_mlstm_c, state_mlstm_n, state_mlstm_m,
           state_ret_s, c_ctx, w_ada, b_ada, norm_mix, norm_ffn, w_in, b_in, q_norm, k_norm, ret_decay,
           w_branch, w_out, ffn_w_gate, ffn_w_up, ffn_w_down, moe_router, moe_w_gate, moe_w_up, moe_w_down,
           final_norm):
    b1, s1, d = x_prompt.shape
    b2, s2, _ = x_sample.shape
    depth = w_in.shape[0]
    n1, n2 = b1 * s1, b2 * s2
    assert n1 % TOK_TILE == 0 and s2 % TOK_TILE == 0 and s2 % FFN_TILE == 0 and n1 % FFN_TILE == 0 and b2 + 1 <= 8

    x = jnp.concatenate([x_prompt.reshape(n1, d), x_sample.reshape(n2, d)], axis=0)
    c8 = jnp.concatenate([c_ctx[None, :], c, jnp.zeros((8 - 1 - b2, d), F32)], axis=0)
    mod_all = _ada_call(c8, w_ada, b_ada).reshape(depth, 8, 6, d)

    w_in_p, b_in_p = _prep_in_weights(w_in, b_in)
    cos_t, sin_t = _rope_tables(s2, TOK_TILE)
    wb = jnp.concatenate([_pair_q_heads(w_branch[:, :ATT_Q_W], 1), w_branch[:, ATT_Q_W:]], axis=1).astype(BF16)
    wo = w_out.astype(BF16)
    fn = final_norm[None, :]

    zeros_like_state = lambda shp: jnp.zeros((b1,) + shp, F32)
    states = []
    for l in range(depth):
        mod = mod_all[l]
        qg = jnp.tile(q_norm[l], 2)[None, :]
        kg = jnp.tile(k_norm[l], 2)[None, :]
        q, k, v, kb, vb, mqkv, rqkv, mo, rg, bg, gcol, grow = _inproj_call(
            x, mod, norm_mix[l][None, :], w_in_p[l], b_in_p[l], qg, kg, cos_t, sin_t, n1, s2)

        vt1 = jnp.swapaxes(vb[:n1].reshape(b1, s1, ATT_KV_W), 1, 2)
        att1 = _attn_call(q, kb[:n1].reshape(b1, s1, ATT_KV_W), vt1, 0, s1)
        k2 = jnp.concatenate([cache_attn_k[:, l].reshape(b2, -1, ATT_KV_W).astype(BF16),
                              kb[n1:].reshape(b2, s2, ATT_KV_W)], axis=1)
        v2 = jnp.concatenate([cache_attn_v[:, l].reshape(b2, -1, ATT_KV_W).astype(BF16),
                              vb[n1:].reshape(b2, s2, ATT_KV_W)], axis=1)
        att2 = _attn_call(q, k2, jnp.swapaxes(v2, 1, 2), n1, s2)

        rd = jnp.broadcast_to(ret_decay[l].reshape(2 * RET_HEADS, 1), (2 * RET_HEADS, LANES))
        r1 = _scan_call(mqkv, rqkv, gcol, grow, rd,
                        zeros_like_state((2, 2, LANES, LANES)), zeros_like_state((2, 2, 1, LANES)),
                        zeros_like_state((2, 2, 1, LANES)), zeros_like_state((2, 2, LANES, LANES)), 0, s1)
        c0 = _pair_blockdiag(state_mlstm_c[:, l])
        n0 = state_mlstm_n[:, l].reshape(b2, 2, 2, 1, LANES)
        m0 = jnp.repeat(state_mlstm_m[:, l], HALF, axis=-1).reshape(b2, 2, 2, 1, LANES)
        s0 = _pair_blockdiag(state_ret_s[:, l])
        r2 = _scan_call(mqkv, rqkv, gcol, grow, rd, c0, n0, m0, s0, n1, s2)
        states.append((k[:n1].reshape(b1, s1, N_KV_HEADS, HEAD_DIM), v[:n1].reshape(b1, s1, N_KV_HEADS, HEAD_DIM),
                       _pair_unblock(r1[4]), r1[5].reshape(b1, 2, ML_HEADS, HEAD_DIM), r1[6][:, :, :, 0, ::HALF].reshape(b1, 2, ML_HEADS),
                       _pair_unblock(r1[7])))

        moe = l % 2 == 1
        jj = l // 2
        router = jnp.pad(moe_router[jj], ((0, 0), (0, LANES - N_EXPERTS))) if moe else None
        outs = _merge_call(x, mod, att1, att2, r1[:4], r2[:4], mo, rg, bg, wb[l], wo[l], norm_ffn[l][None, :],
                           router, n1, s2)
        final = l == depth - 1
        if moe:
            x1, h2, route = outs
            src, dst, tile_e = _route_tables(route, n1 + n2)
            yy = _moe_routed_call(h2, src, dst, tile_e, moe_w_gate[jj].astype(BF16), moe_w_up[jj].astype(BF16),
                                  moe_w_down[jj].astype(BF16))
            x = _moe_combine_call(x1, mod, route, yy, fn, n1, s2, final)
        else:
            x1, h2 = outs
            x = _ffn_call(h2, x1, mod, ffn_w_gate[jj].astype(BF16), ffn_w_up[jj].astype(BF16),
                          ffn_w_down[jj].astype(BF16), fn, n1, s2, final)

    y_prompt = x[:n1].reshape(b1, s1, d)
    y_sample = x[n1:].reshape(b2, s2, d)
    stack = lambda i: jnp.stack([s[i] for s in states], axis=1)
    return (y_prompt, y_sample, stack(0), stack(1), stack(2), stack(3), stack(4), stack(5))
```

```python
import functools

import jax
import jax.numpy as jnp
from jax import lax
from jax.experimental import pallas as pl
from jax.experimental.pallas import tpu as pltpu

F32 = jnp.float32
BF16 = jnp.bfloat16

N_HEADS = 8
N_KV_HEADS = 2
HEAD_DIM = 64
ML_HEADS = 4
RET_HEADS = 4
GRID_W = 64
CHUNK = 128
ROPE_BASE = 10000.0
EPS = 1e-6
N_EXPERTS = 8
LANES = 128
HALF = 64

ATT_Q_W = N_HEADS * HEAD_DIM
ATT_KV_W = N_KV_HEADS * HEAD_DIM
ML_W = ML_HEADS * HEAD_DIM
RET_W = RET_HEADS * HEAD_DIM
N_GATES = 4 * ML_HEADS

TOK_TILE = 512
SCAN_SEQS = 2
FFN_TILE = 512
MOE_TILE = 512
ATT_Q_TILE = 256
ATT_K_TILE = 512
ATT_SHORT_SEQS = 4
LOG2E = 1.4426950408889634
Q_SCALE = 0.125 * LOG2E
VMEM_LIMIT = 56 * 1024 * 1024

O_AQ, O_AK, O_AV = 0, 512, 640
O_MQ, O_MK, O_MV, O_MO = 768, 1024, 1280, 1536
O_RQ, O_RK, O_RV, O_RG = 1792, 2048, 2304, 2560
O_BG = 2816
O_MG = 5888
N_IN_PAD = 6016


def _cparams(sem, vmem=VMEM_LIMIT):
    return pltpu.CompilerParams(dimension_semantics=sem, vmem_limit_bytes=vmem)


def _lane_iota(shape=(1, LANES)):
    return lax.broadcasted_iota(jnp.int32, shape, len(shape) - 1)


def _head_mean_matrix():
    r = lax.broadcasted_iota(jnp.int32, (LANES, LANES), 0) >> 6
    c = lax.broadcasted_iota(jnp.int32, (LANES, LANES), 1) >> 6
    return jnp.where(r == c, 1.0 / HALF, 0.0).astype(BF16)


def _head_rms(xs, mean_mat):
    t = xs[0].shape[0]
    sq = jnp.concatenate([x * x for x in xs], axis=0)
    hi = sq.astype(BF16)
    lo = (sq - hi.astype(F32)).astype(BF16)
    ms = jnp.dot(jnp.concatenate([hi, lo], axis=0), mean_mat, preferred_element_type=F32)
    ms = ms[:len(xs) * t] + ms[len(xs) * t:]
    return [x * lax.rsqrt(ms[i * t:(i + 1) * t] + EPS) for i, x in enumerate(xs)]


def _rope(y, cos, sin, lane):
    up = pltpu.roll(y, LANES - 16, 1)
    dn = pltpu.roll(y, 16, 1)
    partner = jnp.where((lane & 31) < 16, up, dn)
    return y * cos + partner * sin


def _log_sigmoid(x):
    return jnp.minimum(x, 0.0) - jnp.log1p(jnp.exp(-jnp.abs(x)))


def _sigmoid(x):
    return 1.0 / (1.0 + jnp.exp(-x))


def _store_token_tiles(ref, x):
    for j in range(x.shape[1] // LANES):
        ref[:, j, :] = x[:, j * LANES:(j + 1) * LANES]


def _load_token_tiles(ref):
    return jnp.concatenate([ref[:, j, :] for j in range(ref.shape[1])], axis=1)


def _split3(x):
    h = x.astype(BF16)
    r = x - h.astype(F32)
    m = r.astype(BF16)
    l = (r - m.astype(F32)).astype(BF16)
    return h, m, l


def _ada_kernel(c_ref, w_ref, b_ref, o_ref):
    c = c_ref[...]
    a = (c * _sigmoid(c)).astype(BF16)
    o_ref[...] = jnp.dot(a, w_ref[...].astype(BF16), preferred_element_type=F32) + b_ref[...]


def _ada_call(c8, w_ada, b_ada):
    depth, d, n6 = w_ada.shape
    tn = 1536
    return pl.pallas_call(
        _ada_kernel,
        out_shape=jax.ShapeDtypeStruct((depth, 8, n6), F32),
        grid=(depth, n6 // tn),
        in_specs=[
            pl.BlockSpec((8, d), lambda l, j: (0, 0)),
            pl.BlockSpec((None, d, tn), lambda l, j: (l, 0, j)),
            pl.BlockSpec((None, 1, tn), lambda l, j: (l, 0, j)),
        ],
        out_specs=pl.BlockSpec((None, 8, tn), lambda l, j: (l, 0, j)),
        compiler_params=_cparams(("arbitrary", "arbitrary")),
        name="ada_mod",
    )(c8, w_ada, b_ada.reshape(depth, 1, n6))


def _inproj_kernel(x_ref, mod_ref, gn_ref, w_ref, b_ref, qg_ref, kg_ref, cos_ref, sin_ref,
                   q_ref, k_ref, v_ref, kb_ref, vb_ref, mqkv_ref, rqkv_ref, mo_ref, rg_ref, bg_ref, gcol_ref,
                   grow_ref):
    x = x_ref[...]
    ms = jnp.mean(x * x, axis=-1, keepdims=True)
    h = x * lax.rsqrt(ms + EPS) * gn_ref[...]
    h = h * (1.0 + mod_ref[1:2, :]) + mod_ref[0:1, :]
    hb = h.astype(BF16)

    def seg(a, b):
        return jnp.dot(hb, w_ref[:, a:b], preferred_element_type=F32) + b_ref[:, a:b]

    lane = _lane_iota()
    mean_mat = _head_mean_matrix()
    cos = cos_ref[...]
    sin = sin_ref[...]
    n_q = ATT_Q_W // LANES
    za = seg(O_AQ, O_MQ)
    normed = _head_rms([za[:, s * LANES:(s + 1) * LANES] for s in range(n_q + 1)], mean_mat)
    for s in range(n_q):
        q_ref[s] = (_rope(normed[s] * qg_ref[...], cos, sin, lane) * Q_SCALE).astype(BF16)
    y = _rope(normed[n_q] * kg_ref[...], cos, sin, lane)
    k_ref[...] = y
    kb_ref[...] = y.astype(BF16)
    y = za[:, O_AV:O_AV + LANES]
    v_ref[...] = y
    vb_ref[...] = y.astype(BF16)

    zm = seg(O_MQ, O_RQ)
    mqkv_ref[:, 0:ML_W] = zm[:, 0:ML_W].astype(BF16)
    mqkv_ref[:, ML_W:2 * ML_W] = (zm[:, ML_W:2 * ML_W] * 0.125).astype(BF16)
    mqkv_ref[:, 2 * ML_W:3 * ML_W] = zm[:, 2 * ML_W:3 * ML_W].astype(BF16)
    mo_ref[...] = zm[:, 3 * ML_W:]

    zr = seg(O_RQ, O_BG)
    for s in range(RET_W // LANES):
        sl = slice(s * LANES, (s + 1) * LANES)
        rqkv_ref[:, sl] = _rope(zr[:, sl], cos, sin, lane).astype(BF16)
        slk = slice(RET_W + s * LANES, RET_W + (s + 1) * LANES)
        rqkv_ref[:, slk] = (_rope(zr[:, slk], cos, sin, lane) * 0.125).astype(BF16)
    rqkv_ref[:, 2 * RET_W:3 * RET_W] = zr[:, 2 * RET_W:3 * RET_W].astype(BF16)
    rg_ref[...] = zr[:, 3 * RET_W:]

    zg = seg(O_BG, N_IN_PAD)
    bg_ref[...] = _sigmoid(zg[:, :O_MG - O_BG]).astype(BF16)

    g = zg[:, O_MG - O_BG:]
    is_f = ((lane >> 2) & 1) == 1
    g = jnp.where(is_f, _log_sigmoid(g), g)
    gcol_ref[...] = g
    grow_ref[...] = g.T[0:N_GATES, :]


def _inproj_call(x, mod, gn, w, b, qg, kg, cos_t, sin_t, n1, s2):
    n, d = x.shape
    tm = TOK_TILE
    n1t = n1 // tm
    t2 = s2 // tm

    def mod_row(t):
        return jnp.where(t < n1t, 0, 1 + (t - n1t) // t2)

    def tab_row(t):
        return jnp.where(t < n1t, 0, 1 + (t - n1t) % t2)

    row = lambda w_: pl.BlockSpec((tm, w_), lambda t: (t, 0))
    const = lambda shp: pl.BlockSpec(shp, lambda t: (0,) * len(shp))
    n_slabs = ATT_Q_W // LANES
    out_shape = [
        jax.ShapeDtypeStruct((n_slabs, n, LANES), BF16),
        jax.ShapeDtypeStruct((n, ATT_KV_W), F32),
        jax.ShapeDtypeStruct((n, ATT_KV_W), F32),
        jax.ShapeDtypeStruct((n, ATT_KV_W), BF16),
        jax.ShapeDtypeStruct((n, ATT_KV_W), BF16),
        jax.ShapeDtypeStruct((n, 3 * ML_W), BF16),
        jax.ShapeDtypeStruct((n, 3 * RET_W), BF16),
        jax.ShapeDtypeStruct((n, ML_W), F32),
        jax.ShapeDtypeStruct((n, RET_W), F32),
        jax.ShapeDtypeStruct((n, 3 * d), BF16),
        jax.ShapeDtypeStruct((n, LANES), F32),
        jax.ShapeDtypeStruct((N_GATES, n), F32),
    ]
    out_specs = [pl.BlockSpec((n_slabs, tm, LANES), lambda t: (0, t, 0)), row(ATT_KV_W), row(ATT_KV_W),
                 row(ATT_KV_W), row(ATT_KV_W), row(3 * ML_W), row(3 * RET_W), row(ML_W),
                 row(RET_W), row(3 * d), row(LANES), pl.BlockSpec((N_GATES, tm), lambda t: (0, t))]
    return pl.pallas_call(
        _inproj_kernel,
        out_shape=out_shape,
        grid=(n // tm,),
        in_specs=[
            row(d),
            pl.BlockSpec((None, 6, d), lambda t: (mod_row(t), 0, 0)),
            const((1, d)),
            pl.BlockSpec((d, N_IN_PAD), lambda t: (0, 0), pipeline_mode=pl.Buffered(1)),
            const((1, N_IN_PAD)),
            const((1, LANES)),
            const((1, LANES)),
            pl.BlockSpec((tm, LANES), lambda t: (tab_row(t), 0)),
            pl.BlockSpec((tm, LANES), lambda t: (tab_row(t), 0)),
        ],
        out_specs=out_specs,
        compiler_params=_cparams(("arbitrary",)),
        name="in_proj",
    )(x, mod, gn, w, b, qg, kg, cos_t, sin_t)


def _attn_kernel(q_ref, k_ref, vt_ref, o_ref, s_sc, *, tk, tq):
    nb = k_ref.shape[0]
    nk = k_ref.shape[1] // tk
    row_lo = lax.broadcasted_iota(jnp.int32, (LANES, 1), 0) < HALF

    for bi in range(nb):
        for j in range(q_ref.shape[0]):
            qt = q_ref[j, bi * tq:(bi + 1) * tq, :].astype(F32).T.astype(BF16)
            outs = []
            for half in range(2):
                keep = row_lo if half == 0 else jnp.logical_not(row_lo)
                qm = jnp.where(keep, qt, jnp.zeros_like(qt))
                mx = None
                for c in range(nk):
                    s = jnp.dot(k_ref[bi, c * tk:(c + 1) * tk, :], qm, preferred_element_type=F32)
                    s_sc[half, c * tk:(c + 1) * tk, :] = s
                    cm = jnp.max(s, axis=0, keepdims=True)
                    mx = cm if mx is None else jnp.maximum(mx, cm)
                l = jnp.zeros((1, tq), F32)
                acc = jnp.zeros((LANES, tq), F32)
                for c in range(nk):
                    p = jnp.exp2(s_sc[half, c * tk:(c + 1) * tk, :] - mx)
                    l += jnp.sum(p, axis=0, keepdims=True)
                    acc += jnp.dot(vt_ref[bi, :, c * tk:(c + 1) * tk], p.astype(BF16), preferred_element_type=F32)
                outs.append(acc * (1.0 / l))
            o_ref[j, bi * tq:(bi + 1) * tq, :] = jnp.where(row_lo, outs[0], outs[1]).T.astype(o_ref.dtype)


def _attn_call(q, k, vt, tok_off, sq):
    n_slabs = q.shape[0]
    b, sk, _ = k.shape
    tq = min(ATT_Q_TILE, sq)
    tk = min(ATT_K_TILE, sk)
    nq = sq // tq
    nb = ATT_SHORT_SEQS if (nq == 1 and b % ATT_SHORT_SEQS == 0) else 1
    off = tok_off // (nb * tq)
    return pl.pallas_call(
        functools.partial(_attn_kernel, tk=tk, tq=tq),
        out_shape=jax.ShapeDtypeStruct((n_slabs, b * sq, LANES), BF16),
        grid=(b // nb, nq),
        in_specs=[
            pl.BlockSpec((n_slabs, nb * tq, LANES), lambda i, j: (0, off + i * nq + j, 0)),
            pl.BlockSpec((nb, sk, ATT_KV_W), lambda i, j: (i, 0, 0)),
            pl.BlockSpec((nb, ATT_KV_W, sk), lambda i, j: (i, 0, 0)),
        ],
        out_specs=pl.BlockSpec((n_slabs, nb * tq, LANES), lambda i, j: (0, i * nq + j, 0)),
        scratch_shapes=[pltpu.VMEM((2, sk, tq), F32)],
        compiler_params=_cparams(("arbitrary", "arbitrary")),
        name="attention",
    )(q, k, vt)


def _scan_kernel(*refs, nb):
    L = CHUNK
    seq_in = refs[:8 * nb]
    rd_ref, c0_ref, n0_ref, m0_ref, s0_ref = refs[8 * nb:8 * nb + 5]
    o0 = 8 * nb + 5
    hf_ref, hb_ref, of_ref, ob_ref, cout_ref, nout_ref, mout_ref, sout_ref = refs[o0:o0 + 8]
    c_all, n_all, m_all, s_all, dec_st, qdec_st, kdec_st, cdec_st = refs[o0 + 8:]
    b_idx = pl.program_id(0)
    j = pl.program_id(1)
    nc = pl.num_programs(1)
    lane = _lane_iota()
    lo = lane < HALF
    row_i = lax.broadcasted_iota(jnp.int32, (L, L), 0)
    col_i = lax.broadcasted_iota(jnp.int32, (L, L), 1)
    blockmask = (row_i >> 6) == (col_i >> 6)
    causal = (row_i >= col_i, row_i <= col_i)

    @pl.when(jnp.logical_and(b_idx == 0, j == 0))
    def _init_tables():
        lg = _log_sigmoid(rd_ref[...])
        pos = lax.broadcasted_iota(jnp.int32, (L, 1), 0).astype(F32)
        diff = (row_i - col_i).astype(F32)
        for d in range(2):
            sd = diff if d == 0 else -diff
            for p in range(2):
                qd, kd, cd = [], [], []
                for e in range(2):
                    r = d * RET_HEADS + 2 * p + e
                    g = lg[r:r + 1, 0:1]
                    dec_st[d, p, e * L:(e + 1) * L, :] = jnp.where(sd >= 0, jnp.exp(g * jnp.maximum(sd, 0.0)), 0.0)
                    if d == 0:
                        qd.append(jnp.exp(g * (pos + 1.0)))
                        kd.append(jnp.exp(g * (L - 1.0 - pos)))
                    else:
                        qd.append(jnp.exp(g * (L - pos)))
                        kd.append(jnp.exp(g * pos))
                    cd.append(jnp.exp(g * float(L)))
                qdec_st[d, p] = jnp.where(lo, qd[0], qd[1])
                kdec_st[d, p] = jnp.where(lo, kd[0], kd[1])
                cdec_st[d, p] = jnp.where(lo, cd[0], cd[1])

    @pl.when(j == 0)
    def _load_state():
        c_all[...] = c0_ref[...]
        n_all[...] = n0_ref[...]
        m_all[...] = m0_ref[...]
        s_all[...] = s0_ref[...]

    tri = (row_i >= col_i).astype(BF16)
    triu = (row_i <= col_i).astype(BF16)

    def cumsums(gc_ref, gr_ref, d):
        a_col, a_row = (tri, triu) if d == 0 else (triu, tri)
        col3 = jnp.dot(a_col, jnp.concatenate(_split3(gc_ref[...]), axis=1), preferred_element_type=F32)
        row3 = jnp.dot(jnp.concatenate(_split3(gr_ref[...]), axis=0), a_row, preferred_element_type=F32)
        col = col3[:, 0:LANES] + col3[:, LANES:2 * LANES] + col3[:, 2 * LANES:]
        rowv = row3[0:N_GATES] + row3[N_GATES:2 * N_GATES] + row3[2 * N_GATES:]
        return col, rowv

    pair = lambda x0, x1: jnp.where(lo, x0, x1)
    lo2 = (_lane_iota((1, 2 * LANES)) & (LANES - 1)) < HALF

    keys = [(bb, d, p) for bb in range(nb) for d in range(2) for p in range(2)]
    c_old = {k: c_all[k] for k in keys}
    n_old = {k: n_all[k] for k in keys}
    m_old = {k: m_all[k] for k in keys}
    s_old = {k: s_all[k] for k in keys}
    c_new, n_new, m_new_st, s_new = {}, {}, {}, {}

    for bb in range(nb):
        mf_ref, mb_ref, rf_ref, rb_ref, gcf_ref, gcb_ref, grf_ref, grb_ref = seq_in[8 * bb:8 * bb + 8]
        for d, (m_ref, r_ref, gc_ref, gr_ref, h_out, o_out) in enumerate(
                ((mf_ref, rf_ref, gcf_ref, grf_ref, hf_ref, of_ref),
                 (mb_ref, rb_ref, gcb_ref, grb_ref, hb_ref, ob_ref))):
            gcol = gc_ref[...]
            grow = gr_ref[...]
            cum_col, cum_row = cumsums(gc_ref, gr_ref, d)
            gi = 2 * ML_HEADS * d
            gf = gi + ML_HEADS
            last = L - 1 if d == 0 else 0
            for p in range(2):
                sl = slice(p * LANES, (p + 1) * LANES)
                mq, mk, mv = (m_ref[:, i * ML_W + p * LANES:i * ML_W + (p + 1) * LANES] for i in range(3))
                rq, rk, rv = (r_ref[:, i * RET_W + p * LANES:i * RET_W + (p + 1) * LANES] for i in range(3))
                c2 = c_old[bb, d, p]
                n2 = n_old[bb, d, p]
                m2 = m_old[bb, d, p]
                s2 = s_old[bb, d, p]
                zero = jnp.zeros_like(mq)
                zero_st = jnp.zeros((LANES, LANES), BF16)
                bdiag = lambda x0, x1, z: jnp.concatenate(
                    [jnp.concatenate([x0, z], axis=1), jnp.concatenate([z, x1], axis=1)], axis=0)
                q_cat = jnp.concatenate([mq, rq], axis=1)
                q_st = jnp.dot(q_cat, bdiag(c2.astype(BF16), s2.astype(BF16), zero_st), preferred_element_type=F32)
                q_c = q_st[:, :LANES]
                q_s = q_st[:, LANES:] * qdec_st[d, p]
                q_n = mq.astype(F32) * n2
                hds = (2 * p, 2 * p + 1)
                i_cols = [gcol[:, gi + hd:gi + hd + 1] for hd in hds]
                b_cols = [cum_col[:, gf + hd:gf + hd + 1] for hd in hds]
                b2 = pair(*b_cols)
                i2 = pair(*i_cols)
                a2 = b2 + m2
                dms = []
                for e, hd in enumerate(hds):
                    i_row = grow[gi + hd:gi + hd + 1, :]
                    b_row = cum_row[gf + hd:gf + hd + 1, :]
                    dms.append(jnp.where(causal[d], b_cols[e] - b_row + i_row, -jnp.inf))
                mt2 = jnp.maximum(a2, pair(*[jnp.max(dm, axis=-1, keepdims=True) for dm in dms]))
                wa2 = jnp.exp(a2 - mt2)
                qs = jnp.concatenate([jnp.where(lo2, q_cat, jnp.zeros_like(q_cat)),
                                      jnp.where(lo2, jnp.zeros_like(q_cat), q_cat)], axis=0)
                scores = lax.dot_general(qs, bdiag(mk, rk, zero), (((1,), (1,)), ((), ())),
                                         preferred_element_type=F32)
                wd = jnp.concatenate([jnp.exp(dms[e] - mt2[:, e * HALF:e * HALF + 1]) for e in range(2)], axis=0)
                s_m = scores[:, :L] * wd
                s_r = scores[:, L:] * dec_st[d, p]
                sv = jnp.dot(jnp.concatenate([s_m, s_r], axis=1).astype(BF16), bdiag(mv, rv, zero),
                             preferred_element_type=F32)
                rs = jnp.sum(s_m, axis=-1, keepdims=True)
                qns = [jnp.sum(jnp.where(lo if e == 0 else jnp.logical_not(lo), q_n, 0.0), axis=-1, keepdims=True)
                       for e in range(2)]
                den2 = pair(rs[:L], rs[L:]) + wa2 * pair(*qns)
                dd2 = jnp.maximum(jnp.abs(den2), jnp.exp(-mt2))
                h_out[bb, :, sl] = (pair(sv[:L, :LANES], sv[L:, :LANES]) + wa2 * q_c) / dd2
                o_out[bb, :, sl] = pair(sv[:L, LANES:], sv[L:, LANES:]) + q_s
                b_last = b2[last:last + 1, :]
                g2 = b_last - b2 + i2
                m_new = jnp.maximum(b_last + m2, jnp.max(g2, axis=0, keepdims=True))
                wc2 = jnp.exp(b_last + m2 - m_new)
                kw_m = mk.astype(F32) * jnp.exp(g2 - m_new)
                kw_r = rk.astype(F32) * kdec_st[d, p]
                upd = lax.dot_general(jnp.concatenate([kw_m, kw_r], axis=1).astype(BF16),
                                      jnp.concatenate([mv, rv], axis=1), (((0,), (0,)), ((), ())),
                                      preferred_element_type=F32)
                c_new[bb, d, p] = jnp.where(blockmask, wc2 * c2 + upd[:LANES, :LANES], 0.0)
                n_new[bb, d, p] = wc2 * n2 + jnp.sum(kw_m, axis=0, keepdims=True)
                m_new_st[bb, d, p] = m_new
                s_new[bb, d, p] = jnp.where(blockmask, cdec_st[d, p] * s2 + upd[LANES:, LANES:], 0.0)

    for k in keys:
        c_all[k] = c_new[k]
        n_all[k] = n_new[k]
        m_all[k] = m_new_st[k]
        s_all[k] = s_new[k]

    @pl.when(j == nc - 1)
    def _store_state():
        cout_ref[...] = c_all[...]
        nout_ref[...] = n_all[...]
        mout_ref[...] = m_all[...]
        sout_ref[...] = s_all[...]


def _scan_call(mqkv, rqkv, gcol, grow, rd, c0, n0, m0, s0, tok_off, s):
    b = c0.shape[0]
    nb = SCAN_SEQS
    L = CHUNK
    nc = s // L
    off = tok_off // L
    fwd = lambda w_, bb: pl.BlockSpec((L, w_), lambda i, j: (off + (i * nb + bb) * nc + j, 0))
    bwd = lambda w_, bb: pl.BlockSpec((L, w_), lambda i, j: (off + (i * nb + bb) * nc + nc - 1 - j, 0))
    st = lambda shp: pl.BlockSpec((nb,) + shp, lambda i, j: (i,) + (0,) * len(shp))
    pair_mat = (2, 2, LANES, LANES)
    pair_vec = (2, 2, 1, LANES)
    seq_in_specs, seq_args = [], []
    for bb in range(nb):
        seq_in_specs += [
            fwd(3 * ML_W, bb), bwd(3 * ML_W, bb), fwd(3 * RET_W, bb), bwd(3 * RET_W, bb),
            fwd(LANES, bb), bwd(LANES, bb),
            pl.BlockSpec((N_GATES, L), lambda i, j, bb=bb: (0, off + (i * nb + bb) * nc + j)),
            pl.BlockSpec((N_GATES, L), lambda i, j, bb=bb: (0, off + (i * nb + bb) * nc + nc - 1 - j)),
        ]
        seq_args += [mqkv, mqkv, rqkv, rqkv, gcol, gcol, grow, grow]
    seq_out_shape = [jax.ShapeDtypeStruct((b, s, w_), F32) for w_ in (ML_W, ML_W, RET_W, RET_W)]
    seq_out_specs = [pl.BlockSpec((nb, L, ML_W), lambda i, j: (i, j, 0)),
                     pl.BlockSpec((nb, L, ML_W), lambda i, j: (i, nc - 1 - j, 0)),
                     pl.BlockSpec((nb, L, RET_W), lambda i, j: (i, j, 0)),
                     pl.BlockSpec((nb, L, RET_W), lambda i, j: (i, nc - 1 - j, 0))]
    state_shape = [
        jax.ShapeDtypeStruct((b,) + pair_mat, F32), jax.ShapeDtypeStruct((b,) + pair_vec, F32),
        jax.ShapeDtypeStruct((b,) + pair_vec, F32), jax.ShapeDtypeStruct((b,) + pair_mat, F32),
    ]
    state_specs = [st(pair_mat), st(pair_vec), st(pair_vec), st(pair_mat)]
    outs = pl.pallas_call(
        functools.partial(_scan_kernel, nb=nb),
        out_shape=seq_out_shape + state_shape,
        grid=(b // nb, nc),
        in_specs=seq_in_specs + [pl.BlockSpec((8, LANES), lambda i, j: (0, 0))] + state_specs,
        out_specs=seq_out_specs + state_specs,
        scratch_shapes=[
            pltpu.VMEM((nb,) + pair_mat, F32), pltpu.VMEM((nb,) + pair_vec, F32),
            pltpu.VMEM((nb,) + pair_vec, F32), pltpu.VMEM((nb,) + pair_mat, F32),
            pltpu.VMEM((2, 2, 2 * L, L), F32), pltpu.VMEM((2, 2, L, LANES), F32),
            pltpu.VMEM((2, 2, L, LANES), F32), pltpu.VMEM((2, 2, 1, LANES), F32),
        ],
        compiler_params=_cparams(("arbitrary", "arbitrary")),
        name="scan_mixers",
    )(*seq_args, rd, c0, n0, m0, s0)
    return [o.reshape(b * s, o.shape[-1]) for o in outs[:4]] + list(outs[4:])


def _merge_kernel(x_ref, mod_ref, att1_ref, att2_ref, hf1_ref, hb1_ref, of1_ref, ob1_ref, hf2_ref, hb2_ref,
                  of2_ref, ob2_ref, mo_ref, rg_ref, bg_ref, wb_ref, wo_ref, gn_ref, *rest, moe, n1_tiles):
    sc1_refs = (hf1_ref, hb1_ref, of1_ref, ob1_ref)
    sc2_refs = (hf2_ref, hb2_ref, of2_ref, ob2_ref)
    if moe:
        router_ref, x1_ref, h2_ref, gates_ref = rest
    else:
        x1_ref, h2_ref = rest
    d = x_ref.shape[1]
    first = pl.program_id(0) < n1_tiles
    pick = lambda r1, r2, idx: jnp.where(first, r1[idx], r2[idx])
    mean_mat = _head_mean_matrix()
    att = jnp.concatenate([pick(att1_ref, att2_ref, s) for s in range(ATT_Q_W // LANES)], axis=1)
    y = jnp.dot(att, wb_ref[0:ATT_Q_W, :], preferred_element_type=F32) * bg_ref[:, 0:d].astype(F32)
    n_t = ML_W // LANES
    sums = []
    for s in range(n_t):
        sl = (slice(None), slice(s * LANES, (s + 1) * LANES))
        hf, hb, of, ob = (pick(r1, r2, sl) for r1, r2 in zip(sc1_refs, sc2_refs))
        sums += [hf + hb, of + ob]
    normed = _head_rms(sums, mean_mat)
    ml = jnp.concatenate([(normed[2 * s] * _sigmoid(mo_ref[:, s * LANES:(s + 1) * LANES])).astype(BF16)
                          for s in range(n_t)], axis=1)
    rgs = [rg_ref[:, s * LANES:(s + 1) * LANES] for s in range(n_t)]
    ret = jnp.concatenate([(normed[2 * s + 1] * (rgs[s] * _sigmoid(rgs[s]))).astype(BF16)
                           for s in range(n_t)], axis=1)
    y += jnp.dot(ml, wb_ref[ATT_Q_W:ATT_Q_W + ML_W, :], preferred_element_type=F32) * bg_ref[:, d:2 * d].astype(F32)
    y += jnp.dot(ret, wb_ref[ATT_Q_W + ML_W:, :], preferred_element_type=F32) * bg_ref[:, 2 * d:3 * d].astype(F32)
    y = jnp.dot(y.astype(BF16), wo_ref[...], preferred_element_type=F32)
    x1 = x_ref[...] + mod_ref[2:3, :] * y
    x1_ref[...] = x1
    ms = jnp.mean(x1 * x1, axis=-1, keepdims=True)
    h2 = x1 * lax.rsqrt(ms + EPS) * gn_ref[...]
    h2 = h2 * (1.0 + mod_ref[4:5, :]) + mod_ref[3:4, :]
    if moe:
        _store_token_tiles(h2_ref, h2)
    else:
        h2_ref[...] = h2.astype(BF16)
    if moe:
        hh = h2.astype(BF16)
        hl = (h2 - hh.astype(F32)).astype(BF16)
        r = router_ref[...]
        rh = r.astype(BF16)
        rl = (r - rh.astype(F32)).astype(BF16)
        tm = h2.shape[0]
        prod = jnp.dot(jnp.concatenate([hh, hl], axis=0), jnp.concatenate([rh, rl], axis=1),
                       preferred_element_type=F32)
        logits = (prod[:tm, :LANES] + prod[tm:, :LANES]) + (prod[:tm, LANES:] + prod[tm:, LANES:])
        lane = _lane_iota()
        lg = jnp.where(lane < N_EXPERTS, logits, -jnp.inf)
        m1 = jnp.max(lg, axis=-1, keepdims=True)
        i1 = jnp.min(jnp.where(lg == m1, lane, LANES), axis=-1, keepdims=True)
        sel1 = lane == i1
        lg2 = jnp.where(sel1, -jnp.inf, lg)
        m2 = jnp.max(lg2, axis=-1, keepdims=True)
        i2 = jnp.min(jnp.where(lg2 == m2, lane, LANES), axis=-1, keepdims=True)
        e2 = jnp.exp(m2 - m1)
        den = 1.0 + e2
        gates_ref[...] = jnp.where(lane == 0, i1.astype(F32), jnp.where(
            lane == 1, i2.astype(F32), jnp.where(lane == 2, 1.0 / den, jnp.where(lane == 3, e2 / den, 0.0))))


def _merge_call(x, mod, att1, att2, scan1, scan2, mo, rg, bg, wb, wo, gn, router, n1, s2):
    n, d = x.shape
    tm = TOK_TILE
    n1t = n1 // tm
    t2 = s2 // tm
    moe = router is not None
    n_slabs = att1.shape[0]

    def mod_row(t):
        return jnp.where(t < n1t, 0, 1 + (t - n1t) // t2)

    g1 = lambda t: jnp.minimum(t, n1t - 1)
    g2 = lambda t: jnp.maximum(t - n1t, 0)
    row = lambda w_: pl.BlockSpec((tm, w_), lambda t: (t, 0))
    row1 = lambda w_: pl.BlockSpec((tm, w_), lambda t: (g1(t), 0))
    row2 = lambda w_: pl.BlockSpec((tm, w_), lambda t: (g2(t), 0))
    const = lambda shp: pl.BlockSpec(shp, lambda t: (0,) * len(shp))
    in_specs = [row(d), pl.BlockSpec((None, 6, d), lambda t: (mod_row(t), 0, 0)),
                pl.BlockSpec((n_slabs, tm, LANES), lambda t: (0, g1(t), 0)),
                pl.BlockSpec((n_slabs, tm, LANES), lambda t: (0, g2(t), 0)),
                row1(ML_W), row1(ML_W), row1(RET_W), row1(RET_W),
                row2(ML_W), row2(ML_W), row2(RET_W), row2(RET_W),
                row(ML_W), row(RET_W), row(3 * d),
                const((d, d)), const((d, d)), const((1, d))]
    args = [x, mod, att1, att2, *scan1, *scan2, mo, rg, bg, wb, wo, gn]
    if moe:
        out_shape = [jax.ShapeDtypeStruct((n, d), F32), jax.ShapeDtypeStruct((n, d // LANES, LANES), F32)]
        out_specs = [row(d), pl.BlockSpec((tm, d // LANES, LANES), lambda t: (t, 0, 0))]
    else:
        out_shape = [jax.ShapeDtypeStruct((n, d), F32), jax.ShapeDtypeStruct((n, d), BF16)]
        out_specs = [row(d), row(d)]
    if moe:
        in_specs.append(const((d, LANES)))
        args.append(router)
        out_shape.append(jax.ShapeDtypeStruct((n, LANES), F32))
        out_specs.append(row(LANES))
    return pl.pallas_call(
        functools.partial(_merge_kernel, moe=moe, n1_tiles=n1t),
        out_shape=out_shape,
        grid=(n // tm,),
        in_specs=in_specs,
        out_specs=out_specs,
        compiler_params=_cparams(("arbitrary",)),
        name="merge_out",
    )(*args)


def _final_norm(x, g):
    ms = jnp.mean(x * x, axis=-1, keepdims=True)
    return x * lax.rsqrt(ms + EPS) * g


def _ffn_kernel(h_ref, x1_ref, mod_ref, wg_ref, wu_ref, wd_ref, fn_ref, o_ref, *, n_chunks, final):
    h = h_ref[...]
    f = wg_ref.shape[1]
    fc = f // n_chunks
    acc = jnp.zeros(o_ref.shape, F32)
    for c in range(n_chunks):
        sl = slice(c * fc, (c + 1) * fc)
        g = jnp.dot(h, wg_ref[:, sl], preferred_element_type=F32)
        u = jnp.dot(h, wu_ref[:, sl], preferred_element_type=F32)
        a = (g * _sigmoid(g) * u).astype(BF16)
        acc += jnp.dot(a, wd_ref[sl, :], preferred_element_type=F32)
    x2 = x1_ref[...] + mod_ref[5:6, :] * acc
    o_ref[...] = _final_norm(x2, fn_ref[...]) if final else x2


def _ffn_call(h2, x1, mod, wg, wu, wd, fn, n1, s2, final):
    n, d = x1.shape
    f = wg.shape[1]
    tm = FFN_TILE

    def mod_row(t):
        return jnp.where(t * tm < n1, 0, 1 + (t * tm - n1) // s2)

    row = lambda dt: pl.BlockSpec((tm, d), lambda t: (t, 0))
    const = lambda shp: pl.BlockSpec(shp, lambda t: (0,) * len(shp), pipeline_mode=pl.Buffered(1))
    return pl.pallas_call(
        functools.partial(_ffn_kernel, n_chunks=2, final=final),
        out_shape=jax.ShapeDtypeStruct((n, d), F32),
        grid=(n // tm,),
        in_specs=[row(BF16), row(F32), pl.BlockSpec((None, 6, d), lambda t: (mod_row(t), 0, 0)),
                  const((d, f)), const((d, f)), const((f, d)), pl.BlockSpec((1, d), lambda t: (0, 0))],
        out_specs=row(F32),
        compiler_params=_cparams(("arbitrary",)),
        name="ffn_dense",
    )(h2, x1, mod, wg, wu, wd, fn)


def _route_tables(route, n):
    tmx = MOE_TILE
    e = jnp.concatenate([route[:, 0], route[:, 1]]).astype(jnp.int32)
    oh = (e[:, None] == jnp.arange(N_EXPERTS, dtype=jnp.int32)[None, :]).astype(jnp.int32)
    cs = jnp.cumsum(oh, axis=0)
    counts = cs[-1]
    rank = jnp.sum(oh * cs, axis=1) - 1
    padded = ((counts + tmx - 1) // tmx) * tmx
    pend = jnp.cumsum(padded)
    pos = jnp.sum(oh * (pend - padded)[None, :], axis=1) + rank
    n_rows = 2 * n + N_EXPERTS * tmx
    n_tiles = n_rows // tmx
    inv = jnp.full((n_rows,), -1, jnp.int32).at[pos].set(jnp.arange(2 * n, dtype=jnp.int32), unique_indices=True)
    valid = inv >= 0
    src = jnp.concatenate([jnp.where(valid, inv % n, 0), jnp.zeros((2 * tmx,), jnp.int32)])
    dump = 2 * n + jnp.arange(n_rows, dtype=jnp.int32) % (2 * tmx)
    dst = jnp.concatenate([2 * n + tmx + jnp.arange(tmx, dtype=jnp.int32), jnp.where(valid, inv, dump)])
    tile_e = jnp.sum((jnp.arange(n_tiles + 1, dtype=jnp.int32) * tmx)[:, None] >= pend[None, :], axis=1)
    tile_e = jnp.minimum(tile_e, N_EXPERTS - 1).astype(jnp.int32)
    return src.reshape(n_tiles + 2, 1, tmx), dst.reshape(n_tiles + 1, 1, tmx), tile_e


def _moe_routed_kernel(te_ref, src_ref, srcn_ref, dst_ref, h_hbm, wg_ref, wu_ref, wd_ref, yy_hbm,
                       xbuf, ybuf, sem_in, sem_out):
    del te_ref
    tmx = MOE_TILE
    t = pl.program_id(0)
    last = pl.num_programs(0) - 1
    slot = t % 2
    other = 1 - slot

    def gather(idx_ref, s):
        for r in range(tmx):
            pltpu.make_async_copy(h_hbm.at[idx_ref[0, r]], xbuf.at[s, r], sem_in.at[s]).start(priority=r % 2)

    def wait_gather(s):
        pltpu.make_async_copy(h_hbm.at[pl.ds(0, tmx)], xbuf.at[s], sem_in.at[s]).wait()

    def wait_scatter(s):
        pltpu.make_async_copy(ybuf.at[s], yy_hbm.at[pl.ds(0, tmx)], sem_out.at[s]).wait()

    @pl.when(t == 0)
    def _prologue():
        ybuf[...] = jnp.zeros_like(ybuf)
        gather(src_ref, 0)

    @pl.when(t >= 1)
    def _free_ybuf():
        wait_scatter(slot)

    wait_gather(slot)
    gather(srcn_ref, other)
    for r in range(tmx):
        pltpu.make_async_copy(ybuf.at[other, r], yy_hbm.at[dst_ref[0, r]], sem_out.at[other]).start(priority=r % 2)
    x = _load_token_tiles(xbuf.at[slot]).astype(BF16)
    g = jnp.dot(x, wg_ref[...], preferred_element_type=F32)
    u = jnp.dot(x, wu_ref[...], preferred_element_type=F32)
    a = (g * _sigmoid(g) * u).astype(BF16)
    _store_token_tiles(ybuf.at[slot], jnp.dot(a, wd_ref[...], preferred_element_type=F32))

    @pl.when(t == last)
    def _drain():
        wait_gather(other)
        wait_scatter(other)


def _moe_routed_call(h2, src, dst, tile_e, wg, wu, wd):
    n, nj, _ = h2.shape
    d = nj * LANES
    ne, _, f = wg.shape
    tmx = MOE_TILE
    n_tiles = tile_e.shape[0] - 1
    smem_blk = lambda fn_: pl.BlockSpec((None, 1, tmx), fn_, memory_space=pltpu.SMEM)
    grid_spec = pltpu.PrefetchScalarGridSpec(
        num_scalar_prefetch=1,
        grid=(n_tiles + 1,),
        in_specs=[
            smem_blk(lambda t, te: (t, 0, 0)),
            smem_blk(lambda t, te: (t + 1, 0, 0)),
            smem_blk(lambda t, te: (t, 0, 0)),
            pl.BlockSpec(memory_space=pl.ANY),
            pl.BlockSpec((None, d, f), lambda t, te: (te[t], 0, 0)),
            pl.BlockSpec((None, d, f), lambda t, te: (te[t], 0, 0)),
            pl.BlockSpec((None, f, d), lambda t, te: (te[t], 0, 0)),
        ],
        out_specs=pl.BlockSpec(memory_space=pl.ANY),
        scratch_shapes=[pltpu.VMEM((2, tmx, nj, LANES), F32), pltpu.VMEM((2, tmx, nj, LANES), F32),
                        pltpu.SemaphoreType.DMA((2,)), pltpu.SemaphoreType.DMA((2,))],
    )
    return pl.pallas_call(
        _moe_routed_kernel,
        out_shape=jax.ShapeDtypeStruct((2 * n + 2 * tmx, nj, LANES), F32),
        grid_spec=grid_spec,
        compiler_params=_cparams(("arbitrary",)),
        name="moe_experts",
    )(tile_e, src, src, dst, h2, wg, wu, wd)


def _moe_combine_kernel(x1_ref, mod_ref, route_ref, y1_ref, y2_ref, fn_ref, *o_refs, final, n1_tiles):
    w1 = route_ref[:, 2:3]
    w2 = route_ref[:, 3:4]
    y = w1 * _load_token_tiles(y1_ref) + w2 * _load_token_tiles(y2_ref)
    x2 = x1_ref[...] + mod_ref[5:6, :] * y
    if not final:
        o_refs[0][...] = x2
        return
    res = _final_norm(x2, fn_ref[...])
    t = pl.program_id(0)

    @pl.when(t < n1_tiles)
    def _group1():
        o_refs[0][...] = res

    @pl.when(t >= n1_tiles)
    def _group2():
        o_refs[1][...] = res


def _moe_combine_call(x1, mod, route, yy, fn, n1, s2, final):
    n, d = x1.shape
    tm = FFN_TILE
    nt = n // tm
    n1t = n1 // tm

    def mod_row(t):
        return jnp.where(t * tm < n1, 0, 1 + (t * tm - n1) // s2)

    row = lambda w_: pl.BlockSpec((tm, w_), lambda t: (t, 0))
    if final:
        out_shape = [jax.ShapeDtypeStruct((n1, d), F32), jax.ShapeDtypeStruct((n - n1, d), F32)]
        out_specs = [pl.BlockSpec((tm, d), lambda t: (jnp.minimum(t, n1t - 1), 0)),
                     pl.BlockSpec((tm, d), lambda t: (jnp.maximum(t - n1t, 0), 0))]
    else:
        out_shape = jax.ShapeDtypeStruct((n, d), F32)
        out_specs = row(d)
    return pl.pallas_call(
        functools.partial(_moe_combine_kernel, final=final, n1_tiles=n1t),
        out_shape=out_shape,
        grid=(nt,),
        in_specs=[row(d), pl.BlockSpec((None, 6, d), lambda t: (mod_row(t), 0, 0)), row(LANES),
                  pl.BlockSpec((tm, d // LANES, LANES), lambda t: (t, 0, 0)),
                  pl.BlockSpec((tm, d // LANES, LANES), lambda t: (nt + t, 0, 0)),
                  pl.BlockSpec((1, d), lambda t: (0, 0))],
        out_specs=out_specs,
        compiler_params=_cparams(("arbitrary",)),
        name="moe_combine",
    )(x1, mod, route, yy, yy, fn)


def _pair_q_heads(a, axis):
    g = N_HEADS // N_KV_HEADS
    shp = a.shape
    a = a.reshape(shp[:axis] + (N_KV_HEADS, g, HEAD_DIM) + shp[axis + 1:])
    a = jnp.swapaxes(a, axis, axis + 1)
    return a.reshape(shp)


def _prep_in_weights(w_in, b_in):
    g0 = ATT_Q_W + 2 * ATT_KV_W + 4 * ML_W
    pad = N_IN_PAD - w_in.shape[-1]

    def reorder(a):
        parts = [_pair_q_heads(a[..., :ATT_Q_W], a.ndim - 1), a[..., ATT_Q_W:g0], a[..., g0 + N_GATES:],
                 a[..., g0:g0 + N_GATES], jnp.zeros(a.shape[:-1] + (pad,), a.dtype)]
        return jnp.concatenate(parts, axis=-1)

    return reorder(w_in).astype(BF16), reorder(b_in)[:, None, :]


def _rope_tables(s2, tm):
    pos = jnp.arange(s2)
    rowp = (pos // GRID_W).astype(F32)
    colp = (pos % GRID_W).astype(F32)
    quarter = HEAD_DIM // 4
    inv = ROPE_BASE ** (-jnp.arange(quarter, dtype=F32) / quarter)
    ang_r = rowp[:, None] * inv
    ang_c = colp[:, None] * inv
    cos_h = jnp.concatenate([jnp.cos(ang_r), jnp.cos(ang_r), jnp.cos(ang_c), jnp.cos(ang_c)], axis=1)
    sin_h = jnp.concatenate([-jnp.sin(ang_r), jnp.sin(ang_r), -jnp.sin(ang_c), jnp.sin(ang_c)], axis=1)
    cos_t = jnp.concatenate([cos_h, cos_h], axis=1)
    sin_t = jnp.concatenate([sin_h, sin_h], axis=1)
    cos_t = jnp.concatenate([jnp.ones((tm, LANES), F32), cos_t], axis=0)
    sin_t = jnp.concatenate([jnp.zeros((tm, LANES), F32), sin_t], axis=0)
    return cos_t, sin_t


def _pair_blockdiag(m):
    b, nd, h, d, _ = m.shape
    m = m.reshape(b, nd, h // 2, 2, d, d)
    z = jnp.zeros_like(m[:, :, :, 0])
    top = jnp.concatenate([m[:, :, :, 0], z], axis=-1)
    bot = jnp.concatenate([z, m[:, :, :, 1]], axis=-1)
    return jnp.concatenate([top, bot], axis=-2)


def _pair_unblock(m):
    b, nd, p, _, _ = m.shape
    a = m[:, :, :, :HALF, :HALF]
    c = m[:, :, :, HALF:, HALF:]
    return jnp.stack([a, c], axis=3).reshape(b, nd, 2 * p, HALF, HALF)


def kernel(x_prompt, x_sample, c, cache_attn_k, cache_attn_v, state_mlstm_c, state_mlstm_n, state_mlstm_m,
           state_ret_s, c_ctx, w_ada, b_ada, norm_mix, norm_ffn, w_in, b_in, q_norm, k_norm, ret_decay,
           w_branch, w_out, ffn_w_gate, ffn_w_up, ffn_w_down, moe_router, moe_w_gate, moe_w_up, moe_w_down,
           final_norm):
    b1, s1, d = x_prompt.shape
    b2, s2, _ = x_sample.shape
    depth = w_in.shape[0]
    n1, n2 = b1 * s1, b2 * s2
    assert n1 % TOK_TILE == 0 and s2 % TOK_TILE == 0 and s2 % FFN_TILE == 0 and n1 % FFN_TILE == 0 and b2 + 1 <= 8

    x = jnp.concatenate([x_prompt.reshape(n1, d), x_sample.reshape(n2, d)], axis=0)
    c8 = jnp.concatenate([c_ctx[None, :], c, jnp.zeros((8 - 1 - b2, d), F32)], axis=0)
    mod_all = _ada_call(c8, w_ada, b_ada).reshape(depth, 8, 6, d)

    w_in_p, b_in_p = _prep_in_weights(w_in, b_in)
    cos_t, sin_t = _rope_tables(s2, TOK_TILE)
    wb = jnp.concatenate([_pair_q_heads(w_branch[:, :ATT_Q_W], 1), w_branch[:, ATT_Q_W:]], axis=1).astype(BF16)
    wo = w_out.astype(BF16)
    fn = final_norm[None, :]

    zeros_like_state = lambda shp: jnp.zeros((b1,) + shp, F32)
    states = []
    for l in range(depth):
        mod = mod_all[l]
        qg = jnp.tile(q_norm[l], 2)[None, :]
        kg = jnp.tile(k_norm[l], 2)[None, :]
        q, k, v, kb, vb, mqkv, rqkv, mo, rg, bg, gcol, grow = _inproj_call(
            x, mod, norm_mix[l][None, :], w_in_p[l], b_in_p[l], qg, kg, cos_t, sin_t, n1, s2)

        vt1 = jnp.swapaxes(vb[:n1].reshape(b1, s1, ATT_KV_W), 1, 2)
        att1 = _attn_call(q, kb[:n1].reshape(b1, s1, ATT_KV_W), vt1, 0, s1)
        k2 = jnp.concatenate([cache_attn_k[:, l].reshape(b2, -1, ATT_KV_W).astype(BF16),
                              kb[n1:].reshape(b2, s2, ATT_KV_W)], axis=1)
        v2 = jnp.concatenate([cache_attn_v[:, l].reshape(b2, -1, ATT_KV_W).astype(BF16),
                              vb[n1:].reshape(b2, s2, ATT_KV_W)], axis=1)
        att2 = _attn_call(q, k2, jnp.swapaxes(v2, 1, 2), n1, s2)

        rd = jnp.broadcast_to(ret_decay[l].reshape(2 * RET_HEADS, 1), (2 * RET_HEADS, LANES))
        r1 = _scan_call(mqkv, rqkv, gcol, grow, rd,
                        zeros_like_state((2, 2, LANES, LANES)), zeros_like_state((2, 2, 1, LANES)),
                        zeros_like_state((2, 2, 1, LANES)), zeros_like_state((2, 2, LANES, LANES)), 0, s1)
        c0 = _pair_blockdiag(state_mlstm_c[:, l])
        n0 = state_mlstm_n[:, l].reshape(b2, 2, 2, 1, LANES)
        m0 = jnp.repeat(state_mlstm_m[:, l], HALF, axis=-1).reshape(b2, 2, 2, 1, LANES)
        s0 = _pair_blockdiag(state_ret_s[:, l])
        r2 = _scan_call(mqkv, rqkv, gcol, grow, rd, c0, n0, m0, s0, n1, s2)
        states.append((k[:n1].reshape(b1, s1, N_KV_HEADS, HEAD_DIM), v[:n1].reshape(b1, s1, N_KV_HEADS, HEAD_DIM),
                       _pair_unblock(r1[4]), r1[5].reshape(b1, 2, ML_HEADS, HEAD_DIM), r1[6][:, :, :, 0, ::HALF].reshape(b1, 2, ML_HEADS),
                       _pair_unblock(r1[7])))

        moe = l % 2 == 1
        jj = l // 2
        router = jnp.pad(moe_router[jj], ((0, 0), (0, LANES - N_EXPERTS))) if moe else None
        outs = _merge_call(x, mod, att1, att2, r1[:4], r2[:4], mo, rg, bg, wb[l], wo[l], norm_ffn[l][None, :],
                           router, n1, s2)
        final = l == depth - 1
        if moe:
            x1, h2, route = outs
            src, dst, tile_e = _route_tables(route, n1 + n2)
            yy = _moe_routed_call(h2, src, dst, tile_e, moe_w_gate[jj].astype(BF16), moe_w_up[jj].astype(BF16),
                                  moe_w_down[jj].astype(BF16))
            x = _moe_combine_call(x1, mod, route, yy, fn, n1, s2, final)
        else:
            x1, h2 = outs
            x = _ffn_call(h2, x1, mod, ffn_w_gate[jj].astype(BF16), ffn_w_up[jj].astype(BF16),
                          ffn_w_down[jj].astype(BF16), fn, n1, s2, final)

    y1, y2 = x if isinstance(x, (list, tuple)) else (x[:n1], x[n1:])
    y_prompt = y1.reshape(b1, s1, d)
    y_sample = y2.reshape(b2, s2, d)
    stack = lambda i: jnp.stack([s[i] for s in states], axis=1)
    return (y_prompt, y_sample, stack(0), stack(1), stack(2), stack(3), stack(4), stack(5))
```

```python
import functools

import jax
import jax.numpy as jnp
from jax import lax
from jax.experimental import pallas as pl
from jax.experimental.pallas import tpu as pltpu

F32 = jnp.float32
BF16 = jnp.bfloat16

N_HEADS = 8
N_KV_HEADS = 2
HEAD_DIM = 64
ML_HEADS = 4
RET_HEADS = 4
GRID_W = 64
CHUNK = 128
ROPE_BASE = 10000.0
EPS = 1e-6
N_EXPERTS = 8
LANES = 128
HALF = 64

ATT_Q_W = N_HEADS * HEAD_DIM
ATT_KV_W = N_KV_HEADS * HEAD_DIM
ML_W = ML_HEADS * HEAD_DIM
RET_W = RET_HEADS * HEAD_DIM
N_GATES = 4 * ML_HEADS

TOK_TILE = 512
SCAN_SEQS = 2
FFN_TILE = 512
MOE_TILE = 512
ATT_Q_TILE = 256
ATT_K_TILE = 512
ATT_SHORT_SEQS = 4
LOG2E = 1.4426950408889634
Q_SCALE = 0.125 * LOG2E
VMEM_LIMIT = 56 * 1024 * 1024

O_AQ, O_AK, O_AV = 0, 512, 640
O_MQ, O_MK, O_MV, O_MO = 768, 1024, 1280, 1536
O_RQ, O_RK, O_RV, O_RG = 1792, 2048, 2304, 2560
O_BG = 2816
O_MG = 5888
N_IN_PAD = 6016


def _cparams(sem, vmem=VMEM_LIMIT):
    return pltpu.CompilerParams(dimension_semantics=sem, vmem_limit_bytes=vmem)


def _lane_iota(shape=(1, LANES)):
    return lax.broadcasted_iota(jnp.int32, shape, len(shape) - 1)


def _head_mean_matrix():
    r = lax.broadcasted_iota(jnp.int32, (LANES, LANES), 0) >> 6
    c = lax.broadcasted_iota(jnp.int32, (LANES, LANES), 1) >> 6
    return jnp.where(r == c, 1.0 / HALF, 0.0).astype(BF16)


def _head_rms(xs, mean_mat):
    t = xs[0].shape[0]
    sq = jnp.concatenate([x * x for x in xs], axis=0)
    hi = sq.astype(BF16)
    lo = (sq - hi.astype(F32)).astype(BF16)
    ms = jnp.dot(jnp.concatenate([hi, lo], axis=0), mean_mat, preferred_element_type=F32)
    ms = ms[:len(xs) * t] + ms[len(xs) * t:]
    return [x * lax.rsqrt(ms[i * t:(i + 1) * t] + EPS) for i, x in enumerate(xs)]


def _rope(y, cos, sin, lane):
    up = pltpu.roll(y, LANES - 16, 1)
    dn = pltpu.roll(y, 16, 1)
    partner = jnp.where((lane & 31) < 16, up, dn)
    return y * cos + partner * sin


def _log_sigmoid(x):
    return jnp.minimum(x, 0.0) - jnp.log1p(jnp.exp(-jnp.abs(x)))


def _sigmoid(x):
    return 1.0 / (1.0 + jnp.exp(-x))


def _store_token_tiles(ref, x):
    for j in range(x.shape[1] // LANES):
        ref[:, j, :] = x[:, j * LANES:(j + 1) * LANES]


def _load_token_tiles(ref):
    return jnp.concatenate([ref[:, j, :] for j in range(ref.shape[1])], axis=1)


def _split3(x):
    h = x.astype(BF16)
    r = x - h.astype(F32)
    m = r.astype(BF16)
    l = (r - m.astype(F32)).astype(BF16)
    return h, m, l


def _ada_kernel(c_ref, w_ref, b_ref, o_ref):
    c = c_ref[...]
    a = (c * _sigmoid(c)).astype(BF16)
    o_ref[...] = jnp.dot(a, w_ref[...].astype(BF16), preferred_element_type=F32) + b_ref[...]


def _ada_call(c8, w_ada, b_ada):
    depth, d, n6 = w_ada.shape
    tn = 1536
    return pl.pallas_call(
        _ada_kernel,
        out_shape=jax.ShapeDtypeStruct((depth, 8, n6), F32),
        grid=(depth, n6 // tn),
        in_specs=[
            pl.BlockSpec((8, d), lambda l, j: (0, 0)),
            pl.BlockSpec((None, d, tn), lambda l, j: (l, 0, j)),
            pl.BlockSpec((None, 1, tn), lambda l, j: (l, 0, j)),
        ],
        out_specs=pl.BlockSpec((None, 8, tn), lambda l, j: (l, 0, j)),
        compiler_params=_cparams(("arbitrary", "arbitrary")),
        name="ada_mod",
    )(c8, w_ada, b_ada.reshape(depth, 1, n6))


def _inproj_kernel(x_ref, mod_ref, gn_ref, w_ref, b_ref, qg_ref, kg_ref, cos_ref, sin_ref,
                   q_ref, k_ref, v_ref, kb_ref, vb_ref, mqkv_ref, rqkv_ref, mo_ref, rg_ref, bg_ref, gcol_ref,
                   grow_ref):
    x = x_ref[...]
    ms = jnp.mean(x * x, axis=-1, keepdims=True)
    h = x * lax.rsqrt(ms + EPS) * gn_ref[...]
    h = h * (1.0 + mod_ref[1:2, :]) + mod_ref[0:1, :]
    hb = h.astype(BF16)

    def seg(a, b):
        return jnp.dot(hb, w_ref[:, a:b], preferred_element_type=F32) + b_ref[:, a:b]

    lane = _lane_iota()
    mean_mat = _head_mean_matrix()
    cos = cos_ref[...]
    sin = sin_ref[...]
    n_q = ATT_Q_W // LANES
    za = seg(O_AQ, O_MQ)
    normed = _head_rms([za[:, s * LANES:(s + 1) * LANES] for s in range(n_q + 1)], mean_mat)
    for s in range(n_q):
        q_ref[s] = (_rope(normed[s] * qg_ref[...], cos, sin, lane) * Q_SCALE).astype(BF16)
    y = _rope(normed[n_q] * kg_ref[...], cos, sin, lane)
    k_ref[...] = y
    kb_ref[...] = y.astype(BF16)
    y = za[:, O_AV:O_AV + LANES]
    v_ref[...] = y
    vb_ref[...] = y.astype(BF16)

    zm = seg(O_MQ, O_RQ)
    mqkv_ref[:, 0:ML_W] = zm[:, 0:ML_W].astype(BF16)
    mqkv_ref[:, ML_W:2 * ML_W] = (zm[:, ML_W:2 * ML_W] * 0.125).astype(BF16)
    mqkv_ref[:, 2 * ML_W:3 * ML_W] = zm[:, 2 * ML_W:3 * ML_W].astype(BF16)
    mo_ref[...] = zm[:, 3 * ML_W:]

    zr = seg(O_RQ, O_BG)
    for s in range(RET_W // LANES):
        sl = slice(s * LANES, (s + 1) * LANES)
        rqkv_ref[:, sl] = _rope(zr[:, sl], cos, sin, lane).astype(BF16)
        slk = slice(RET_W + s * LANES, RET_W + (s + 1) * LANES)
        rqkv_ref[:, slk] = (_rope(zr[:, slk], cos, sin, lane) * 0.125).astype(BF16)
    rqkv_ref[:, 2 * RET_W:3 * RET_W] = zr[:, 2 * RET_W:3 * RET_W].astype(BF16)
    rg_ref[...] = zr[:, 3 * RET_W:]

    zg = seg(O_BG, N_IN_PAD)
    bg_ref[...] = _sigmoid(zg[:, :O_MG - O_BG]).astype(BF16)

    g = zg[:, O_MG - O_BG:]
    is_f = ((lane >> 2) & 1) == 1
    g = jnp.where(is_f, _log_sigmoid(g), g)
    gcol_ref[...] = g
    grow_ref[...] = g.T[0:N_GATES, :]


def _inproj_call(x, mod, gn, w, b, qg, kg, cos_t, sin_t, n1, s2):
    n, d = x.shape
    tm = TOK_TILE
    n1t = n1 // tm
    t2 = s2 // tm

    def mod_row(t):
        return jnp.where(t < n1t, 0, 1 + (t - n1t) // t2)

    def tab_row(t):
        return jnp.where(t < n1t, 0, 1 + (t - n1t) % t2)

    row = lambda w_: pl.BlockSpec((tm, w_), lambda t: (t, 0))
    const = lambda shp: pl.BlockSpec(shp, lambda t: (0,) * len(shp))
    n_slabs = ATT_Q_W // LANES
    out_shape = [
        jax.ShapeDtypeStruct((n_slabs, n, LANES), BF16),
        jax.ShapeDtypeStruct((n, ATT_KV_W), F32),
        jax.ShapeDtypeStruct((n, ATT_KV_W), F32),
        jax.ShapeDtypeStruct((n, ATT_KV_W), BF16),
        jax.ShapeDtypeStruct((n, ATT_KV_W), BF16),
        jax.ShapeDtypeStruct((n, 3 * ML_W), BF16),
        jax.ShapeDtypeStruct((n, 3 * RET_W), BF16),
        jax.ShapeDtypeStruct((n, ML_W), F32),
        jax.ShapeDtypeStruct((n, RET_W), F32),
        jax.ShapeDtypeStruct((n, 3 * d), BF16),
        jax.ShapeDtypeStruct((n, LANES), F32),
        jax.ShapeDtypeStruct((N_GATES, n), F32),
    ]
    out_specs = [pl.BlockSpec((n_slabs, tm, LANES), lambda t: (0, t, 0)), row(ATT_KV_W), row(ATT_KV_W),
                 row(ATT_KV_W), row(ATT_KV_W), row(3 * ML_W), row(3 * RET_W), row(ML_W),
                 row(RET_W), row(3 * d), row(LANES), pl.BlockSpec((N_GATES, tm), lambda t: (0, t))]
    return pl.pallas_call(
        _inproj_kernel,
        out_shape=out_shape,
        grid=(n // tm,),
        in_specs=[
            row(d),
            pl.BlockSpec((None, 6, d), lambda t: (mod_row(t), 0, 0)),
            const((1, d)),
            pl.BlockSpec((d, N_IN_PAD), lambda t: (0, 0), pipeline_mode=pl.Buffered(1)),
            const((1, N_IN_PAD)),
            const((1, LANES)),
            const((1, LANES)),
            pl.BlockSpec((tm, LANES), lambda t: (tab_row(t), 0)),
            pl.BlockSpec((tm, LANES), lambda t: (tab_row(t), 0)),
        ],
        out_specs=out_specs,
        compiler_params=_cparams(("arbitrary",)),
        name="in_proj",
    )(x, mod, gn, w, b, qg, kg, cos_t, sin_t)


def _attn_kernel(q_ref, k_ref, vt_ref, o_ref, s_sc, *, tk, tq):
    nb = k_ref.shape[0]
    nk = k_ref.shape[1] // tk
    row_lo = lax.broadcasted_iota(jnp.int32, (LANES, 1), 0) < HALF

    for bi in range(nb):
        for j in range(q_ref.shape[0]):
            qt = q_ref[j, bi * tq:(bi + 1) * tq, :].astype(F32).T.astype(BF16)
            outs = []
            for half in range(2):
                keep = row_lo if half == 0 else jnp.logical_not(row_lo)
                qm = jnp.where(keep, qt, jnp.zeros_like(qt))
                mx = None
                for c in range(nk):
                    s = jnp.dot(k_ref[bi, c * tk:(c + 1) * tk, :], qm, preferred_element_type=F32)
                    s_sc[half, c * tk:(c + 1) * tk, :] = s
                    cm = jnp.max(s, axis=0, keepdims=True)
                    mx = cm if mx is None else jnp.maximum(mx, cm)
                l = jnp.zeros((1, tq), F32)
                acc = jnp.zeros((LANES, tq), F32)
                for c in range(nk):
                    p = jnp.exp2(s_sc[half, c * tk:(c + 1) * tk, :] - mx)
                    l += jnp.sum(p, axis=0, keepdims=True)
                    acc += jnp.dot(vt_ref[bi, :, c * tk:(c + 1) * tk], p.astype(BF16), preferred_element_type=F32)
                outs.append(acc * (1.0 / l))
            o_ref[j, bi * tq:(bi + 1) * tq, :] = jnp.where(row_lo, outs[0], outs[1]).T.astype(o_ref.dtype)


def _attn_call(q, k, vt, tok_off, sq):
    n_slabs = q.shape[0]
    b, sk, _ = k.shape
    tq = min(ATT_Q_TILE, sq)
    tk = min(ATT_K_TILE, sk)
    nq = sq // tq
    nb = ATT_SHORT_SEQS if (nq == 1 and b % ATT_SHORT_SEQS == 0) else 1
    off = tok_off // (nb * tq)
    return pl.pallas_call(
        functools.partial(_attn_kernel, tk=tk, tq=tq),
        out_shape=jax.ShapeDtypeStruct((n_slabs, b * sq, LANES), BF16),
        grid=(b // nb, nq),
        in_specs=[
            pl.BlockSpec((n_slabs, nb * tq, LANES), lambda i, j: (0, off + i * nq + j, 0)),
            pl.BlockSpec((nb, sk, ATT_KV_W), lambda i, j: (i, 0, 0)),
            pl.BlockSpec((nb, ATT_KV_W, sk), lambda i, j: (i, 0, 0)),
        ],
        out_specs=pl.BlockSpec((n_slabs, nb * tq, LANES), lambda i, j: (0, i * nq + j, 0)),
        scratch_shapes=[pltpu.VMEM((2, sk, tq), F32)],
        compiler_params=_cparams(("arbitrary", "arbitrary")),
        name="attention",
    )(q, k, vt)


def _scan_kernel(*refs, nb):
    L = CHUNK
    seq_in = refs[:8 * nb]
    rd_ref, c0_ref, n0_ref, m0_ref, s0_ref = refs[8 * nb:8 * nb + 5]
    o0 = 8 * nb + 5
    hf_ref, hb_ref, of_ref, ob_ref, cout_ref, nout_ref, mout_ref, sout_ref = refs[o0:o0 + 8]
    c_all, n_all, m_all, s_all, dec_st, qdec_st, kdec_st, cdec_st = refs[o0 + 8:]
    b_idx = pl.program_id(0)
    j = pl.program_id(1)
    nc = pl.num_programs(1)
    lane = _lane_iota()
    lo = lane < HALF
    row_i = lax.broadcasted_iota(jnp.int32, (L, L), 0)
    col_i = lax.broadcasted_iota(jnp.int32, (L, L), 1)
    blockmask = (row_i >> 6) == (col_i >> 6)
    causal = (row_i >= col_i, row_i <= col_i)

    @pl.when(jnp.logical_and(b_idx == 0, j == 0))
    def _init_tables():
        lg = _log_sigmoid(rd_ref[...])
        pos = lax.broadcasted_iota(jnp.int32, (L, 1), 0).astype(F32)
        diff = (row_i - col_i).astype(F32)
        for d in range(2):
            sd = diff if d == 0 else -diff
            for p in range(2):
                qd, kd, cd = [], [], []
                for e in range(2):
                    r = d * RET_HEADS + 2 * p + e
                    g = lg[r:r + 1, 0:1]
                    dec_st[d, p, e * L:(e + 1) * L, :] = jnp.where(sd >= 0, jnp.exp(g * jnp.maximum(sd, 0.0)), 0.0)
                    if d == 0:
                        qd.append(jnp.exp(g * (pos + 1.0)))
                        kd.append(jnp.exp(g * (L - 1.0 - pos)))
                    else:
                        qd.append(jnp.exp(g * (L - pos)))
                        kd.append(jnp.exp(g * pos))
                    cd.append(jnp.exp(g * float(L)))
                qdec_st[d, p] = jnp.where(lo, qd[0], qd[1])
                kdec_st[d, p] = jnp.where(lo, kd[0], kd[1])
                cdec_st[d, p] = jnp.where(lo, cd[0], cd[1])

    @pl.when(j == 0)
    def _load_state():
        c_all[...] = c0_ref[...]
        n_all[...] = n0_ref[...]
        m_all[...] = m0_ref[...]
        s_all[...] = s0_ref[...]

    tri = (row_i >= col_i).astype(BF16)
    triu = (row_i <= col_i).astype(BF16)

    def cumsums(gc_ref, gr_ref, d):
        a_col, a_row = (tri, triu) if d == 0 else (triu, tri)
        col3 = jnp.dot(a_col, jnp.concatenate(_split3(gc_ref[...]), axis=1), preferred_element_type=F32)
        row3 = jnp.dot(jnp.concatenate(_split3(gr_ref[...]), axis=0), a_row, preferred_element_type=F32)
        col = col3[:, 0:LANES] + col3[:, LANES:2 * LANES] + col3[:, 2 * LANES:]
        rowv = row3[0:N_GATES] + row3[N_GATES:2 * N_GATES] + row3[2 * N_GATES:]
        return col, rowv

    pair = lambda x0, x1: jnp.where(lo, x0, x1)
    lo2 = (_lane_iota((1, 2 * LANES)) & (LANES - 1)) < HALF

    keys = [(bb, d, p) for bb in range(nb) for d in range(2) for p in range(2)]
    c_old = {k: c_all[k] for k in keys}
    n_old = {k: n_all[k] for k in keys}
    m_old = {k: m_all[k] for k in keys}
    s_old = {k: s_all[k] for k in keys}
    c_new, n_new, m_new_st, s_new = {}, {}, {}, {}

    for bb in range(nb):
        mf_ref, mb_ref, rf_ref, rb_ref, gcf_ref, gcb_ref, grf_ref, grb_ref = seq_in[8 * bb:8 * bb + 8]
        for d, (m_ref, r_ref, gc_ref, gr_ref, h_out, o_out) in enumerate(
                ((mf_ref, rf_ref, gcf_ref, grf_ref, hf_ref, of_ref),
                 (mb_ref, rb_ref, gcb_ref, grb_ref, hb_ref, ob_ref))):
            gcol = gc_ref[...]
            grow = gr_ref[...]
            cum_col, cum_row = cumsums(gc_ref, gr_ref, d)
            gi = 2 * ML_HEADS * d
            gf = gi + ML_HEADS
            last = L - 1 if d == 0 else 0
            for p in range(2):
                sl = slice(p * LANES, (p + 1) * LANES)
                mq, mk, mv = (m_ref[:, i * ML_W + p * LANES:i * ML_W + (p + 1) * LANES] for i in range(3))
                rq, rk, rv = (r_ref[:, i * RET_W + p * LANES:i * RET_W + (p + 1) * LANES] for i in range(3))
                c2 = c_old[bb, d, p]
                n2 = n_old[bb, d, p]
                m2 = m_old[bb, d, p]
                s2 = s_old[bb, d, p]
                zero = jnp.zeros_like(mq)
                zero_st = jnp.zeros((LANES, LANES), BF16)
                bdiag = lambda x0, x1, z: jnp.concatenate(
                    [jnp.concatenate([x0, z], axis=1), jnp.concatenate([z, x1], axis=1)], axis=0)
                q_cat = jnp.concatenate([mq, rq], axis=1)
                q_st = jnp.dot(q_cat, bdiag(c2.astype(BF16), s2.astype(BF16), zero_st), preferred_element_type=F32)
                q_c = q_st[:, :LANES]
                q_s = q_st[:, LANES:] * qdec_st[d, p]
                q_n = mq.astype(F32) * n2
                hds = (2 * p, 2 * p + 1)
                i_cols = [gcol[:, gi + hd:gi + hd + 1] for hd in hds]
                b_cols = [cum_col[:, gf + hd:gf + hd + 1] for hd in hds]
                b2 = pair(*b_cols)
                i2 = pair(*i_cols)
                a2 = b2 + m2
                dms = []
                for e, hd in enumerate(hds):
                    i_row = grow[gi + hd:gi + hd + 1, :]
                    b_row = cum_row[gf + hd:gf + hd + 1, :]
                    dms.append(jnp.where(causal[d], b_cols[e] - b_row + i_row, -jnp.inf))
                mt2 = jnp.maximum(a2, pair(*[jnp.max(dm, axis=-1, keepdims=True) for dm in dms]))
                wa2 = jnp.exp(a2 - mt2)
                qs = jnp.concatenate([jnp.where(lo2, q_cat, jnp.zeros_like(q_cat)),
                                      jnp.where(lo2, jnp.zeros_like(q_cat), q_cat)], axis=0)
                scores = lax.dot_general(qs, bdiag(mk, rk, zero), (((1,), (1,)), ((), ())),
                                         preferred_element_type=F32)
                wd = jnp.concatenate([jnp.exp(dms[e] - mt2[:, e * HALF:e * HALF + 1]) for e in range(2)], axis=0)
                s_m = scores[:, :L] * wd
                s_r = scores[:, L:] * dec_st[d, p]
                sv = jnp.dot(jnp.concatenate([s_m, s_r], axis=1).astype(BF16), bdiag(mv, rv, zero),
                             preferred_element_type=F32)
                rs = jnp.sum(s_m, axis=-1, keepdims=True)
                qns = [jnp.sum(jnp.where(lo if e == 0 else jnp.logical_not(lo), q_n, 0.0), axis=-1, keepdims=True)
                       for e in range(2)]
                den2 = pair(rs[:L], rs[L:]) + wa2 * pair(*qns)
                dd2 = jnp.maximum(jnp.abs(den2), jnp.exp(-mt2))
                h_out[bb, :, sl] = (pair(sv[:L, :LANES], sv[L:, :LANES]) + wa2 * q_c) / dd2
                o_out[bb, :, sl] = pair(sv[:L, LANES:], sv[L:, LANES:]) + q_s
                b_last = b2[last:last + 1, :]
                g2 = b_last - b2 + i2
                m_new = jnp.maximum(b_last + m2, jnp.max(g2, axis=0, keepdims=True))
                wc2 = jnp.exp(b_last + m2 - m_new)
                kw_m = mk.astype(F32) * jnp.exp(g2 - m_new)
                kw_r = rk.astype(F32) * kdec_st[d, p]
                upd = lax.dot_general(jnp.concatenate([kw_m, kw_r], axis=1).astype(BF16),
                                      jnp.concatenate([mv, rv], axis=1), (((0,), (0,)), ((), ())),
                                      preferred_element_type=F32)
                c_new[bb, d, p] = jnp.where(blockmask, wc2 * c2 + upd[:LANES, :LANES], 0.0)
                n_new[bb, d, p] = wc2 * n2 + jnp.sum(kw_m, axis=0, keepdims=True)
                m_new_st[bb, d, p] = m_new
                s_new[bb, d, p] = jnp.where(blockmask, cdec_st[d, p] * s2 + upd[LANES:, LANES:], 0.0)

    for k in keys:
        c_all[k] = c_new[k]
        n_all[k] = n_new[k]
        m_all[k] = m_new_st[k]
        s_all[k] = s_new[k]

    @pl.when(j == nc - 1)
    def _store_state():
        cout_ref[...] = c_all[...]
        nout_ref[...] = n_all[...]
        mout_ref[...] = m_all[...]
        sout_ref[...] = s_all[...]


def _scan_call(mqkv, rqkv, gcol, grow, rd, c0, n0, m0, s0, tok_off, s):
    b = c0.shape[0]
    nb = SCAN_SEQS
    L = CHUNK
    nc = s // L
    off = tok_off // L
    fwd = lambda w_, bb: pl.BlockSpec((L, w_), lambda i, j: (off + (i * nb + bb) * nc + j, 0))
    bwd = lambda w_, bb: pl.BlockSpec((L, w_), lambda i, j: (off + (i * nb + bb) * nc + nc - 1 - j, 0))
    st = lambda shp: pl.BlockSpec((nb,) + shp, lambda i, j: (i,) + (0,) * len(shp))
    pair_mat = (2, 2, LANES, LANES)
    pair_vec = (2, 2, 1, LANES)
    seq_in_specs, seq_args = [], []
    for bb in range(nb):
        seq_in_specs += [
            fwd(3 * ML_W, bb), bwd(3 * ML_W, bb), fwd(3 * RET_W, bb), bwd(3 * RET_W, bb),
            fwd(LANES, bb), bwd(LANES, bb),
            pl.BlockSpec((N_GATES, L), lambda i, j, bb=bb: (0, off + (i * nb + bb) * nc + j)),
            pl.BlockSpec((N_GATES, L), lambda i, j, bb=bb: (0, off + (i * nb + bb) * nc + nc - 1 - j)),
        ]
        seq_args += [mqkv, mqkv, rqkv, rqkv, gcol, gcol, grow, grow]
    seq_out_shape = [jax.ShapeDtypeStruct((b, s, w_), F32) for w_ in (ML_W, ML_W, RET_W, RET_W)]
    seq_out_specs = [pl.BlockSpec((nb, L, ML_W), lambda i, j: (i, j, 0)),
                     pl.BlockSpec((nb, L, ML_W), lambda i, j: (i, nc - 1 - j, 0)),
                     pl.BlockSpec((nb, L, RET_W), lambda i, j: (i, j, 0)),
                     pl.BlockSpec((nb, L, RET_W), lambda i, j: (i, nc - 1 - j, 0))]
    state_shape = [
        jax.ShapeDtypeStruct((b,) + pair_mat, F32), jax.ShapeDtypeStruct((b,) + pair_vec, F32),
        jax.ShapeDtypeStruct((b,) + pair_vec, F32), jax.ShapeDtypeStruct((b,) + pair_mat, F32),
    ]
    state_specs = [st(pair_mat), st(pair_vec), st(pair_vec), st(pair_mat)]
    outs = pl.pallas_call(
        functools.partial(_scan_kernel, nb=nb),
        out_shape=seq_out_shape + state_shape,
        grid=(b // nb, nc),
        in_specs=seq_in_specs + [pl.BlockSpec((8, LANES), lambda i, j: (0, 0))] + state_specs,
        out_specs=seq_out_specs + state_specs,
        scratch_shapes=[
            pltpu.VMEM((nb,) + pair_mat, F32), pltpu.VMEM((nb,) + pair_vec, F32),
            pltpu.VMEM((nb,) + pair_vec, F32), pltpu.VMEM((nb,) + pair_mat, F32),
            pltpu.VMEM((2, 2, 2 * L, L), F32), pltpu.VMEM((2, 2, L, LANES), F32),
            pltpu.VMEM((2, 2, L, LANES), F32), pltpu.VMEM((2, 2, 1, LANES), F32),
        ],
        compiler_params=_cparams(("arbitrary", "arbitrary")),
        name="scan_mixers",
    )(*seq_args, rd, c0, n0, m0, s0)
    return [o.reshape(b * s, o.shape[-1]) for o in outs[:4]] + list(outs[4:])


def _merge_kernel(x_ref, mod_ref, att1_ref, att2_ref, hf1_ref, hb1_ref, of1_ref, ob1_ref, hf2_ref, hb2_ref,
                  of2_ref, ob2_ref, mo_ref, rg_ref, bg_ref, wb_ref, wo_ref, gn_ref, *rest, moe, n1_tiles):
    sc1_refs = (hf1_ref, hb1_ref, of1_ref, ob1_ref)
    sc2_refs = (hf2_ref, hb2_ref, of2_ref, ob2_ref)
    if moe:
        router_ref, x1_ref, h2_ref, gates_ref = rest
    else:
        x1_ref, h2_ref = rest
    d = x_ref.shape[1]
    first = pl.program_id(0) < n1_tiles
    pick = lambda r1, r2, idx: jnp.where(first, r1[idx], r2[idx])
    mean_mat = _head_mean_matrix()
    att = jnp.concatenate([pick(att1_ref, att2_ref, s) for s in range(ATT_Q_W // LANES)], axis=1)
    y = jnp.dot(att, wb_ref[0:ATT_Q_W, :], preferred_element_type=F32) * bg_ref[:, 0:d].astype(F32)
    n_t = ML_W // LANES
    sums = []
    for s in range(n_t):
        sl = (slice(None), slice(s * LANES, (s + 1) * LANES))
        hf, hb, of, ob = (pick(r1, r2, sl) for r1, r2 in zip(sc1_refs, sc2_refs))
        sums += [hf + hb, of + ob]
    normed = _head_rms(sums, mean_mat)
    ml = jnp.concatenate([(normed[2 * s] * _sigmoid(mo_ref[:, s * LANES:(s + 1) * LANES])).astype(BF16)
                          for s in range(n_t)], axis=1)
    rgs = [rg_ref[:, s * LANES:(s + 1) * LANES] for s in range(n_t)]
    ret = jnp.concatenate([(normed[2 * s + 1] * (rgs[s] * _sigmoid(rgs[s]))).astype(BF16)
                           for s in range(n_t)], axis=1)
    y += jnp.dot(ml, wb_ref[ATT_Q_W:ATT_Q_W + ML_W, :], preferred_element_type=F32) * bg_ref[:, d:2 * d].astype(F32)
    y += jnp.dot(ret, wb_ref[ATT_Q_W + ML_W:, :], preferred_element_type=F32) * bg_ref[:, 2 * d:3 * d].astype(F32)
    y = jnp.dot(y.astype(BF16), wo_ref[...], preferred_element_type=F32)
    x1 = x_ref[...] + mod_ref[2:3, :] * y
    x1_ref[...] = x1
    ms = jnp.mean(x1 * x1, axis=-1, keepdims=True)
    h2 = x1 * lax.rsqrt(ms + EPS) * gn_ref[...]
    h2 = h2 * (1.0 + mod_ref[4:5, :]) + mod_ref[3:4, :]
    if moe:
        _store_token_tiles(h2_ref, h2)
    else:
        h2_ref[...] = h2.astype(BF16)
    if moe:
        hh = h2.astype(BF16)
        hl = (h2 - hh.astype(F32)).astype(BF16)
        r = router_ref[...]
        rh = r.astype(BF16)
        rl = (r - rh.astype(F32)).astype(BF16)
        tm = h2.shape[0]
        prod = jnp.dot(jnp.concatenate([hh, hl], axis=0), jnp.concatenate([rh, rl], axis=1),
                       preferred_element_type=F32)
        logits = (prod[:tm, :LANES] + prod[tm:, :LANES]) + (prod[:tm, LANES:] + prod[tm:, LANES:])
        lane = _lane_iota()
        lg = jnp.where(lane < N_EXPERTS, logits, -jnp.inf)
        m1 = jnp.max(lg, axis=-1, keepdims=True)
        i1 = jnp.min(jnp.where(lg == m1, lane, LANES), axis=-1, keepdims=True)
        sel1 = lane == i1
        lg2 = jnp.where(sel1, -jnp.inf, lg)
        m2 = jnp.max(lg2, axis=-1, keepdims=True)
        i2 = jnp.min(jnp.where(lg2 == m2, lane, LANES), axis=-1, keepdims=True)
        e2 = jnp.exp(m2 - m1)
        den = 1.0 + e2
        gates_ref[...] = jnp.where(lane == 0, i1.astype(F32), jnp.where(
            lane == 1, i2.astype(F32), jnp.where(lane == 2, 1.0 / den, jnp.where(lane == 3, e2 / den, 0.0))))


def _merge_call(x, mod, att1, att2, scan1, scan2, mo, rg, bg, wb, wo, gn, router, n1, s2):
    n, d = x.shape
    tm = TOK_TILE
    n1t = n1 // tm
    t2 = s2 // tm
    moe = router is not None
    n_slabs = att1.shape[0]

    def mod_row(t):
        return jnp.where(t < n1t, 0, 1 + (t - n1t) // t2)

    g1 = lambda t: jnp.minimum(t, n1t - 1)
    g2 = lambda t: jnp.maximum(t - n1t, 0)
    row = lambda w_: pl.BlockSpec((tm, w_), lambda t: (t, 0))
    row1 = lambda w_: pl.BlockSpec((tm, w_), lambda t: (g1(t), 0))
    row2 = lambda w_: pl.BlockSpec((tm, w_), lambda t: (g2(t), 0))
    const = lambda shp: pl.BlockSpec(shp, lambda t: (0,) * len(shp))
    in_specs = [row(d), pl.BlockSpec((None, 6, d), lambda t: (mod_row(t), 0, 0)),
                pl.BlockSpec((n_slabs, tm, LANES), lambda t: (0, g1(t), 0)),
                pl.BlockSpec((n_slabs, tm, LANES), lambda t: (0, g2(t), 0)),
                row1(ML_W), row1(ML_W), row1(RET_W), row1(RET_W),
                row2(ML_W), row2(ML_W), row2(RET_W), row2(RET_W),
                row(ML_W), row(RET_W), row(3 * d),
                const((d, d)), const((d, d)), const((1, d))]
    args = [x, mod, att1, att2, *scan1, *scan2, mo, rg, bg, wb, wo, gn]
    if moe:
        out_shape = [jax.ShapeDtypeStruct((n, d), F32), jax.ShapeDtypeStruct((n, d // LANES, LANES), F32)]
        out_specs = [row(d), pl.BlockSpec((tm, d // LANES, LANES), lambda t: (t, 0, 0))]
    else:
        out_shape = [jax.ShapeDtypeStruct((n, d), F32), jax.ShapeDtypeStruct((n, d), BF16)]
        out_specs = [row(d), row(d)]
    if moe:
        in_specs.append(const((d, LANES)))
        args.append(router)
        out_shape.append(jax.ShapeDtypeStruct((n, LANES), F32))
        out_specs.append(row(LANES))
    return pl.pallas_call(
        functools.partial(_merge_kernel, moe=moe, n1_tiles=n1t),
        out_shape=out_shape,
        grid=(n // tm,),
        in_specs=in_specs,
        out_specs=out_specs,
        compiler_params=_cparams(("arbitrary",)),
        name="merge_out",
    )(*args)


def _final_norm(x, g):
    ms = jnp.mean(x * x, axis=-1, keepdims=True)
    return x * lax.rsqrt(ms + EPS) * g


def _ffn_kernel(h_ref, x1_ref, mod_ref, wg_ref, wu_ref, wd_ref, fn_ref, o_ref, *, n_chunks, final):
    h = h_ref[...]
    f = wg_ref.shape[1]
    fc = f // n_chunks
    acc = jnp.zeros(o_ref.shape, F32)
    for c in range(n_chunks):
        sl = slice(c * fc, (c + 1) * fc)
        g = jnp.dot(h, wg_ref[:, sl], preferred_element_type=F32)
        u = jnp.dot(h, wu_ref[:, sl], preferred_element_type=F32)
        a = (g * _sigmoid(g) * u).astype(BF16)
        acc += jnp.dot(a, wd_ref[sl, :], preferred_element_type=F32)
    x2 = x1_ref[...] + mod_ref[5:6, :] * acc
    o_ref[...] = _final_norm(x2, fn_ref[...]) if final else x2


def _ffn_call(h2, x1, mod, wg, wu, wd, fn, n1, s2, final):
    n, d = x1.shape
    f = wg.shape[1]
    tm = FFN_TILE

    def mod_row(t):
        return jnp.where(t * tm < n1, 0, 1 + (t * tm - n1) // s2)

    row = lambda dt: pl.BlockSpec((tm, d), lambda t: (t, 0))
    const = lambda shp: pl.BlockSpec(shp, lambda t: (0,) * len(shp), pipeline_mode=pl.Buffered(1))
    return pl.pallas_call(
        functools.partial(_ffn_kernel, n_chunks=2, final=final),
        out_shape=jax.ShapeDtypeStruct((n, d), F32),
        grid=(n // tm,),
        in_specs=[row(BF16), row(F32), pl.BlockSpec((None, 6, d), lambda t: (mod_row(t), 0, 0)),
                  const((d, f)), const((d, f)), const((f, d)), pl.BlockSpec((1, d), lambda t: (0, 0))],
        out_specs=row(F32),
        compiler_params=_cparams(("arbitrary",)),
        name="ffn_dense",
    )(h2, x1, mod, wg, wu, wd, fn)


def _route_tables(route, n):
    tmx = MOE_TILE
    e = jnp.concatenate([route[:, 0], route[:, 1]]).astype(jnp.int32)
    oh = (e[:, None] == jnp.arange(N_EXPERTS, dtype=jnp.int32)[None, :]).astype(jnp.int32)
    assert oh.shape[0] % tmx == 0
    nblk = oh.shape[0] // tmx
    tri = (jnp.arange(tmx)[:, None] >= jnp.arange(tmx)[None, :]).astype(F32)
    within = jnp.einsum('ts,bse->bte', tri, oh.reshape(nblk, tmx, N_EXPERTS).astype(F32)).astype(jnp.int32)
    before = jnp.cumsum(within[:, -1, :], axis=0) - within[:, -1, :]
    cs = (within + before[:, None, :]).reshape(nblk * tmx, N_EXPERTS)
    counts = cs[-1]
    rank = jnp.sum(oh * cs, axis=1) - 1
    padded = ((counts + tmx - 1) // tmx) * tmx
    pend = jnp.cumsum(padded)
    pos = jnp.sum(oh * (pend - padded)[None, :], axis=1) + rank
    n_rows = 2 * n + N_EXPERTS * tmx
    n_tiles = n_rows // tmx
    inv = jnp.full((n_rows,), -1, jnp.int32).at[pos].set(jnp.arange(2 * n, dtype=jnp.int32), unique_indices=True)
    valid = inv >= 0
    src = jnp.concatenate([jnp.where(valid, inv % n, 0), jnp.zeros((2 * tmx,), jnp.int32)])
    dump = 2 * n + jnp.arange(n_rows, dtype=jnp.int32) % (2 * tmx)
    dst = jnp.concatenate([2 * n + tmx + jnp.arange(tmx, dtype=jnp.int32), jnp.where(valid, inv, dump)])
    tile_e = jnp.sum((jnp.arange(n_tiles + 1, dtype=jnp.int32) * tmx)[:, None] >= pend[None, :], axis=1)
    tile_e = jnp.minimum(tile_e, N_EXPERTS - 1).astype(jnp.int32)
    return src.reshape(n_tiles + 2, 1, tmx), dst.reshape(n_tiles + 1, 1, tmx), tile_e


def _moe_routed_kernel(te_ref, src_ref, srcn_ref, dst_ref, h_hbm, wg_ref, wu_ref, wd_ref, yy_hbm,
                       xbuf, ybuf, sem_in, sem_out):
    del te_ref
    tmx = MOE_TILE
    t = pl.program_id(0)
    last = pl.num_programs(0) - 1
    slot = t % 2
    other = 1 - slot

    def gather(idx_ref, s):
        for r in range(tmx):
            pltpu.make_async_copy(h_hbm.at[idx_ref[0, r]], xbuf.at[s, r], sem_in.at[s]).start(priority=r % 2)

    def wait_gather(s):
        pltpu.make_async_copy(h_hbm.at[pl.ds(0, tmx)], xbuf.at[s], sem_in.at[s]).wait()

    def wait_scatter(s):
        pltpu.make_async_copy(ybuf.at[s], yy_hbm.at[pl.ds(0, tmx)], sem_out.at[s]).wait()

    @pl.when(t == 0)
    def _prologue():
        ybuf[...] = jnp.zeros_like(ybuf)
        gather(src_ref, 0)

    @pl.when(t >= 1)
    def _free_ybuf():
        wait_scatter(slot)

    wait_gather(slot)
    gather(srcn_ref, other)
    for r in range(tmx):
        pltpu.make_async_copy(ybuf.at[other, r], yy_hbm.at[dst_ref[0, r]], sem_out.at[other]).start(priority=r % 2)
    x = _load_token_tiles(xbuf.at[slot]).astype(BF16)
    g = jnp.dot(x, wg_ref[...], preferred_element_type=F32)
    u = jnp.dot(x, wu_ref[...], preferred_element_type=F32)
    a = (g * _sigmoid(g) * u).astype(BF16)
    _store_token_tiles(ybuf.at[slot], jnp.dot(a, wd_ref[...], preferred_element_type=F32))

    @pl.when(t == last)
    def _drain():
        wait_gather(other)
        wait_scatter(other)


def _moe_routed_call(h2, src, dst, tile_e, wg, wu, wd):
    n, nj, _ = h2.shape
    d = nj * LANES
    ne, _, f = wg.shape
    tmx = MOE_TILE
    n_tiles = tile_e.shape[0] - 1
    smem_blk = lambda fn_: pl.BlockSpec((None, 1, tmx), fn_, memory_space=pltpu.SMEM)
    grid_spec = pltpu.PrefetchScalarGridSpec(
        num_scalar_prefetch=1,
        grid=(n_tiles + 1,),
        in_specs=[
            smem_blk(lambda t, te: (t, 0, 0)),
            smem_blk(lambda t, te: (t + 1, 0, 0)),
            smem_blk(lambda t, te: (t, 0, 0)),
            pl.BlockSpec(memory_space=pl.ANY),
            pl.BlockSpec((None, d, f), lambda t, te: (te[t], 0, 0)),
            pl.BlockSpec((None, d, f), lambda t, te: (te[t], 0, 0)),
            pl.BlockSpec((None, f, d), lambda t, te: (te[t], 0, 0)),
        ],
        out_specs=pl.BlockSpec(memory_space=pl.ANY),
        scratch_shapes=[pltpu.VMEM((2, tmx, nj, LANES), F32), pltpu.VMEM((2, tmx, nj, LANES), F32),
                        pltpu.SemaphoreType.DMA((2,)), pltpu.SemaphoreType.DMA((2,))],
    )
    return pl.pallas_call(
        _moe_routed_kernel,
        out_shape=jax.ShapeDtypeStruct((2 * n + 2 * tmx, nj, LANES), F32),
        grid_spec=grid_spec,
        compiler_params=_cparams(("arbitrary",)),
        name="moe_experts",
    )(tile_e, src, src, dst, h2, wg, wu, wd)


def _moe_combine_kernel(x1_ref, mod_ref, route_ref, y1_ref, y2_ref, fn_ref, *o_refs, final, n1_tiles):
    w1 = route_ref[:, 2:3]
    w2 = route_ref[:, 3:4]
    y = w1 * _load_token_tiles(y1_ref) + w2 * _load_token_tiles(y2_ref)
    x2 = x1_ref[...] + mod_ref[5:6, :] * y
    if not final:
        o_refs[0][...] = x2
        return
    res = _final_norm(x2, fn_ref[...])
    t = pl.program_id(0)

    @pl.when(t < n1_tiles)
    def _group1():
        o_refs[0][...] = res

    @pl.when(t >= n1_tiles)
    def _group2():
        o_refs[1][...] = res


def _moe_combine_call(x1, mod, route, yy, fn, n1, s2, final):
    n, d = x1.shape
    tm = FFN_TILE
    nt = n // tm
    n1t = n1 // tm

    def mod_row(t):
        return jnp.where(t * tm < n1, 0, 1 + (t * tm - n1) // s2)

    row = lambda w_: pl.BlockSpec((tm, w_), lambda t: (t, 0))
    if final:
        out_shape = [jax.ShapeDtypeStruct((n1, d), F32), jax.ShapeDtypeStruct((n - n1, d), F32)]
        out_specs = [pl.BlockSpec((tm, d), lambda t: (jnp.minimum(t, n1t - 1), 0)),
                     pl.BlockSpec((tm, d), lambda t: (jnp.maximum(t - n1t, 0), 0))]
    else:
        out_shape = jax.ShapeDtypeStruct((n, d), F32)
        out_specs = row(d)
    return pl.pallas_call(
        functools.partial(_moe_combine_kernel, final=final, n1_tiles=n1t),
        out_shape=out_shape,
        grid=(nt,),
        in_specs=[row(d), pl.BlockSpec((None, 6, d), lambda t: (mod_row(t), 0, 0)), row(LANES),
                  pl.BlockSpec((tm, d // LANES, LANES), lambda t: (t, 0, 0)),
                  pl.BlockSpec((tm, d // LANES, LANES), lambda t: (nt + t, 0, 0)),
                  pl.BlockSpec((1, d), lambda t: (0, 0))],
        out_specs=out_specs,
        compiler_params=_cparams(("arbitrary",)),
        name="moe_combine",
    )(x1, mod, route, yy, yy, fn)


def _pair_q_heads(a, axis):
    g = N_HEADS // N_KV_HEADS
    shp = a.shape
    a = a.reshape(shp[:axis] + (N_KV_HEADS, g, HEAD_DIM) + shp[axis + 1:])
    a = jnp.swapaxes(a, axis, axis + 1)
    return a.reshape(shp)


def _prep_in_weights(w_in, b_in):
    g0 = ATT_Q_W + 2 * ATT_KV_W + 4 * ML_W
    pad = N_IN_PAD - w_in.shape[-1]

    def reorder(a):
        parts = [_pair_q_heads(a[..., :ATT_Q_W], a.ndim - 1), a[..., ATT_Q_W:g0], a[..., g0 + N_GATES:],
                 a[..., g0:g0 + N_GATES], jnp.zeros(a.shape[:-1] + (pad,), a.dtype)]
        return jnp.concatenate(parts, axis=-1)

    return reorder(w_in).astype(BF16), reorder(b_in)[:, None, :]


def _rope_tables(s2, tm):
    pos = jnp.arange(s2)
    rowp = (pos // GRID_W).astype(F32)
    colp = (pos % GRID_W).astype(F32)
    quarter = HEAD_DIM // 4
    inv = ROPE_BASE ** (-jnp.arange(quarter, dtype=F32) / quarter)
    ang_r = rowp[:, None] * inv
    ang_c = colp[:, None] * inv
    cos_h = jnp.concatenate([jnp.cos(ang_r), jnp.cos(ang_r), jnp.cos(ang_c), jnp.cos(ang_c)], axis=1)
    sin_h = jnp.concatenate([-jnp.sin(ang_r), jnp.sin(ang_r), -jnp.sin(ang_c), jnp.sin(ang_c)], axis=1)
    cos_t = jnp.concatenate([cos_h, cos_h], axis=1)
    sin_t = jnp.concatenate([sin_h, sin_h], axis=1)
    cos_t = jnp.concatenate([jnp.ones((tm, LANES), F32), cos_t], axis=0)
    sin_t = jnp.concatenate([jnp.zeros((tm, LANES), F32), sin_t], axis=0)
    return cos_t, sin_t


def _pair_blockdiag(m):
    b, nd, h, d, _ = m.shape
    m = m.reshape(b, nd, h // 2, 2, d, d)
    z = jnp.zeros_like(m[:, :, :, 0])
    top = jnp.concatenate([m[:, :, :, 0], z], axis=-1)
    bot = jnp.concatenate([z, m[:, :, :, 1]], axis=-1)
    return jnp.concatenate([top, bot], axis=-2)


def _pair_unblock(m):
    b, nd, p, _, _ = m.shape
    a = m[:, :, :, :HALF, :HALF]
    c = m[:, :, :, HALF:, HALF:]
    return jnp.stack([a, c], axis=3).reshape(b, nd, 2 * p, HALF, HALF)


def kernel(x_prompt, x_sample, c, cache_attn_k, cache_attn_v, state_mlstm_c, state_mlstm_n, state_mlstm_m,
           state_ret_s, c_ctx, w_ada, b_ada, norm_mix, norm_ffn, w_in, b_in, q_norm, k_norm, ret_decay,
           w_branch, w_out, ffn_w_gate, ffn_w_up, ffn_w_down, moe_router, moe_w_gate, moe_w_up, moe_w_down,
           final_norm):
    b1, s1, d = x_prompt.shape
    b2, s2, _ = x_sample.shape
    depth = w_in.shape[0]
    n1, n2 = b1 * s1, b2 * s2
    assert n1 % TOK_TILE == 0 and s2 % TOK_TILE == 0 and s2 % FFN_TILE == 0 and n1 % FFN_TILE == 0 and b2 + 1 <= 8

    x = jnp.concatenate([x_prompt.reshape(n1, d), x_sample.reshape(n2, d)], axis=0)
    c8 = jnp.concatenate([c_ctx[None, :], c, jnp.zeros((8 - 1 - b2, d), F32)], axis=0)
    mod_all = _ada_call(c8, w_ada, b_ada).reshape(depth, 8, 6, d)

    w_in_p, b_in_p = _prep_in_weights(w_in, b_in)
    cos_t, sin_t = _rope_tables(s2, TOK_TILE)
    wb = jnp.concatenate([_pair_q_heads(w_branch[:, :ATT_Q_W], 1), w_branch[:, ATT_Q_W:]], axis=1).astype(BF16)
    wo = w_out.astype(BF16)
    fn = final_norm[None, :]

    zeros_like_state = lambda shp: jnp.zeros((b1,) + shp, F32)
    states = []
    for l in range(depth):
        mod = mod_all[l]
        qg = jnp.tile(q_norm[l], 2)[None, :]
        kg = jnp.tile(k_norm[l], 2)[None, :]
        q, k, v, kb, vb, mqkv, rqkv, mo, rg, bg, gcol, grow = _inproj_call(
            x, mod, norm_mix[l][None, :], w_in_p[l], b_in_p[l], qg, kg, cos_t, sin_t, n1, s2)

        vt1 = jnp.swapaxes(vb[:n1].reshape(b1, s1, ATT_KV_W), 1, 2)
        att1 = _attn_call(q, kb[:n1].reshape(b1, s1, ATT_KV_W), vt1, 0, s1)
        k2 = jnp.concatenate([cache_attn_k[:, l].reshape(b2, -1, ATT_KV_W).astype(BF16),
                              kb[n1:].reshape(b2, s2, ATT_KV_W)], axis=1)
        v2 = jnp.concatenate([cache_attn_v[:, l].reshape(b2, -1, ATT_KV_W).astype(BF16),
                              vb[n1:].reshape(b2, s2, ATT_KV_W)], axis=1)
        att2 = _attn_call(q, k2, jnp.swapaxes(v2, 1, 2), n1, s2)

        rd = jnp.broadcast_to(ret_decay[l].reshape(2 * RET_HEADS, 1), (2 * RET_HEADS, LANES))
        r1 = _scan_call(mqkv, rqkv, gcol, grow, rd,
                        zeros_like_state((2, 2, LANES, LANES)), zeros_like_state((2, 2, 1, LANES)),
                        zeros_like_state((2, 2, 1, LANES)), zeros_like_state((2, 2, LANES, LANES)), 0, s1)
        c0 = _pair_blockdiag(state_mlstm_c[:, l])
        n0 = state_mlstm_n[:, l].reshape(b2, 2, 2, 1, LANES)
        m0 = jnp.repeat(state_mlstm_m[:, l], HALF, axis=-1).reshape(b2, 2, 2, 1, LANES)
        s0 = _pair_blockdiag(state_ret_s[:, l])
        r2 = _scan_call(mqkv, rqkv, gcol, grow, rd, c0, n0, m0, s0, n1, s2)
        states.append((k[:n1].reshape(b1, s1, N_KV_HEADS, HEAD_DIM), v[:n1].reshape(b1, s1, N_KV_HEADS, HEAD_DIM),
                       _pair_unblock(r1[4]), r1[5].reshape(b1, 2, ML_HEADS, HEAD_DIM), r1[6][:, :, :, 0, ::HALF].reshape(b1, 2, ML_HEADS),
                       _pair_unblock(r1[7])))

        moe = l % 2 == 1
        jj = l // 2
        router = jnp.pad(moe_router[jj], ((0, 0), (0, LANES - N_EXPERTS))) if moe else None
        outs = _merge_call(x, mod, att1, att2, r1[:4], r2[:4], mo, rg, bg, wb[l], wo[l], norm_ffn[l][None, :],
                           router, n1, s2)
        final = l == depth - 1
        if moe:
            x1, h2, route = outs
            src, dst, tile_e = _route_tables(route, n1 + n2)
            yy = _moe_routed_call(h2, src, dst, tile_e, moe_w_gate[jj].astype(BF16), moe_w_up[jj].astype(BF16),
                                  moe_w_down[jj].astype(BF16))
            x = _moe_combine_call(x1, mod, route, yy, fn, n1, s2, final)
        else:
            x1, h2 = outs
            x = _ffn_call(h2, x1, mod, ffn_w_gate[jj].astype(BF16), ffn_w_up[jj].astype(BF16),
                          ffn_w_down[jj].astype(BF16), fn, n1, s2, final)

    y1, y2 = x if isinstance(x, (list, tuple)) else (x[:n1], x[n1:])
    y_prompt = y1.reshape(b1, s1, d)
    y_sample = y2.reshape(b2, s2, d)
    stack = lambda i: jnp.stack([s[i] for s in states], axis=1)
    return (y_prompt, y_sample, stack(0), stack(1), stack(2), stack(3), stack(4), stack(5))
```
